```python
import math
import jax, jax.numpy as jnp
from jax import lax
import numpy as np

D_MODEL = 1024
BATCH = 2
SEQ = 16384
DEPTH = 2

GRID_W = 64
CTX_LEN = 256
N_MOD = 6
EPS = 1e-6

N_POOL_GROUPS = 4
POOL_GROUP_DIM = D_MODEL // 16
POOL_WIDTH = N_POOL_GROUPS * POOL_GROUP_DIM
POOL_WINDOWS = (2, 4, 8, 16)
N_FNO_GROUPS = 4
FNO_GROUP_DIM = D_MODEL // 16
FNO_WIDTH = N_FNO_GROUPS * FNO_GROUP_DIM
N_HEADS = 8
QK_NOPE = 64
QK_ROPE = 32
V_DIM = 64
Q_LORA = 384
KV_LORA = 256
ATTN_WIDTH = N_HEADS * V_DIM
MIX_WIDTH = POOL_WIDTH + FNO_WIDTH + ATTN_WIDTH
COL_FNO = POOL_WIDTH
COL_Q = COL_FNO + FNO_WIDTH
COL_KV = COL_Q + Q_LORA
COL_KR = COL_KV + KV_LORA
IN_WIDTH = COL_KR + QK_ROPE
ROPE_AXIS_DIM = QK_ROPE // 2
ROPE_FREQS = ROPE_AXIS_DIM // 2
ROPE_THETA = 10000.0
SOFTMAX_SCALE = (QK_NOPE + QK_ROPE) ** -0.5
Q_BLOCK = 128
N_EXPERTS = 64
TOP_K = 8
N_EXPERT_GROUPS = 8
TOPK_GROUPS = 4
EXPERTS_PER_GROUP = N_EXPERTS // N_EXPERT_GROUPS
D_EXPERT = 256
D_SHARED = 256
ROUTED_SCALE = 2.5
EXPERT_BLOCK = 128
BLOCKS_PER_STEP = 8

kernel_name = "hybrid_pool_fourier_mla_moe_dit"


def rmsnorm(x, g):
    xf = x.astype(jnp.float32)
    y = xf * lax.rsqrt(jnp.mean(xf * xf, axis=-1, keepdims=True) + EPS)
    return y.astype(x.dtype) * g


def modulate(h, shift, scale):
    return h * (1.0 + scale) + shift


def axial_rope_tables(n_tokens):
    rows = n_tokens // GRID_W
    row = jnp.repeat(jnp.arange(rows, dtype=jnp.float32), GRID_W)
    col = jnp.tile(jnp.arange(GRID_W, dtype=jnp.float32), rows)
    inv_freq = ROPE_THETA ** (-jnp.arange(ROPE_FREQS, dtype=jnp.float32) * 2.0 / ROPE_AXIS_DIM)
    ang = jnp.stack([row[:, None] * inv_freq, col[:, None] * inv_freq], axis=1)
    return jnp.cos(ang), jnp.sin(ang)


def rope2d(x, cos, sin):
    xr = x.reshape(x.shape[:-1] + (2, 2, ROPE_FREQS))
    x1, x2 = xr[..., 0, :], xr[..., 1, :]
    cos, sin = cos.astype(x.dtype), sin.astype(x.dtype)
    out = jnp.stack([x1 * cos - x2 * sin, x2 * cos + x1 * sin], axis=-2)
    return out.reshape(x.shape)


def pool_mixer(u, w_pool, pool_scale):
    B, L, _ = u.shape
    ug = u.astype(jnp.float32).reshape(B, L, N_POOL_GROUPS, POOL_GROUP_DIM)
    cs = jnp.concatenate([jnp.zeros((B, 1, N_POOL_GROUPS, POOL_GROUP_DIM), jnp.float32),
                          jnp.cumsum(ug, axis=1)], axis=1)
    half = jnp.array(POOL_WINDOWS, jnp.int32) // 2
    t = jnp.arange(L, dtype=jnp.int32)[:, None]
    lo = jnp.clip(t - half, 0, L)
    hi = jnp.clip(t + half, 0, L)
    grp = jnp.arange(N_POOL_GROUPS)[None, :]
    win_sum = cs[:, hi, grp] - cs[:, lo, grp]
    count = (hi - lo).astype(jnp.float32)[None, :, :, None]
    d = (win_sum / count - ug).astype(u.dtype)
    y = jnp.einsum('blgc,gcd->blgd', d, w_pool).reshape(B, L, POOL_WIDTH)
    return y * pool_scale


def fourier_mixer(u, w_fno):
    B, L, _ = u.shape
    ug = u.astype(jnp.float32).reshape(B, L, N_FNO_GROUPS, FNO_GROUP_DIM)
    f = jnp.fft.fft2(ug, axes=(1, 3), norm='ortho').real.astype(u.dtype)
    return jnp.einsum('blgc,gcd->blgd', f, w_fno).reshape(B, L, FNO_WIDTH)


def mla_queries(u, g_q, w_uq):
    B, L, _ = u.shape
    cq = rmsnorm(u[..., COL_Q:COL_KV], g_q)
    return (cq @ w_uq).reshape(B, L, N_HEADS, QK_NOPE + QK_ROPE)


def mla_keys_values(u_kvr, g_kv, w_ukv, rope=None):
    B, L, _ = u_kvr.shape
    ckv = rmsnorm(u_kvr[..., :KV_LORA], g_kv)
    kv = (ckv @ w_ukv).reshape(B, L, N_HEADS, QK_NOPE + V_DIM)
    k_rope = u_kvr[..., KV_LORA:]
    if rope is not None:
        k_rope = rope2d(k_rope, rope[0], rope[1])
    k = jnp.concatenate([kv[..., :QK_NOPE],
                         jnp.broadcast_to(k_rope[:, :, None, :], (B, L, N_HEADS, QK_ROPE))], axis=-1)
    return k, kv[..., QK_NOPE:]


def attend(q, k, v):
    s = jnp.einsum('bqhd,bkhd->bhqk', q, k).astype(jnp.float32) * SOFTMAX_SCALE
    p = jax.nn.softmax(s, axis=-1).astype(v.dtype)
    return jnp.einsum('bhqk,bkhd->bqhd', p, v)


def latent_attention(q, k, v, k_ctx, v_ctx):
    B, L, H, Dqk = q.shape
    keys = jnp.concatenate([k_ctx, k], axis=1)
    vals = jnp.concatenate([v_ctx, v], axis=1)
    nb = L // Q_BLOCK
    qb = q.reshape(B, nb, Q_BLOCK, H, Dqk).transpose(1, 0, 2, 3, 4)
    o = lax.map(lambda qblk: attend(qblk, keys, vals), qb)
    return o.transpose(1, 0, 2, 3, 4).reshape(B, L, H * V_DIM)


def head_groups_output(u, attn_out, w_pool, pool_scale, w_fno, w_out):
    y = jnp.concatenate([pool_mixer(u[..., :COL_FNO], w_pool, pool_scale),
                         fourier_mixer(u[..., COL_FNO:COL_Q], w_fno),
                         attn_out], axis=-1)
    return y @ w_out


def route(h, w_router, router_bias):
    T = h.shape[0]
    s = jax.nn.sigmoid((h @ w_router).astype(jnp.float32))
    sel = s + router_bias.astype(jnp.float32)
    grp_score = lax.top_k(sel.reshape(T, N_EXPERT_GROUPS, EXPERTS_PER_GROUP), 2)[0].sum(-1)
    _, top_g = lax.top_k(grp_score, TOPK_GROUPS)
    gmask = jax.nn.one_hot(top_g, N_EXPERT_GROUPS, dtype=jnp.float32).sum(1) > 0
    emask = jnp.repeat(gmask, EXPERTS_PER_GROUP, axis=1)
    _, idx = lax.top_k(jnp.where(emask, sel, -jnp.inf), TOP_K)
    w = jnp.take_along_axis(s, idx, axis=1)
    w = w / jnp.sum(w, axis=-1, keepdims=True) * ROUTED_SCALE
    return idx, w


def moe_ffn(h, w_router, router_bias, w_gate, w_up, w_down, w_sh_gate, w_sh_up, w_sh_down):
    T = h.shape[0]
    idx, wts = route(h, w_router, router_bias)
    n_assign = T * TOP_K
    e_flat = idx.reshape(-1)
    tok_flat = jnp.arange(n_assign, dtype=jnp.int32) // TOP_K
    order = jnp.argsort(e_flat)
    e_sorted = e_flat[order]
    tok_sorted = tok_flat[order]
    w_sorted = wts.reshape(-1)[order]
    counts = jnp.zeros((N_EXPERTS,), jnp.int32).at[e_flat].add(1)
    padded = (counts + EXPERT_BLOCK - 1) // EXPERT_BLOCK * EXPERT_BLOCK
    pad_end = jnp.cumsum(padded)
    pad_start = pad_end - padded
    start = jnp.cumsum(counts) - counts
    dest = pad_start[e_sorted] + jnp.arange(n_assign, dtype=jnp.int32) - start[e_sorted]
    step_rows = EXPERT_BLOCK * BLOCKS_PER_STEP
    n_steps = -(-(n_assign + N_EXPERTS * (EXPERT_BLOCK - 1)) // step_rows)
    n_rows = n_steps * step_rows
    src = jnp.zeros((n_rows,), jnp.int32).at[dest].set(tok_sorted)
    row_w = jnp.zeros((n_rows,), h.dtype).at[dest].set(w_sorted.astype(h.dtype))
    block_start = jnp.arange(n_rows // EXPERT_BLOCK, dtype=jnp.int32) * EXPERT_BLOCK
    block_expert = jnp.minimum(jnp.searchsorted(pad_end, block_start, side='right'), N_EXPERTS - 1)

    def run_step(args):
        src_s, exp_s, w_s = args
        xb = h[src_s]
        a = jnp.einsum('nmd,ndf->nmf', xb, w_gate[exp_s])
        b = jnp.einsum('nmd,ndf->nmf', xb, w_up[exp_s])
        y = jnp.einsum('nmf,nfd->nmd', jax.nn.silu(a) * b, w_down[exp_s])
        return y * w_s[..., None]

    y = lax.map(run_step, (src.reshape(n_steps, BLOCKS_PER_STEP, EXPERT_BLOCK),
                           block_expert.reshape(n_steps, BLOCKS_PER_STEP),
                           row_w.reshape(n_steps, BLOCKS_PER_STEP, EXPERT_BLOCK)))
    routed = jax.ops.segment_sum(y.reshape(n_rows, -1), src, num_segments=T)
    shared = (jax.nn.silu(h @ w_sh_gate) * (h @ w_sh_up)) @ w_sh_down
    return routed + shared


def setup_inputs(seed: int = 0) -> dict:
    key = jax.random.key(seed)
    ks = jax.random.split(key, 26)

    def nrm(k, shape, scale):
        return scale * jax.random.normal(k, shape, jnp.float32)

    def gain(k, shape):
        return 1.0 + 0.1 * jax.random.normal(k, shape, jnp.float32)

    return {
        "x": nrm(ks[0], (BATCH, SEQ, D_MODEL), 1.0),
        "c": nrm(ks[1], (BATCH, D_MODEL), 1.0),
        "ctx": nrm(ks[2], (BATCH, CTX_LEN, D_MODEL), 1.0),
        "c_ctx": nrm(ks[3], (D_MODEL,), 1.0),
        "w_mod": nrm(ks[4], (DEPTH, D_MODEL, N_MOD * D_MODEL), 0.5 * D_MODEL ** -0.5),
        "b_mod": nrm(ks[5], (DEPTH, N_MOD * D_MODEL), 0.02),
        "g_mix": gain(ks[6], (DEPTH, D_MODEL)),
        "g_ffn": gain(ks[7], (DEPTH, D_MODEL)),
        "w_in": nrm(ks[8], (DEPTH, D_MODEL, IN_WIDTH), D_MODEL ** -0.5),
        "w_pool": nrm(ks[9], (DEPTH, N_POOL_GROUPS, POOL_GROUP_DIM, POOL_GROUP_DIM), POOL_GROUP_DIM ** -0.5),
        "pool_scale": gain(ks[10], (DEPTH, POOL_WIDTH)),
        "w_fno": nrm(ks[11], (DEPTH, N_FNO_GROUPS, FNO_GROUP_DIM, FNO_GROUP_DIM), FNO_GROUP_DIM ** -0.5),
        "g_q": gain(ks[12], (DEPTH, Q_LORA)),
        "w_uq": nrm(ks[13], (DEPTH, Q_LORA, N_HEADS * (QK_NOPE + QK_ROPE)), Q_LORA ** -0.5),
        "g_kv": gain(ks[14], (DEPTH, KV_LORA)),
        "w_ukv": nrm(ks[15], (DEPTH, KV_LORA, N_HEADS * (QK_NOPE + V_DIM)), KV_LORA ** -0.5),
        "w_out": nrm(ks[16], (DEPTH, MIX_WIDTH, D_MODEL), MIX_WIDTH ** -0.5),
        "w_router": nrm(ks[17], (DEPTH, D_MODEL, N_EXPERTS), D_MODEL ** -0.5),
        "router_bias": nrm(ks[18], (DEPTH, N_EXPERTS), 0.01),
        "w_gate": nrm(ks[19], (DEPTH, N_EXPERTS, D_MODEL, D_EXPERT), D_MODEL ** -0.5),
        "w_up": nrm(ks[20], (DEPTH, N_EXPERTS, D_MODEL, D_EXPERT), D_MODEL ** -0.5),
        "w_down": nrm(ks[21], (DEPTH, N_EXPERTS, D_EXPERT, D_MODEL), D_EXPERT ** -0.5),
        "w_sh_gate": nrm(ks[22], (DEPTH, D_MODEL, D_SHARED), D_MODEL ** -0.5),
        "w_sh_up": nrm(ks[23], (DEPTH, D_MODEL, D_SHARED), D_MODEL ** -0.5),
        "w_sh_down": nrm(ks[24], (DEPTH, D_SHARED, D_MODEL), D_SHARED ** -0.5),
        "g_final": gain(ks[25], (D_MODEL,)),
    }


def reference(x, c, ctx, c_ctx, w_mod, b_mod, g_mix, g_ffn, w_in, w_pool, pool_scale, w_fno,
              g_q, w_uq, g_kv, w_ukv, w_out, w_router, router_bias, w_gate, w_up, w_down,
              w_sh_gate, w_sh_up, w_sh_down, g_final):
    B, L, D = x.shape
    Lc = ctx.shape[1]
    cos, sin = axial_rope_tables(L)
    silu_c = jax.nn.silu(c)
    silu_cc = jax.nn.silu(c_ctx)
    xc = ctx
    for l in range(DEPTH):
        last = l == DEPTH - 1
        mod = silu_c @ w_mod[l] + b_mod[l]
        modc = silu_cc @ w_mod[l] + b_mod[l]
        sh1, sc1, gt1, sh2, sc2, gt2 = [m[:, None, :] for m in jnp.split(mod, N_MOD, axis=-1)]
        shc1, scc1, gtc1, shc2, scc2, gtc2 = jnp.split(modc, N_MOD, axis=-1)
        w_in_l = w_in[l]

        h = modulate(rmsnorm(x, g_mix[l]), sh1, sc1)
        hc = modulate(rmsnorm(xc, g_mix[l]), shc1, scc1)
        u = h @ w_in_l
        if last:
            uc_kvr = hc @ w_in_l[:, COL_KV:]
        else:
            uc = hc @ w_in_l
            uc_kvr = uc[..., COL_KV:]
        k_ctx, v_ctx = mla_keys_values(uc_kvr, g_kv[l], w_ukv[l])
        q = mla_queries(u, g_q[l], w_uq[l])
        q = jnp.concatenate([q[..., :QK_NOPE], rope2d(q[..., QK_NOPE:], cos[:, None], sin[:, None])], axis=-1)
        k, v = mla_keys_values(u[..., COL_KV:], g_kv[l], w_ukv[l], rope=(cos, sin))
        attn = latent_attention(q, k, v, k_ctx, v_ctx)
        y = head_groups_output(u, attn, w_pool[l], pool_scale[l], w_fno[l], w_out[l])
        x = x + gt1 * y
        if not last:
            qc = mla_queries(uc, g_q[l], w_uq[l])
            attn_c = attend(qc, k_ctx, v_ctx).reshape(B, Lc, ATTN_WIDTH)
            yc = head_groups_output(uc, attn_c, w_pool[l], pool_scale[l], w_fno[l], w_out[l])
            xc = xc + gtc1 * yc

        h2 = modulate(rmsnorm(x, g_ffn[l]), sh2, sc2)
        if last:
            out = moe_ffn(h2.reshape(B * L, D), w_router[l], router_bias[l], w_gate[l], w_up[l],
                          w_down[l], w_sh_gate[l], w_sh_up[l], w_sh_down[l])
            x = x + gt2 * out.reshape(B, L, D)
        else:
            hc2 = modulate(rmsnorm(xc, g_ffn[l]), shc2, scc2)
            tokens = jnp.concatenate([h2.reshape(B * L, D), hc2.reshape(B * Lc, D)], axis=0)
            out = moe_ffn(tokens, w_router[l], router_bias[l], w_gate[l], w_up[l],
                          w_down[l], w_sh_gate[l], w_sh_up[l], w_sh_down[l])
            x = x + gt2 * out[:B * L].reshape(B, L, D)
            xc = xc + gtc2 * out[B * L:].reshape(B, Lc, D)
    return rmsnorm(x, g_final)
```

```python
import functools
import math

import jax
import jax.numpy as jnp
import numpy as np
from jax import lax
from jax.experimental import pallas as pl
from jax.experimental.pallas import tpu as pltpu

F32 = jnp.float32
BF16 = jnp.bfloat16
HIGHEST = lax.Precision.HIGHEST

EPS = 1e-6
N_MOD = 6
GRID_W = 64
POOL_WINDOWS = (2, 4, 8, 16)
GROUP_DIM = 64
N_GROUPS = 4
MIX_GROUP_WIDTH = N_GROUPS * GROUP_DIM
N_HEADS = 8
QK_NOPE = 64
QK_ROPE = 32
V_DIM = 64
Q_LORA = 384
KV_LORA = 256
ROPE_FREQS = QK_ROPE // 4
ROPE_THETA = 10000.0
SOFTMAX_SCALE = (QK_NOPE + QK_ROPE) ** -0.5
N_EXPERTS = 64
TOP_K = 8
N_EXPERT_GROUPS = 8
TOPK_GROUPS = 4
EXPERTS_PER_GROUP = N_EXPERTS // N_EXPERT_GROUPS
ROUTED_SCALE = 2.5

LANES = 128
SUBLANES = 8
HEAD_PAD = LANES
VMEM_LIMIT = 56 * 1024 * 1024

COL_FNO = MIX_GROUP_WIDTH
COL_Q = COL_FNO + MIX_GROUP_WIDTH
COL_KV = COL_Q + Q_LORA
COL_KR = COL_KV + KV_LORA
IN_PAD = COL_KR + HEAD_PAD
ROPE_LANE0 = QK_NOPE

EXPERT_BLOCK = 128


def _cparams(sem, vmem=VMEM_LIMIT):
    return pltpu.CompilerParams(dimension_semantics=sem, vmem_limit_bytes=vmem)


def _rms(x, g):
    return x * lax.rsqrt(jnp.mean(x * x, axis=-1, keepdims=True) + EPS) * g


def _silu(x):
    return x * jax.nn.sigmoid(x)


def _mod_kernel(c_ref, w_ref, b_ref, o_ref):
    s = _silu(c_ref[...])
    o_ref[0] = jnp.dot(s, w_ref[0], preferred_element_type=F32, precision=HIGHEST) + b_ref[0]


def _modulation(cc, w_mod, b_mod):
    depth, d, nd = w_mod.shape
    n_chunks = nd // d
    return pl.pallas_call(
        _mod_kernel,
        grid=(depth, n_chunks),
        in_specs=[pl.BlockSpec((SUBLANES, d), lambda l, j: (0, 0)),
                  pl.BlockSpec((1, d, d), lambda l, j: (l, 0, j)),
                  pl.BlockSpec((1, 1, d), lambda l, j: (l, 0, j))],
        out_specs=pl.BlockSpec((1, SUBLANES, d), lambda l, j: (l, 0, j)),
        out_shape=jax.ShapeDtypeStruct((depth, SUBLANES, nd), F32),
        compiler_params=_cparams(("parallel", "parallel")),
        name="modulation",
    )(cc, w_mod, b_mod.reshape(depth, 1, nd))


def _rope(t, c, s):
    lane = lax.broadcasted_iota(jnp.int32, t.shape, 1)
    first_half = (lane % (2 * ROPE_FREQS)) < ROPE_FREQS
    partner = jnp.where(first_half, pltpu.roll(t, LANES - ROPE_FREQS, 1), pltpu.roll(t, ROPE_FREQS, 1))
    return t * c + partner * s


def _inproj_kernel(x_ref, sh_ref, sc_ref, g_ref, win_ref, gq_ref, wuq_ref, gkv_ref, wukv_ref,
                   cos_ref, sin_ref, up_ref, uf_ref, q_ref, k_ref, v_ref):
    x = x_ref[0]
    h = _rms(x, g_ref[...]) * (1.0 + sc_ref[0]) + sh_ref[0]
    u = jnp.dot(h.astype(BF16), win_ref[...], preferred_element_type=F32)
    up_ref[0] = u[:, :COL_FNO]
    uf_ref[0] = u[:, COL_FNO:COL_Q]
    cq = _rms(u[:, COL_Q:COL_KV], gq_ref[...])
    q = jnp.dot(cq.astype(BF16), wuq_ref[...], preferred_element_type=F32)
    ckv = _rms(u[:, COL_KV:COL_KR], gkv_ref[...])
    kv = jnp.dot(ckv.astype(BF16), wukv_ref[...], preferred_element_type=F32)
    cos = cos_ref[...]
    sin = sin_ref[...]
    kr = _rope(u[:, COL_KR:IN_PAD], cos, sin)
    lane = lax.broadcasted_iota(jnp.int32, kr.shape, 1)
    ones_col = (lane == V_DIM).astype(F32)
    q_scale = SOFTMAX_SCALE * math.log2(math.e)
    for hd in range(N_HEADS):
        lo = hd * HEAD_PAD
        q_ref[0, hd] = (_rope(q[:, lo:lo + HEAD_PAD], cos, sin) * q_scale).astype(BF16)
        k_ref[0, hd] = (kv[:, lo:lo + HEAD_PAD] + kr).astype(BF16)
        vo = N_HEADS * HEAD_PAD + lo
        v_ref[0, hd] = (kv[:, vo:vo + HEAD_PAD] + ones_col).astype(BF16)


def _in_projection(x, sh, sc, g, w_in_p, g_q, w_uq_p, g_kv, w_ukv_p, cos_t, sin_t, tm):
    b, l, d = x.shape
    tm = min(tm, l)
    full = lambda a: pl.BlockSpec(a.shape, lambda bi, i: (0,) * a.ndim)
    vec = pl.BlockSpec((1, 1, d), lambda bi, i: (bi, 0, 0))
    head_out = pl.BlockSpec((1, N_HEADS, tm, HEAD_PAD), lambda bi, i: (bi, 0, i, 0))
    head_shape = jax.ShapeDtypeStruct((b, N_HEADS, l, HEAD_PAD), BF16)
    return pl.pallas_call(
        _inproj_kernel,
        grid=(b, l // tm),
        in_specs=[pl.BlockSpec((1, tm, d), lambda bi, i: (bi, i, 0)), vec, vec,
                  full(g), full(w_in_p), full(g_q), full(w_uq_p), full(g_kv), full(w_ukv_p),
                  pl.BlockSpec((tm, HEAD_PAD), lambda bi, i: (i, 0)),
                  pl.BlockSpec((tm, HEAD_PAD), lambda bi, i: (i, 0))],
        out_specs=[pl.BlockSpec((1, tm, MIX_GROUP_WIDTH), lambda bi, i: (bi, i, 0)),
                   pl.BlockSpec((1, tm, MIX_GROUP_WIDTH), lambda bi, i: (bi, i, 0)),
                   head_out, head_out, head_out],
        out_shape=[jax.ShapeDtypeStruct((b, l, MIX_GROUP_WIDTH), F32),
                   jax.ShapeDtypeStruct((b, l, MIX_GROUP_WIDTH), F32),
                   head_shape, head_shape, head_shape],
        compiler_params=_cparams(("parallel", "parallel")),
        name="in_projection",
    )(x, sh, sc, g, w_in_p, g_q, w_uq_p, g_kv, w_ukv_p, cos_t, sin_t)


ATTN_HEADS_PER_STEP = 2


def _attn_kernel(*refs, tk, n_chunks):
    if n_chunks:
        q_ref, kc_ref, vc_ref, k_ref, v_ref, o_ref = refs
    else:
        q_ref, kc_ref, vc_ref, o_ref = refs
    tq = q_ref.shape[2]
    nt = (((1,), (1,)), ((), ()))
    for hh in range(ATTN_HEADS_PER_STEP):
        q = q_ref[0, hh]

        def step(kc, vc, carry, q=q):
            m, acc = carry
            s = lax.dot_general(q, kc, nt, preferred_element_type=F32)
            m_new = jnp.maximum(m, jnp.max(s, axis=-1, keepdims=True))
            alpha = jnp.exp2(m - m_new)
            p = jnp.exp2(s - m_new).astype(BF16)
            acc = alpha * acc + jnp.dot(p, vc, preferred_element_type=F32)
            return m_new, acc

        carry = (jnp.full((tq, 1), -1e30, F32), jnp.zeros((tq, HEAD_PAD), F32))
        carry = step(kc_ref[0, hh], vc_ref[0, hh], carry)
        if n_chunks:
            def body(j, carry, hh=hh, step=step):
                off = pl.multiple_of(j * tk, tk)
                return step(k_ref[0, hh, pl.ds(off, tk), :], v_ref[0, hh, pl.ds(off, tk), :], carry)

            carry = lax.fori_loop(0, n_chunks, body, carry)
        _, acc = carry
        out = acc[:, :V_DIM] / acc[:, V_DIM:V_DIM + 1]
        o_ref[0, :, hh * V_DIM:(hh + 1) * V_DIM] = out.astype(o_ref.dtype)


def _attention(q, k_ctx, v_ctx, k=None, v=None, *, tq, tk):
    b, h, l, _ = q.shape
    lc = k_ctx.shape[2]
    tq = min(tq, l)
    hps = ATTN_HEADS_PER_STEP
    qspec = pl.BlockSpec((1, hps, tq, HEAD_PAD), lambda bi, hi, i: (bi, hi, i, 0))
    cspec = pl.BlockSpec((1, hps, lc, HEAD_PAD), lambda bi, hi, i: (bi, hi, 0, 0))
    args, specs, n_chunks = [q, k_ctx, v_ctx], [qspec, cspec, cspec], 0
    if k is not None:
        lk = k.shape[2]
        tk = min(tk, lk)
        n_chunks = lk // tk
        kspec = pl.BlockSpec((1, hps, lk, HEAD_PAD), lambda bi, hi, i: (bi, hi, 0, 0))
        args += [k, v]
        specs += [kspec, kspec]
    return pl.pallas_call(
        functools.partial(_attn_kernel, tk=tk, n_chunks=n_chunks),
        grid=(b, h // hps, l // tq),
        in_specs=specs,
        out_specs=pl.BlockSpec((1, tq, hps * V_DIM), lambda bi, hi, i: (bi, i, hi)),
        out_shape=jax.ShapeDtypeStruct((b, l, h * V_DIM), BF16),
        compiler_params=_cparams(("parallel", "parallel", "arbitrary")),
        name="attention",
    )(*args)


def _dft_step1_kernel(x_ref, c_ref, s_ref, yr_ref, yi_ref):
    xb = x_ref[0].astype(BF16)
    yr_ref[0] = jnp.dot(c_ref[...], xb, preferred_element_type=F32)
    yi_ref[0] = jnp.dot(s_ref[...], xb, preferred_element_type=F32)


def _dft_step2_kernel(yr_ref, yi_ref, tc_ref, ts_ref, c_ref, s_ref, cc_ref, sc_ref, o_ref):
    for j in range(SUBLANES):
        yr, yi = yr_ref[0, j], yi_ref[0, j]
        tc, ts = tc_ref[j], ts_ref[j]
        zr = (yr * tc - yi * ts).astype(BF16)
        zi = (yi * tc + yr * ts).astype(BF16)
        a = (jnp.dot(c_ref[...], zr, preferred_element_type=F32)
             - jnp.dot(s_ref[...], zi, preferred_element_type=F32))
        bm = (jnp.dot(s_ref[...], zr, preferred_element_type=F32)
              + jnp.dot(c_ref[...], zi, preferred_element_type=F32))
        o_ref[0, :, j, :] = (jnp.dot(a.astype(BF16), cc_ref[...], preferred_element_type=F32)
                             - jnp.dot(bm.astype(BF16), sc_ref[...], preferred_element_type=F32))


def _dft_dense_kernel(x_ref, c_ref, s_ref, cc_ref, sc_ref, o_ref):
    xb = x_ref[0].astype(BF16)
    a = jnp.dot(c_ref[...], xb, preferred_element_type=F32)
    bm = jnp.dot(s_ref[...], xb, preferred_element_type=F32)
    o_ref[0] = (jnp.dot(a.astype(BF16), cc_ref[...], preferred_element_type=F32)
                - jnp.dot(bm.astype(BF16), sc_ref[...], preferred_element_type=F32))


def _dft_mats(n):
    ang = 2.0 * np.pi * np.outer(np.arange(n), np.arange(n)) / n
    return np.cos(ang), np.sin(ang)


def _channel_dft(l):
    c, s = _dft_mats(GROUP_DIM)
    eye = np.eye(N_GROUPS)
    norm = 1.0 / math.sqrt(l * GROUP_DIM)
    return (jnp.asarray(np.kron(eye, c) * norm, BF16), jnp.asarray(np.kron(eye, s) * norm, BF16))


def _dft_factors(l):
    n1 = 1 << (int(math.log2(l)) // 2)
    return n1, l // n1


def _fourier(u_fno, lane_block=4096):
    b, l, c = u_fno.shape
    cc, sc = _channel_dft(l)
    full = lambda a: pl.BlockSpec(a.shape, lambda *_: (0,) * a.ndim)
    if l <= 512:
        cm, sm = _dft_mats(l)
        cm, sm = jnp.asarray(cm, BF16), jnp.asarray(sm, BF16)
        blk = pl.BlockSpec((1, l, c), lambda bi: (bi, 0, 0))
        return pl.pallas_call(
            _dft_dense_kernel, grid=(b,),
            in_specs=[blk, full(cm), full(sm), full(cc), full(sc)],
            out_specs=blk, out_shape=jax.ShapeDtypeStruct((b, l, c), F32),
            compiler_params=_cparams(("parallel",)), name="dft_dense",
        )(u_fno, cm, sm, cc, sc)
    n1, n2 = _dft_factors(l)
    c1, s1 = _dft_mats(n1)
    c2, s2 = _dft_mats(n2)
    c1, s1, c2, s2 = (jnp.asarray(m, BF16) for m in (c1, s1, c2, s2))
    ang = 2.0 * np.pi * np.outer(np.arange(n1), np.arange(n2)) / l
    tc = jnp.asarray(np.cos(ang)[:, :, None], F32)
    ts = jnp.asarray(np.sin(ang)[:, :, None], F32)
    w = n2 * c
    tn = min(lane_block, w)
    x2 = u_fno.reshape(b, n1, w)
    yspec = pl.BlockSpec((1, n1, tn), lambda bi, i: (bi, 0, i))
    yr, yi = pl.pallas_call(
        _dft_step1_kernel, grid=(b, w // tn),
        in_specs=[yspec, full(c1), full(s1)],
        out_specs=[yspec, yspec],
        out_shape=[jax.ShapeDtypeStruct((b, n1, w), F32)] * 2,
        compiler_params=_cparams(("parallel", "parallel")), name="dft_step1",
    )(x2, c1, s1)
    yr = yr.reshape(b, n1, n2, c)
    yi = yi.reshape(b, n1, n2, c)
    slab = pl.BlockSpec((1, SUBLANES, n2, c), lambda bi, i: (bi, i, 0, 0))
    tw = pl.BlockSpec((SUBLANES, n2, 1), lambda bi, i: (i, 0, 0))
    out = pl.pallas_call(
        _dft_step2_kernel, grid=(b, n1 // SUBLANES),
        in_specs=[slab, slab, tw, tw, full(c2), full(s2), full(cc), full(sc)],
        out_specs=pl.BlockSpec((1, n2, SUBLANES, c), lambda bi, i: (bi, 0, i, 0)),
        out_shape=jax.ShapeDtypeStruct((b, n2, n1, c), F32),
        compiler_params=_cparams(("parallel", "parallel")), name="dft_step2",
    )(yr, yi, tc, ts, c2, s2, cc, sc)
    return out.reshape(b, l, c)


POOL_HALO = SUBLANES


def _mixout_kernel(x_ref, gt_ref, up_ref, prev_ref, next_ref, f_ref, at_ref,
                   wp_ref, ps_ref, wf_ref, wo_ref, o_ref, *, seq_len):
    i = pl.program_id(1)
    tm = x_ref.shape[1]
    u = up_ref[0]
    prev = jnp.where(i > 0, prev_ref[0], 0.0)
    nxt = jnp.where(i < pl.num_programs(1) - 1, next_ref[0], 0.0)
    p = jnp.concatenate([prev, u, nxt], axis=0)
    n = tm + 2 * POOL_HALO
    s1 = pltpu.roll(p, 1, 0) + p
    s2 = pltpu.roll(s1, 1, 0) + pltpu.roll(s1, n - 1, 0)
    s4 = pltpu.roll(s2, 2, 0) + pltpu.roll(s2, n - 2, 0)
    s8 = pltpu.roll(s4, 4, 0) + pltpu.roll(s4, n - 4, 0)
    lane = lax.broadcasted_iota(jnp.int32, (tm, MIX_GROUP_WIDTH), 1)
    grp = lane // GROUP_DIM
    lo, hi = POOL_HALO, POOL_HALO + tm
    win = jnp.where(grp == 0, s1[lo:hi],
                    jnp.where(grp == 1, s2[lo:hi], jnp.where(grp == 2, s4[lo:hi], s8[lo:hi])))
    half = jnp.left_shift(1, grp)
    t = i * tm + lax.broadcasted_iota(jnp.int32, (tm, MIX_GROUP_WIDTH), 0)
    cnt = (jnp.minimum(t + half, seq_len) - jnp.maximum(t - half, 0)).astype(F32)
    dlt = win / cnt - u
    pool_y = jnp.dot(dlt.astype(BF16), wp_ref[...], preferred_element_type=F32) * ps_ref[...]
    fno_y = jnp.dot(f_ref[0].astype(BF16), wf_ref[...], preferred_element_type=F32)
    w = MIX_GROUP_WIDTH
    y = (jnp.dot(pool_y.astype(BF16), wo_ref[0:w, :], preferred_element_type=F32)
         + jnp.dot(fno_y.astype(BF16), wo_ref[w:2 * w, :], preferred_element_type=F32)
         + jnp.dot(at_ref[0], wo_ref[2 * w:, :], preferred_element_type=F32))
    o_ref[0] = x_ref[0] + gt_ref[0] * y


def _mixer_output(x, gt, u_pool, f, attn, wp_bd, pool_scale, wf_bd, w_out, tm):
    b, l, d = x.shape
    tm = min(tm, l)
    nb = tm // POOL_HALO
    last = l // POOL_HALO - 1
    full = lambda a: pl.BlockSpec(a.shape, lambda bi, i: (0,) * a.ndim)
    w = MIX_GROUP_WIDTH
    return pl.pallas_call(
        functools.partial(_mixout_kernel, seq_len=l),
        grid=(b, l // tm),
        in_specs=[pl.BlockSpec((1, tm, d), lambda bi, i: (bi, i, 0)),
                  pl.BlockSpec((1, 1, d), lambda bi, i: (bi, 0, 0)),
                  pl.BlockSpec((1, tm, w), lambda bi, i: (bi, i, 0)),
                  pl.BlockSpec((1, POOL_HALO, w), lambda bi, i: (bi, jnp.maximum(i * nb - 1, 0), 0)),
                  pl.BlockSpec((1, POOL_HALO, w), lambda bi, i: (bi, jnp.minimum((i + 1) * nb, last), 0)),
                  pl.BlockSpec((1, tm, w), lambda bi, i: (bi, i, 0)),
                  pl.BlockSpec((1, tm, attn.shape[-1]), lambda bi, i: (bi, i, 0)),
                  full(wp_bd), full(pool_scale), full(wf_bd), full(w_out)],
        out_specs=pl.BlockSpec((1, tm, d), lambda bi, i: (bi, i, 0)),
        out_shape=jax.ShapeDtypeStruct((b, l, d), F32),
        compiler_params=_cparams(("parallel", "arbitrary")),
        name="mixer_output",
    )(x, gt, u_pool, u_pool, u_pool, f, attn, wp_bd, pool_scale, wf_bd, w_out)


def _first_index_of_max(vals, iota, n):
    mx = jnp.max(vals, axis=0, keepdims=True)
    ix = jnp.min(jnp.where(vals == mx, iota, n), axis=0, keepdims=True)
    return mx, ix


def _ffnpre_kernel(x_ref, sh_ref, sc_ref, gt_ref, g_ref, wr_ref, rb_ref, wsg_ref, wsu_ref, wsd_ref,
                   cnt0_ref, hin_ref, iin_ref, win_ref, rin_ref,
                   xo_ref, h_ref, idx_ref, wt_ref, rank_ref, cnt_ref, carry_ref):
    del hin_ref, iin_ref, win_ref, rin_ref
    first = jnp.logical_and(pl.program_id(0) == 0, pl.program_id(1) == 0)

    @pl.when(first)
    def _():
        carry_ref[...] = cnt0_ref[...]

    x = x_ref[0]
    tm = x.shape[0]
    h = _rms(x, g_ref[...]) * (1.0 + sc_ref[0]) + sh_ref[0]
    for s in range(h.shape[1] // LANES):
        h_ref[:, s, :] = h[:, s * LANES:(s + 1) * LANES]

    logits = lax.dot_general(wr_ref[...], h, (((1,), (1,)), ((), ())),
                             preferred_element_type=F32, precision=HIGHEST)
    aff = jax.nn.sigmoid(logits)
    sel = aff + rb_ref[...]
    e_iota = lax.broadcasted_iota(jnp.int32, (N_EXPERTS, tm), 0).astype(F32)
    neg = jnp.float32(-jnp.inf)
    gscores = []
    for g in range(N_EXPERT_GROUPS):
        blk = sel[g * EXPERTS_PER_GROUP:(g + 1) * EXPERTS_PER_GROUP]
        it = lax.broadcasted_iota(jnp.int32, blk.shape, 0).astype(F32)
        m1, i1 = _first_index_of_max(blk, it, float(EXPERTS_PER_GROUP))
        m2 = jnp.max(jnp.where(it == i1, neg, blk), axis=0, keepdims=True)
        gscores.append(m1 + m2)
    gs = jnp.concatenate(gscores, axis=0)
    g_iota = lax.broadcasted_iota(jnp.int32, gs.shape, 0).astype(F32)
    gself = jnp.zeros(gs.shape, F32)
    for _ in range(TOPK_GROUPS):
        _, ig = _first_index_of_max(gs, g_iota, float(N_EXPERT_GROUPS))
        hit = g_iota == ig
        gself = jnp.where(hit, 1.0, gself)
        gs = jnp.where(hit, neg, gs)
    emask = jnp.concatenate(
        [jnp.broadcast_to(gself[g:g + 1], (EXPERTS_PER_GROUP, tm)) for g in range(N_EXPERT_GROUPS)], axis=0)
    masked = jnp.where(emask > 0.5, sel, neg)
    hits, idxs, wts = [], [], []
    chosen = jnp.zeros((N_EXPERTS, tm), F32)
    for _ in range(TOP_K):
        _, ie = _first_index_of_max(masked, e_iota, float(N_EXPERTS))
        hit = e_iota == ie
        hits.append(hit)
        idxs.append(ie)
        wts.append(jnp.sum(jnp.where(hit, aff, 0.0), axis=0, keepdims=True))
        chosen = jnp.where(hit, 1.0, chosen)
        masked = jnp.where(hit, neg, masked)
    wsum = wts[0]
    for w in wts[1:]:
        wsum = wsum + w
    scale = ROUTED_SCALE / wsum
    idx_ref[...] = jnp.concatenate(idxs, axis=0).astype(jnp.int32)
    wt_ref[...] = jnp.concatenate([w * scale for w in wts], axis=0)

    r_io = lax.broadcasted_iota(jnp.int32, (tm, tm), 0)
    c_io = lax.broadcasted_iota(jnp.int32, (tm, tm), 1)
    tri = (r_io <= c_io).astype(BF16)
    incl = jnp.dot(chosen.astype(BF16), tri, preferred_element_type=F32)
    base = carry_ref[...]
    rank = base + incl - chosen
    rank_ref[...] = jnp.concatenate(
        [jnp.sum(jnp.where(hit, rank, 0.0), axis=0, keepdims=True) for hit in hits], axis=0).astype(jnp.int32)
    total = base + jnp.sum(chosen, axis=1, keepdims=True)
    carry_ref[...] = total
    cnt_ref[...] = total

    hb = h.astype(BF16)
    a = jnp.dot(hb, wsg_ref[...], preferred_element_type=F32)
    bb = jnp.dot(hb, wsu_ref[...], preferred_element_type=F32)
    sh_out = jnp.dot((_silu(a) * bb).astype(BF16), wsd_ref[...], preferred_element_type=F32)
    xo_ref[0] = x + gt_ref[0] * sh_out


def _ffn_front(x, sh, sc, gt, g, w_rt, rb, wsg, wsu, wsd, cnt0, bufs, tok_off, tm):
    b, l, d = x.shape
    tm = min(tm, l)
    nt = l // tm
    off = tok_off // tm
    hbuf, ibuf, wbuf, rbuf = bufs
    full = lambda a: pl.BlockSpec(a.shape, lambda bi, i: (0,) * a.ndim)
    vec = pl.BlockSpec((1, 1, d), lambda bi, i: (bi, 0, 0))
    anyspec = pl.BlockSpec(memory_space=pl.ANY)
    tokmap = lambda bi, i: (0, off + bi * nt + i)
    outs = pl.pallas_call(
        _ffnpre_kernel,
        grid=(b, nt),
        in_specs=[pl.BlockSpec((1, tm, d), lambda bi, i: (bi, i, 0)), vec, vec, vec,
                  full(g), full(w_rt), full(rb), full(wsg), full(wsu), full(wsd), full(cnt0),
                  anyspec, anyspec, anyspec, anyspec],
        out_specs=[pl.BlockSpec((1, tm, d), lambda bi, i: (bi, i, 0)),
                   pl.BlockSpec((tm, d // LANES, LANES), lambda bi, i: (off + bi * nt + i, 0, 0)),
                   pl.BlockSpec((TOP_K, tm), tokmap), pl.BlockSpec((TOP_K, tm), tokmap),
                   pl.BlockSpec((TOP_K, tm), tokmap),
                   pl.BlockSpec((N_EXPERTS, 1), lambda bi, i: (0, 0))],
        out_shape=[jax.ShapeDtypeStruct(x.shape, F32),
                   jax.ShapeDtypeStruct(hbuf.shape, hbuf.dtype),
                   jax.ShapeDtypeStruct(ibuf.shape, ibuf.dtype),
                   jax.ShapeDtypeStruct(wbuf.shape, wbuf.dtype),
                   jax.ShapeDtypeStruct(rbuf.shape, rbuf.dtype),
                   jax.ShapeDtypeStruct((N_EXPERTS, 1), F32)],
        scratch_shapes=[pltpu.VMEM((N_EXPERTS, 1), F32)],
        input_output_aliases={11: 1, 12: 2, 13: 3, 14: 4},
        compiler_params=_cparams(("arbitrary", "arbitrary")),
        name="ffn_front",
    )(x, sh, sc, gt, g, w_rt, rb, wsg, wsu, wsd, cnt0, hbuf, ibuf, wbuf, rbuf)
    return outs[0], tuple(outs[1:5]), outs[5]


def _gather_pipeline(i, n_steps, idx_hbm, table_hbm, idx_smem, rows, sem_idx, sem_rows, n_rows):
    def idx_copy(step, slot):
        return pltpu.make_async_copy(idx_hbm.at[step], idx_smem.at[slot], sem_idx.at[slot])

    def start_rows(slot):
        def body(r, _):
            tok = idx_smem[slot, 0, r]
            pltpu.make_async_copy(table_hbm.at[tok], rows.at[slot, r], sem_rows.at[slot]).start()
            return 0
        lax.fori_loop(0, n_rows, body, 0, unroll=8)

    def wait_rows(slot):
        pltpu.make_async_copy(table_hbm.at[pl.ds(0, n_rows)], rows.at[slot], sem_rows.at[slot]).wait()

    @pl.when(i == 0)
    def _():
        idx_copy(0, 0).start()
        idx_copy(0, 0).wait()
        start_rows(0)

        @pl.when(n_steps > 1)
        def _():
            idx_copy(1, 1).start()

    nxt = (i + 1) % 2

    @pl.when(i + 1 < n_steps)
    def _():
        idx_copy(i + 1, nxt).wait()
        start_rows(nxt)

    @pl.when(i + 2 < n_steps)
    def _():
        idx_copy(i + 2, i % 2).start()

    wait_rows(i % 2)


def _expert_kernel(bexp_ref, nused_ref, src_ref, h_ref, wg_ref, wu_ref, wd_ref, y_ref,
                   idx_smem, rows, sem_idx, sem_rows):
    del bexp_ref
    i = pl.program_id(0)
    n_steps = nused_ref[0]

    @pl.when(i < n_steps)
    def _():
        _gather_pipeline(i, n_steps, src_ref, h_ref, idx_smem, rows, sem_idx, sem_rows, EXPERT_BLOCK)
        slot = i % 2
        n_chunks = rows.shape[2]
        xb = jnp.concatenate([rows[slot, :, s, :].astype(BF16) for s in range(n_chunks)], axis=1)
        a = jnp.dot(xb, wg_ref[0], preferred_element_type=F32)
        bb = jnp.dot(xb, wu_ref[0], preferred_element_type=F32)
        y = jnp.dot((_silu(a) * bb).astype(BF16), wd_ref[0], preferred_element_type=F32)
        for s in range(n_chunks):
            y_ref[:, s, :] = y[:, s * LANES:(s + 1) * LANES]

    @pl.when(i >= n_steps)
    def _():
        y_ref[...] = jnp.zeros(y_ref.shape, y_ref.dtype)


def _experts(block_expert, n_used, src, h_rows, w_gate, w_up, w_down):
    n_blocks = src.shape[0]
    _, chunks, _ = h_rows.shape
    e, d, f = w_gate.shape
    grid_spec = pltpu.PrefetchScalarGridSpec(
        num_scalar_prefetch=2,
        grid=(n_blocks,),
        in_specs=[pl.BlockSpec(memory_space=pl.ANY), pl.BlockSpec(memory_space=pl.ANY),
                  pl.BlockSpec((1, d, f), lambda i, be, nu: (be[i], 0, 0)),
                  pl.BlockSpec((1, d, f), lambda i, be, nu: (be[i], 0, 0)),
                  pl.BlockSpec((1, f, d), lambda i, be, nu: (be[i], 0, 0))],
        out_specs=pl.BlockSpec((EXPERT_BLOCK, chunks, LANES), lambda i, be, nu: (i, 0, 0)),
        scratch_shapes=[pltpu.SMEM((2, 1, EXPERT_BLOCK), jnp.int32),
                        pltpu.VMEM((2, EXPERT_BLOCK, chunks, LANES), F32),
                        pltpu.SemaphoreType.DMA((2,)), pltpu.SemaphoreType.DMA((2,))],
    )
    return pl.pallas_call(
        _expert_kernel,
        grid_spec=grid_spec,
        out_shape=jax.ShapeDtypeStruct((n_blocks * EXPERT_BLOCK, chunks, LANES), F32),
        compiler_params=_cparams(("arbitrary",)),
        name="experts",
    )(block_expert, n_used, src, h_rows, w_gate, w_up, w_down)


def _combine_kernel(dest_ref, y_ref, x_ref, gt_ref, w_ref, gf_ref, o_ref,
                    idx_smem, rows, sem_idx, sem_rows, *, tile_off, final):
    nt = pl.num_programs(1)
    i = pl.program_id(0) * nt + pl.program_id(1)
    n_steps = pl.num_programs(0) * nt
    tm = x_ref.shape[1]
    n_rows = TOP_K * tm
    dest_view = dest_ref.at[pl.ds(tile_off, n_steps)]
    _gather_pipeline(i, n_steps, dest_view, y_ref, idx_smem, rows, sem_idx, sem_rows, n_rows)
    slot = i % 2
    w = w_ref[...]
    x = x_ref[0]
    gt = gt_ref[0]
    pieces = []
    for s in range(rows.shape[2]):
        acc = jnp.zeros((tm, LANES), F32)
        for k in range(TOP_K):
            acc = acc + w[:, k:k + 1] * rows[slot, k * tm:(k + 1) * tm, s, :]
        pieces.append(acc)
    out = x + gt * jnp.concatenate(pieces, axis=1)
    if final:
        out = _rms(out, gf_ref[...])
    o_ref[0] = out


def _combine(dest_tiles, y_rows, x, gt, w_tok, g_final, tok_off, tm, final):
    b, l, d = x.shape
    tm = min(tm, l)
    nt = l // tm
    chunks = d // LANES
    off = tok_off // tm
    return pl.pallas_call(
        functools.partial(_combine_kernel, tile_off=off, final=final),
        grid=(b, nt),
        in_specs=[pl.BlockSpec(memory_space=pl.ANY), pl.BlockSpec(memory_space=pl.ANY),
                  pl.BlockSpec((1, tm, d), lambda bi, i: (bi, i, 0)),
                  pl.BlockSpec((1, 1, d), lambda bi, i: (bi, 0, 0)),
                  pl.BlockSpec((tm, TOP_K), lambda bi, i: (off + bi * nt + i, 0)),
                  pl.BlockSpec(g_final.shape, lambda bi, i: (0, 0))],
        out_specs=pl.BlockSpec((1, tm, d), lambda bi, i: (bi, i, 0)),
        out_shape=jax.ShapeDtypeStruct(x.shape, F32),
        scratch_shapes=[pltpu.SMEM((2, 1, TOP_K * tm), jnp.int32),
                        pltpu.VMEM((2, TOP_K * tm, chunks, LANES), F32),
                        pltpu.SemaphoreType.DMA((2,)), pltpu.SemaphoreType.DMA((2,))],
        compiler_params=_cparams(("arbitrary", "arbitrary")),
        name="combine",
    )(dest_tiles, y_rows, x, gt, w_tok, g_final)


def _rope_tables(l):
    rows = l // GRID_W
    row = jnp.repeat(jnp.arange(rows, dtype=F32), GRID_W)
    col = jnp.tile(jnp.arange(GRID_W, dtype=F32), rows)
    inv_freq = ROPE_THETA ** (-jnp.arange(ROPE_FREQS, dtype=F32) * 2.0 / (2 * ROPE_FREQS))
    ar, ac = row[:, None] * inv_freq, col[:, None] * inv_freq
    ones = jnp.ones((l, ROPE_LANE0), F32)
    zpad = jnp.zeros((l, HEAD_PAD - ROPE_LANE0 - QK_ROPE), F32)
    cos_t = jnp.concatenate([ones, jnp.cos(ar), jnp.cos(ar), jnp.cos(ac), jnp.cos(ac), zpad], axis=1)
    sin_t = jnp.concatenate([0.0 * ones, -jnp.sin(ar), jnp.sin(ar), -jnp.sin(ac), jnp.sin(ac), zpad], axis=1)
    return cos_t, sin_t


def _identity_tables(l):
    lane = jnp.arange(HEAD_PAD)
    cos_t = jnp.broadcast_to((lane < ROPE_LANE0 + QK_ROPE).astype(F32), (l, HEAD_PAD))
    return cos_t, jnp.zeros((l, HEAD_PAD), F32)


def _block_diag(w):
    g, a, b = w.shape
    out = jnp.zeros((g * a, g * b), w.dtype)
    for i in range(g):
        out = out.at[i * a:(i + 1) * a, i * b:(i + 1) * b].set(w[i])
    return out


def _prep_layer(w_in, w_pool, w_fno, w_uq, w_ukv):
    d = w_in.shape[0]
    kr = jnp.zeros((d, HEAD_PAD), w_in.dtype).at[:, ROPE_LANE0:ROPE_LANE0 + QK_ROPE].set(w_in[:, COL_KR:])
    w_in_p = jnp.concatenate([w_in[:, :COL_KR], kr], axis=1).astype(BF16)
    qk = QK_NOPE + QK_ROPE
    wq = w_uq.reshape(Q_LORA, N_HEADS, qk)
    wq = jnp.pad(wq, ((0, 0), (0, 0), (0, HEAD_PAD - qk))).reshape(Q_LORA, N_HEADS * HEAD_PAD).astype(BF16)
    wkv = w_ukv.reshape(KV_LORA, N_HEADS, QK_NOPE + V_DIM)
    wk = jnp.pad(wkv[..., :QK_NOPE], ((0, 0), (0, 0), (0, HEAD_PAD - QK_NOPE)))
    wv = jnp.pad(wkv[..., QK_NOPE:], ((0, 0), (0, 0), (0, HEAD_PAD - V_DIM)))
    wkv_p = jnp.concatenate([wk.reshape(KV_LORA, -1), wv.reshape(KV_LORA, -1)], axis=1).astype(BF16)
    return w_in_p, wq, wkv_p, _block_diag(w_pool).astype(BF16), _block_diag(w_fno).astype(BF16)


def _routing_tables(idx_t, rank_t, counts, n_tok, tm):
    counts = counts.reshape(-1).astype(jnp.int32)
    padded = (counts + EXPERT_BLOCK - 1) // EXPERT_BLOCK * EXPERT_BLOCK
    pad_end = jnp.cumsum(padded)
    pad_start = pad_end - padded
    dest = pad_start[idx_t] + rank_t
    n_assign = n_tok * TOP_K
    n_rows = (n_assign + N_EXPERTS * (EXPERT_BLOCK - 1) + EXPERT_BLOCK - 1) // EXPERT_BLOCK * EXPERT_BLOCK
    n_blocks = n_rows // EXPERT_BLOCK
    tok = jnp.broadcast_to(jnp.arange(n_tok, dtype=jnp.int32)[None, :], dest.shape)
    src = jnp.zeros((n_rows,), jnp.int32).at[dest.reshape(-1)].set(tok.reshape(-1))
    block_start = jnp.arange(n_blocks, dtype=jnp.int32) * EXPERT_BLOCK
    block_expert = jnp.minimum(jnp.searchsorted(pad_end, block_start, side='right'),
                               N_EXPERTS - 1).astype(jnp.int32)
    n_used = (pad_end[-1:] // EXPERT_BLOCK).astype(jnp.int32)
    dest_tiles = dest.reshape(TOP_K, n_tok // tm, tm).transpose(1, 0, 2).reshape(n_tok // tm, 1, TOP_K * tm)
    return src.reshape(n_blocks, 1, EXPERT_BLOCK), block_expert, n_used, dest_tiles


TM_PROJ = 256
TM_MIX = 256
TM_FFN = 256
TM_COMBINE = 128
ATTN_TQ = 256
ATTN_TK = 512


def kernel(x, c, ctx, c_ctx, w_mod, b_mod, g_mix, g_ffn, w_in, w_pool, pool_scale, w_fno, g_q, w_uq,
           g_kv, w_ukv, w_out, w_router, router_bias, w_gate, w_up, w_down, w_sh_gate, w_sh_up,
           w_sh_down, g_final):
    b, l, d = x.shape
    lc = ctx.shape[1]
    depth = w_mod.shape[0]
    cc = jnp.zeros((SUBLANES, d), F32).at[:b].set(c).at[b].set(c_ctx)
    mods = _modulation(cc, w_mod, b_mod)
    cos_t, sin_t = _rope_tables(l)
    cos_i, sin_i = _identity_tables(lc)
    xc = ctx
    row = lambda v: v.reshape(1, -1)
    for li in range(depth):
        last = li == depth - 1
        m = mods[li].reshape(SUBLANES, N_MOD, d)
        lat = [m[:b, j][:, None, :] for j in range(N_MOD)]
        cm = [jnp.broadcast_to(m[b, j][None, None, :], (b, 1, d)) for j in range(N_MOD)]
        w_in_p, wq_p, wkv_p, wp_bd, wf_bd = _prep_layer(w_in[li], w_pool[li], w_fno[li], w_uq[li], w_ukv[li])
        w_out_b = w_out[li].astype(BF16)
        gq, gkv, gm = row(g_q[li]), row(g_kv[li]), row(g_mix[li])

        up, uf, q, k, v = _in_projection(x, lat[0], lat[1], gm, w_in_p, gq, wq_p, gkv, wkv_p,
                                         cos_t, sin_t, TM_PROJ)
        upc, ufc, qc, kc, vc = _in_projection(xc, cm[0], cm[1], gm, w_in_p, gq, wq_p, gkv, wkv_p,
                                              cos_i, sin_i, TM_PROJ)
        attn = _attention(q, kc, vc, k, v, tq=ATTN_TQ, tk=ATTN_TK)
        f = _fourier(uf)
        ps = row(pool_scale[li])
        x = _mixer_output(x, lat[2], up, f, attn, wp_bd, ps, wf_bd, w_out_b, TM_MIX)
        if not last:
            attn_c = _attention(qc, kc, vc, tq=ATTN_TQ, tk=ATTN_TK)
            fc = _fourier(ufc)
            xc = _mixer_output(xc, cm[2], upc, fc, attn_c, wp_bd, ps, wf_bd, w_out_b, TM_MIX)

        n_tok = b * l + (0 if last else b * lc)
        chunks = d // LANES
        bufs = (jnp.zeros((n_tok, chunks, LANES), F32), jnp.zeros((TOP_K, n_tok), jnp.int32),
                jnp.zeros((TOP_K, n_tok), F32), jnp.zeros((TOP_K, n_tok), jnp.int32))
        w_rt = w_router[li].T
        rb = router_bias[li].reshape(N_EXPERTS, 1)
        wsg, wsu, wsd = w_sh_gate[li].astype(BF16), w_sh_up[li].astype(BF16), w_sh_down[li].astype(BF16)
        gf = row(g_ffn[li])
        cnt0 = jnp.zeros((N_EXPERTS, 1), F32)
        x, bufs, cnt = _ffn_front(x, lat[3], lat[4], lat[5], gf, w_rt, rb, wsg, wsu, wsd, cnt0, bufs, 0, TM_FFN)
        if not last:
            xc, bufs, cnt = _ffn_front(xc, cm[3], cm[4], cm[5], gf, w_rt, rb, wsg, wsu, wsd, cnt, bufs,
                                       b * l, TM_FFN)
        h_rows, idx_t, w_t, rank_t = bufs
        src, block_expert, n_used, dest_tiles = _routing_tables(idx_t, rank_t, cnt, n_tok, TM_COMBINE)
        y_rows = _experts(block_expert, n_used, src, h_rows, w_gate[li].astype(BF16),
                          w_up[li].astype(BF16), w_down[li].astype(BF16))
        w_tok = w_t.T
        x = _combine(dest_tiles, y_rows, x, lat[5], w_tok, row(g_final), 0, TM_COMBINE, last)
        if not last:
            xc = _combine(dest_tiles, y_rows, xc, cm[5], w_tok, row(g_final), b * l, TM_COMBINE, False)
    return x
```

```python
import functools
import math

import jax
import jax.numpy as jnp
import numpy as np
from jax import lax
from jax.experimental import pallas as pl
from jax.experimental.pallas import tpu as pltpu

F32 = jnp.float32
BF16 = jnp.bfloat16
HIGHEST = lax.Precision.HIGHEST

EPS = 1e-6
N_MOD = 6
GRID_W = 64
POOL_WINDOWS = (2, 4, 8, 16)
GROUP_DIM = 64
N_GROUPS = 4
MIX_GROUP_WIDTH = N_GROUPS * GROUP_DIM
N_HEADS = 8
QK_NOPE = 64
QK_ROPE = 32
V_DIM = 64
Q_LORA = 384
KV_LORA = 256
ROPE_FREQS = QK_ROPE // 4
ROPE_THETA = 10000.0
SOFTMAX_SCALE = (QK_NOPE + QK_ROPE) ** -0.5
N_EXPERTS = 64
TOP_K = 8
N_EXPERT_GROUPS = 8
TOPK_GROUPS = 4
EXPERTS_PER_GROUP = N_EXPERTS // N_EXPERT_GROUPS
ROUTED_SCALE = 2.5

LANES = 128
SUBLANES = 8
HEAD_PAD = LANES
VMEM_LIMIT = 56 * 1024 * 1024

COL_FNO = MIX_GROUP_WIDTH
COL_Q = COL_FNO + MIX_GROUP_WIDTH
COL_KV = COL_Q + Q_LORA
COL_KR = COL_KV + KV_LORA
IN_PAD = COL_KR + HEAD_PAD
ROPE_LANE0 = QK_NOPE

EXPERT_BLOCK = 128


def _cparams(sem, vmem=VMEM_LIMIT):
    return pltpu.CompilerParams(dimension_semantics=sem, vmem_limit_bytes=vmem)


def _rms(x, g):
    return x * lax.rsqrt(jnp.mean(x * x, axis=-1, keepdims=True) + EPS) * g


def _silu(x):
    return x * jax.nn.sigmoid(x)


def _mod_kernel(c_ref, w_ref, b_ref, o_ref):
    s = _silu(c_ref[...])
    o_ref[0] = jnp.dot(s, w_ref[0], preferred_element_type=F32, precision=HIGHEST) + b_ref[0]


def _modulation(cc, w_mod, b_mod):
    depth, d, nd = w_mod.shape
    n_chunks = nd // d
    return pl.pallas_call(
        _mod_kernel,
        grid=(depth, n_chunks),
        in_specs=[pl.BlockSpec((SUBLANES, d), lambda l, j: (0, 0)),
                  pl.BlockSpec((1, d, d), lambda l, j: (l, 0, j)),
                  pl.BlockSpec((1, 1, d), lambda l, j: (l, 0, j))],
        out_specs=pl.BlockSpec((1, SUBLANES, d), lambda l, j: (l, 0, j)),
        out_shape=jax.ShapeDtypeStruct((depth, SUBLANES, nd), F32),
        compiler_params=_cparams(("parallel", "parallel")),
        name="modulation",
    )(cc, w_mod, b_mod.reshape(depth, 1, nd))


def _rope(t, c, s):
    lane = lax.broadcasted_iota(jnp.int32, t.shape, 1)
    first_half = (lane % (2 * ROPE_FREQS)) < ROPE_FREQS
    partner = jnp.where(first_half, pltpu.roll(t, LANES - ROPE_FREQS, 1), pltpu.roll(t, ROPE_FREQS, 1))
    return t * c + partner * s


def _inproj_kernel(x_ref, sh_ref, sc_ref, g_ref, win_ref, gq_ref, wuq_ref, gkv_ref, wukv_ref,
                   cos_ref, sin_ref, up_ref, uf_ref, q_ref, k_ref, v_ref):
    x = x_ref[0]
    h = _rms(x, g_ref[...]) * (1.0 + sc_ref[0]) + sh_ref[0]
    u = jnp.dot(h.astype(BF16), win_ref[...], preferred_element_type=F32)
    up_ref[0] = u[:, :COL_FNO]
    uf_ref[0] = u[:, COL_FNO:COL_Q]
    cq = _rms(u[:, COL_Q:COL_KV], gq_ref[...])
    q = jnp.dot(cq.astype(BF16), wuq_ref[...], preferred_element_type=F32)
    ckv = _rms(u[:, COL_KV:COL_KR], gkv_ref[...])
    kv = jnp.dot(ckv.astype(BF16), wukv_ref[...], preferred_element_type=F32)
    cos = cos_ref[...]
    sin = sin_ref[...]
    kr = _rope(u[:, COL_KR:IN_PAD], cos, sin)
    lane = lax.broadcasted_iota(jnp.int32, kr.shape, 1)
    ones_col = (lane == V_DIM).astype(F32)
    q_scale = SOFTMAX_SCALE * math.log2(math.e)
    for hd in range(N_HEADS):
        lo = hd * HEAD_PAD
        q_ref[0, hd] = (_rope(q[:, lo:lo + HEAD_PAD], cos, sin) * q_scale).astype(BF16)
        k_ref[0, hd] = (kv[:, lo:lo + HEAD_PAD] + kr).astype(BF16)
        vo = N_HEADS * HEAD_PAD + lo
        v_ref[0, hd] = (kv[:, vo:vo + HEAD_PAD] + ones_col).astype(BF16)


def _in_projection(x, sh, sc, g, w_in_p, g_q, w_uq_p, g_kv, w_ukv_p, cos_t, sin_t, tm):
    b, l, d = x.shape
    tm = min(tm, l)
    full = lambda a: pl.BlockSpec(a.shape, lambda bi, i: (0,) * a.ndim)
    vec = pl.BlockSpec((1, 1, d), lambda bi, i: (bi, 0, 0))
    head_out = pl.BlockSpec((1, N_HEADS, tm, HEAD_PAD), lambda bi, i: (bi, 0, i, 0))
    head_shape = jax.ShapeDtypeStruct((b, N_HEADS, l, HEAD_PAD), BF16)
    return pl.pallas_call(
        _inproj_kernel,
        grid=(b, l // tm),
        in_specs=[pl.BlockSpec((1, tm, d), lambda bi, i: (bi, i, 0)), vec, vec,
                  full(g), full(w_in_p), full(g_q), full(w_uq_p), full(g_kv), full(w_ukv_p),
                  pl.BlockSpec((tm, HEAD_PAD), lambda bi, i: (i, 0)),
                  pl.BlockSpec((tm, HEAD_PAD), lambda bi, i: (i, 0))],
        out_specs=[pl.BlockSpec((1, tm, MIX_GROUP_WIDTH), lambda bi, i: (bi, i, 0)),
                   pl.BlockSpec((1, tm, MIX_GROUP_WIDTH), lambda bi, i: (bi, i, 0)),
                   head_out, head_out, head_out],
        out_shape=[jax.ShapeDtypeStruct((b, l, MIX_GROUP_WIDTH), F32),
                   jax.ShapeDtypeStruct((b, l, MIX_GROUP_WIDTH), F32),
                   head_shape, head_shape, head_shape],
        compiler_params=_cparams(("parallel", "parallel")),
        name="in_projection",
    )(x, sh, sc, g, w_in_p, g_q, w_uq_p, g_kv, w_ukv_p, cos_t, sin_t)


ATTN_HEADS_PER_STEP = 2


def _attn_kernel(*refs, tk, n_chunks):
    if n_chunks:
        q_ref, kc_ref, vc_ref, k_ref, v_ref, o_ref = refs
    else:
        q_ref, kc_ref, vc_ref, o_ref = refs
    tq = q_ref.shape[2]
    nt = (((1,), (1,)), ((), ()))
    heads = range(ATTN_HEADS_PER_STEP)
    qs = [q_ref[0, hh] for hh in heads]

    def step(q, kc, vc, m, acc):
        s = lax.dot_general(q, kc, nt, preferred_element_type=F32)
        m_new = jnp.maximum(m, jnp.max(s, axis=-1, keepdims=True))
        alpha = jnp.exp2(m - m_new)
        p = jnp.exp2(s - m_new).astype(BF16)
        return m_new, alpha * acc + jnp.dot(p, vc, preferred_element_type=F32)

    carry = tuple(step(qs[hh], kc_ref[0, hh], vc_ref[0, hh],
                       jnp.full((tq, 1), -1e30, F32), jnp.zeros((tq, HEAD_PAD), F32)) for hh in heads)
    if n_chunks:
        def body(j, carry):
            off = pl.multiple_of(j * tk, tk)
            return tuple(step(qs[hh], k_ref[0, hh, pl.ds(off, tk), :], v_ref[0, hh, pl.ds(off, tk), :],
                              *carry[hh]) for hh in heads)

        carry = lax.fori_loop(0, n_chunks, body, carry)
    for hh in heads:
        acc = carry[hh][1]
        out = acc[:, :V_DIM] / acc[:, V_DIM:V_DIM + 1]
        o_ref[0, :, hh * V_DIM:(hh + 1) * V_DIM] = out.astype(o_ref.dtype)


def _attention(q, k_ctx, v_ctx, k=None, v=None, *, tq, tk):
    b, h, l, _ = q.shape
    lc = k_ctx.shape[2]
    tq = min(tq, l)
    hps = ATTN_HEADS_PER_STEP
    qspec = pl.BlockSpec((1, hps, tq, HEAD_PAD), lambda bi, hi, i: (bi, hi, i, 0))
    cspec = pl.BlockSpec((1, hps, lc, HEAD_PAD), lambda bi, hi, i: (bi, hi, 0, 0))
    args, specs, n_chunks = [q, k_ctx, v_ctx], [qspec, cspec, cspec], 0
    if k is not None:
        lk = k.shape[2]
        tk = min(tk, lk)
        n_chunks = lk // tk
        kspec = pl.BlockSpec((1, hps, lk, HEAD_PAD), lambda bi, hi, i: (bi, hi, 0, 0))
        args += [k, v]
        specs += [kspec, kspec]
    return pl.pallas_call(
        functools.partial(_attn_kernel, tk=tk, n_chunks=n_chunks),
        grid=(b, h // hps, l // tq),
        in_specs=specs,
        out_specs=pl.BlockSpec((1, tq, hps * V_DIM), lambda bi, hi, i: (bi, i, hi)),
        out_shape=jax.ShapeDtypeStruct((b, l, h * V_DIM), BF16),
        compiler_params=_cparams(("parallel", "parallel", "arbitrary")),
        name="attention",
    )(*args)


def _dft_step1_kernel(x_ref, c_ref, s_ref, yr_ref, yi_ref):
    xb = x_ref[0].astype(BF16)
    yr_ref[0] = jnp.dot(c_ref[...], xb, preferred_element_type=F32)
    yi_ref[0] = jnp.dot(s_ref[...], xb, preferred_element_type=F32)


def _dft_step2_kernel(yr_ref, yi_ref, tc_ref, ts_ref, c_ref, s_ref, cc_ref, sc_ref, o_ref):
    for j in range(SUBLANES):
        yr, yi = yr_ref[0, j], yi_ref[0, j]
        tc, ts = tc_ref[j], ts_ref[j]
        zr = (yr * tc - yi * ts).astype(BF16)
        zi = (yi * tc + yr * ts).astype(BF16)
        a = (jnp.dot(c_ref[...], zr, preferred_element_type=F32)
             - jnp.dot(s_ref[...], zi, preferred_element_type=F32))
        bm = (jnp.dot(s_ref[...], zr, preferred_element_type=F32)
              + jnp.dot(c_ref[...], zi, preferred_element_type=F32))
        o_ref[0, :, j, :] = (jnp.dot(a.astype(BF16), cc_ref[...], preferred_element_type=F32)
                             - jnp.dot(bm.astype(BF16), sc_ref[...], preferred_element_type=F32))


def _dft_dense_kernel(x_ref, c_ref, s_ref, cc_ref, sc_ref, o_ref):
    xb = x_ref[0].astype(BF16)
    a = jnp.dot(c_ref[...], xb, preferred_element_type=F32)
    bm = jnp.dot(s_ref[...], xb, preferred_element_type=F32)
    o_ref[0] = (jnp.dot(a.astype(BF16), cc_ref[...], preferred_element_type=F32)
                - jnp.dot(bm.astype(BF16), sc_ref[...], preferred_element_type=F32))


def _dft_mats(n):
    ang = 2.0 * np.pi * np.outer(np.arange(n), np.arange(n)) / n
    return np.cos(ang), np.sin(ang)


def _channel_dft(l):
    c, s = _dft_mats(GROUP_DIM)
    eye = np.eye(N_GROUPS)
    norm = 1.0 / math.sqrt(l * GROUP_DIM)
    return (jnp.asarray(np.kron(eye, c) * norm, BF16), jnp.asarray(np.kron(eye, s) * norm, BF16))


def _dft_factors(l):
    n1 = 1 << (int(math.log2(l)) // 2)
    return n1, l // n1


def _fourier(u_fno, lane_block=4096):
    b, l, c = u_fno.shape
    cc, sc = _channel_dft(l)
    full = lambda a: pl.BlockSpec(a.shape, lambda *_: (0,) * a.ndim)
    if l <= 512:
        cm, sm = _dft_mats(l)
        cm, sm = jnp.asarray(cm, BF16), jnp.asarray(sm, BF16)
        blk = pl.BlockSpec((1, l, c), lambda bi: (bi, 0, 0))
        return pl.pallas_call(
            _dft_dense_kernel, grid=(b,),
            in_specs=[blk, full(cm), full(sm), full(cc), full(sc)],
            out_specs=blk, out_shape=jax.ShapeDtypeStruct((b, l, c), F32),
            compiler_params=_cparams(("parallel",)), name="dft_dense",
        )(u_fno, cm, sm, cc, sc)
    n1, n2 = _dft_factors(l)
    c1, s1 = _dft_mats(n1)
    c2, s2 = _dft_mats(n2)
    c1, s1, c2, s2 = (jnp.asarray(m, BF16) for m in (c1, s1, c2, s2))
    ang = 2.0 * np.pi * np.outer(np.arange(n1), np.arange(n2)) / l
    tc = jnp.asarray(np.cos(ang)[:, :, None], F32)
    ts = jnp.asarray(np.sin(ang)[:, :, None], F32)
    w = n2 * c
    tn = min(lane_block, w)
    x2 = u_fno.reshape(b, n1, w)
    yspec = pl.BlockSpec((1, n1, tn), lambda bi, i: (bi, 0, i))
    yr, yi = pl.pallas_call(
        _dft_step1_kernel, grid=(b, w // tn),
        in_specs=[yspec, full(c1), full(s1)],
        out_specs=[yspec, yspec],
        out_shape=[jax.ShapeDtypeStruct((b, n1, w), F32)] * 2,
        compiler_params=_cparams(("parallel", "parallel")), name="dft_step1",
    )(x2, c1, s1)
    yr = yr.reshape(b, n1, n2, c)
    yi = yi.reshape(b, n1, n2, c)
    slab = pl.BlockSpec((1, SUBLANES, n2, c), lambda bi, i: (bi, i, 0, 0))
    tw = pl.BlockSpec((SUBLANES, n2, 1), lambda bi, i: (i, 0, 0))
    out = pl.pallas_call(
        _dft_step2_kernel, grid=(b, n1 // SUBLANES),
        in_specs=[slab, slab, tw, tw, full(c2), full(s2), full(cc), full(sc)],
        out_specs=pl.BlockSpec((1, n2, SUBLANES, c), lambda bi, i: (bi, 0, i, 0)),
        out_shape=jax.ShapeDtypeStruct((b, n2, n1, c), F32),
        compiler_params=_cparams(("parallel", "parallel")), name="dft_step2",
    )(yr, yi, tc, ts, c2, s2, cc, sc)
    return out.reshape(b, l, c)


POOL_HALO = SUBLANES


def _mixout_kernel(x_ref, gt_ref, up_ref, prev_ref, next_ref, f_ref, at_ref,
                   wp_ref, ps_ref, wf_ref, wo_ref, o_ref, *, seq_len):
    i = pl.program_id(1)
    tm = x_ref.shape[1]
    u = up_ref[0]
    prev = jnp.where(i > 0, prev_ref[0], 0.0)
    nxt = jnp.where(i < pl.num_programs(1) - 1, next_ref[0], 0.0)
    p = jnp.concatenate([prev, u, nxt], axis=0)
    n = tm + 2 * POOL_HALO
    s1 = pltpu.roll(p, 1, 0) + p
    s2 = pltpu.roll(s1, 1, 0) + pltpu.roll(s1, n - 1, 0)
    s4 = pltpu.roll(s2, 2, 0) + pltpu.roll(s2, n - 2, 0)
    s8 = pltpu.roll(s4, 4, 0) + pltpu.roll(s4, n - 4, 0)
    lane = lax.broadcasted_iota(jnp.int32, (tm, MIX_GROUP_WIDTH), 1)
    grp = lane // GROUP_DIM
    lo, hi = POOL_HALO, POOL_HALO + tm
    win = jnp.where(grp == 0, s1[lo:hi],
                    jnp.where(grp == 1, s2[lo:hi], jnp.where(grp == 2, s4[lo:hi], s8[lo:hi])))
    half = jnp.left_shift(1, grp)
    t = i * tm + lax.broadcasted_iota(jnp.int32, (tm, MIX_GROUP_WIDTH), 0)
    cnt = (jnp.minimum(t + half, seq_len) - jnp.maximum(t - half, 0)).astype(F32)
    dlt = win / cnt - u
    pool_y = jnp.dot(dlt.astype(BF16), wp_ref[...], preferred_element_type=F32) * ps_ref[...]
    fno_y = jnp.dot(f_ref[0].astype(BF16), wf_ref[...], preferred_element_type=F32)
    w = MIX_GROUP_WIDTH
    y = (jnp.dot(pool_y.astype(BF16), wo_ref[0:w, :], preferred_element_type=F32)
         + jnp.dot(fno_y.astype(BF16), wo_ref[w:2 * w, :], preferred_element_type=F32)
         + jnp.dot(at_ref[0], wo_ref[2 * w:, :], preferred_element_type=F32))
    o_ref[0] = x_ref[0] + gt_ref[0] * y


def _mixer_output(x, gt, u_pool, f, attn, wp_bd, pool_scale, wf_bd, w_out, tm):
    b, l, d = x.shape
    tm = min(tm, l)
    nb = tm // POOL_HALO
    last = l // POOL_HALO - 1
    full = lambda a: pl.BlockSpec(a.shape, lambda bi, i: (0,) * a.ndim)
    w = MIX_GROUP_WIDTH
    return pl.pallas_call(
        functools.partial(_mixout_kernel, seq_len=l),
        grid=(b, l // tm),
        in_specs=[pl.BlockSpec((1, tm, d), lambda bi, i: (bi, i, 0)),
                  pl.BlockSpec((1, 1, d), lambda bi, i: (bi, 0, 0)),
                  pl.BlockSpec((1, tm, w), lambda bi, i: (bi, i, 0)),
                  pl.BlockSpec((1, POOL_HALO, w), lambda bi, i: (bi, jnp.maximum(i * nb - 1, 0), 0)),
                  pl.BlockSpec((1, POOL_HALO, w), lambda bi, i: (bi, jnp.minimum((i + 1) * nb, last), 0)),
                  pl.BlockSpec((1, tm, w), lambda bi, i: (bi, i, 0)),
                  pl.BlockSpec((1, tm, attn.shape[-1]), lambda bi, i: (bi, i, 0)),
                  full(wp_bd), full(pool_scale), full(wf_bd), full(w_out)],
        out_specs=pl.BlockSpec((1, tm, d), lambda bi, i: (bi, i, 0)),
        out_shape=jax.ShapeDtypeStruct((b, l, d), F32),
        compiler_params=_cparams(("parallel", "arbitrary")),
        name="mixer_output",
    )(x, gt, u_pool, u_pool, u_pool, f, attn, wp_bd, pool_scale, wf_bd, w_out)


def _first_index_of_max(vals, iota, n):
    mx = jnp.max(vals, axis=0, keepdims=True)
    ix = jnp.min(jnp.where(vals == mx, iota, n), axis=0, keepdims=True)
    return mx, ix


def _ffnpre_kernel(x_ref, sh_ref, sc_ref, gt_ref, g_ref, wr_ref, rb_ref, wsg_ref, wsu_ref, wsd_ref,
                   cnt0_ref, hin_ref, iin_ref, win_ref, rin_ref,
                   xo_ref, h_ref, idx_ref, wt_ref, rank_ref, cnt_ref, carry_ref):
    del hin_ref, iin_ref, win_ref, rin_ref
    first = jnp.logical_and(pl.program_id(0) == 0, pl.program_id(1) == 0)

    @pl.when(first)
    def _():
        carry_ref[...] = cnt0_ref[...]

    x = x_ref[0]
    tm = x.shape[0]
    h = _rms(x, g_ref[...]) * (1.0 + sc_ref[0]) + sh_ref[0]
    for s in range(h.shape[1] // LANES):
        h_ref[:, s, :] = h[:, s * LANES:(s + 1) * LANES]

    logits = lax.dot_general(wr_ref[...], h, (((1,), (1,)), ((), ())),
                             preferred_element_type=F32, precision=HIGHEST)
    aff = jax.nn.sigmoid(logits)
    sel = aff + rb_ref[...]
    e_iota = lax.broadcasted_iota(jnp.int32, (N_EXPERTS, tm), 0).astype(F32)
    neg = jnp.float32(-jnp.inf)
    gscores = []
    for g in range(N_EXPERT_GROUPS):
        blk = sel[g * EXPERTS_PER_GROUP:(g + 1) * EXPERTS_PER_GROUP]
        it = lax.broadcasted_iota(jnp.int32, blk.shape, 0).astype(F32)
        m1, i1 = _first_index_of_max(blk, it, float(EXPERTS_PER_GROUP))
        m2 = jnp.max(jnp.where(it == i1, neg, blk), axis=0, keepdims=True)
        gscores.append(m1 + m2)
    gs = jnp.concatenate(gscores, axis=0)
    g_iota = lax.broadcasted_iota(jnp.int32, gs.shape, 0).astype(F32)
    gself = jnp.zeros(gs.shape, F32)
    for _ in range(TOPK_GROUPS):
        _, ig = _first_index_of_max(gs, g_iota, float(N_EXPERT_GROUPS))
        hit = g_iota == ig
        gself = jnp.where(hit, 1.0, gself)
        gs = jnp.where(hit, neg, gs)
    emask = jnp.concatenate(
        [jnp.broadcast_to(gself[g:g + 1], (EXPERTS_PER_GROUP, tm)) for g in range(N_EXPERT_GROUPS)], axis=0)
    masked = jnp.where(emask > 0.5, sel, neg)
    hits, idxs, wts = [], [], []
    chosen = jnp.zeros((N_EXPERTS, tm), F32)
    for _ in range(TOP_K):
        _, ie = _first_index_of_max(masked, e_iota, float(N_EXPERTS))
        hit = e_iota == ie
        hits.append(hit)
        idxs.append(ie)
        wts.append(jnp.sum(jnp.where(hit, aff, 0.0), axis=0, keepdims=True))
        chosen = jnp.where(hit, 1.0, chosen)
        masked = jnp.where(hit, neg, masked)
    wsum = wts[0]
    for w in wts[1:]:
        wsum = wsum + w
    scale = ROUTED_SCALE / wsum
    idx_ref[...] = jnp.concatenate(idxs, axis=0).astype(jnp.int32)
    wt_ref[...] = jnp.concatenate([w * scale for w in wts], axis=0)

    r_io = lax.broadcasted_iota(jnp.int32, (tm, tm), 0)
    c_io = lax.broadcasted_iota(jnp.int32, (tm, tm), 1)
    tri = (r_io <= c_io).astype(BF16)
    incl = jnp.dot(chosen.astype(BF16), tri, preferred_element_type=F32)
    base = carry_ref[...]
    rank = base + incl - chosen
    rank_ref[...] = jnp.concatenate(
        [jnp.sum(jnp.where(hit, rank, 0.0), axis=0, keepdims=True) for hit in hits], axis=0).astype(jnp.int32)
    total = base + jnp.sum(chosen, axis=1, keepdims=True)
    carry_ref[...] = total
    cnt_ref[...] = total

    hb = h.astype(BF16)
    a = jnp.dot(hb, wsg_ref[...], preferred_element_type=F32)
    bb = jnp.dot(hb, wsu_ref[...], preferred_element_type=F32)
    sh_out = jnp.dot((_silu(a) * bb).astype(BF16), wsd_ref[...], preferred_element_type=F32)
    xo_ref[0] = x + gt_ref[0] * sh_out


def _ffn_front(x, sh, sc, gt, g, w_rt, rb, wsg, wsu, wsd, cnt0, bufs, tok_off, tm):
    b, l, d = x.shape
    tm = min(tm, l)
    nt = l // tm
    off = tok_off // tm
    hbuf, ibuf, wbuf, rbuf = bufs
    full = lambda a: pl.BlockSpec(a.shape, lambda bi, i: (0,) * a.ndim)
    vec = pl.BlockSpec((1, 1, d), lambda bi, i: (bi, 0, 0))
    anyspec = pl.BlockSpec(memory_space=pl.ANY)
    tokmap = lambda bi, i: (0, off + bi * nt + i)
    outs = pl.pallas_call(
        _ffnpre_kernel,
        grid=(b, nt),
        in_specs=[pl.BlockSpec((1, tm, d), lambda bi, i: (bi, i, 0)), vec, vec, vec,
                  full(g), full(w_rt), full(rb), full(wsg), full(wsu), full(wsd), full(cnt0),
                  anyspec, anyspec, anyspec, anyspec],
        out_specs=[pl.BlockSpec((1, tm, d), lambda bi, i: (bi, i, 0)),
                   pl.BlockSpec((tm, d // LANES, LANES), lambda bi, i: (off + bi * nt + i, 0, 0)),
                   pl.BlockSpec((TOP_K, tm), tokmap), pl.BlockSpec((TOP_K, tm), tokmap),
                   pl.BlockSpec((TOP_K, tm), tokmap),
                   pl.BlockSpec((N_EXPERTS, 1), lambda bi, i: (0, 0))],
        out_shape=[jax.ShapeDtypeStruct(x.shape, F32),
                   jax.ShapeDtypeStruct(hbuf.shape, hbuf.dtype),
                   jax.ShapeDtypeStruct(ibuf.shape, ibuf.dtype),
                   jax.ShapeDtypeStruct(wbuf.shape, wbuf.dtype),
                   jax.ShapeDtypeStruct(rbuf.shape, rbuf.dtype),
                   jax.ShapeDtypeStruct((N_EXPERTS, 1), F32)],
        scratch_shapes=[pltpu.VMEM((N_EXPERTS, 1), F32)],
        input_output_aliases={11: 1, 12: 2, 13: 3, 14: 4},
        compiler_params=_cparams(("arbitrary", "arbitrary")),
        name="ffn_front",
    )(x, sh, sc, gt, g, w_rt, rb, wsg, wsu, wsd, cnt0, hbuf, ibuf, wbuf, rbuf)
    return outs[0], tuple(outs[1:5]), outs[5]


def _gather_pipeline(i, n_steps, idx_hbm, table_hbm, idx_smem, rows, sem_idx, sem_rows, n_rows):
    def idx_copy(step, slot):
        return pltpu.make_async_copy(idx_hbm.at[step], idx_smem.at[slot], sem_idx.at[slot])

    def start_rows(slot):
        def body(r, _):
            tok = idx_smem[slot, 0, r]
            pltpu.make_async_copy(table_hbm.at[tok], rows.at[slot, r], sem_rows.at[slot]).start()
            return 0
        lax.fori_loop(0, n_rows, body, 0, unroll=8)

    def wait_rows(slot):
        pltpu.make_async_copy(table_hbm.at[pl.ds(0, n_rows)], rows.at[slot], sem_rows.at[slot]).wait()

    @pl.when(i == 0)
    def _():
        idx_copy(0, 0).start()
        idx_copy(0, 0).wait()
        start_rows(0)

        @pl.when(n_steps > 1)
        def _():
            idx_copy(1, 1).start()

    nxt = (i + 1) % 2

    @pl.when(i + 1 < n_steps)
    def _():
        idx_copy(i + 1, nxt).wait()
        start_rows(nxt)

    @pl.when(i + 2 < n_steps)
    def _():
        idx_copy(i + 2, i % 2).start()

    wait_rows(i % 2)


def _expert_kernel(bexp_ref, nused_ref, src_ref, h_ref, wg_ref, wu_ref, wd_ref, y_ref,
                   idx_smem, rows, sem_idx, sem_rows):
    del bexp_ref
    i = pl.program_id(0)
    n_steps = nused_ref[0]

    @pl.when(i < n_steps)
    def _():
        _gather_pipeline(i, n_steps, src_ref, h_ref, idx_smem, rows, sem_idx, sem_rows, EXPERT_BLOCK)
        slot = i % 2
        n_chunks = rows.shape[2]
        xb = jnp.concatenate([rows[slot, :, s, :].astype(BF16) for s in range(n_chunks)], axis=1)
        a = jnp.dot(xb, wg_ref[0], preferred_element_type=F32)
        bb = jnp.dot(xb, wu_ref[0], preferred_element_type=F32)
        y = jnp.dot((_silu(a) * bb).astype(BF16), wd_ref[0], preferred_element_type=F32)
        for s in range(n_chunks):
            y_ref[:, s, :] = y[:, s * LANES:(s + 1) * LANES]

    @pl.when(i >= n_steps)
    def _():
        y_ref[...] = jnp.zeros(y_ref.shape, y_ref.dtype)


def _experts(block_expert, n_used, src, h_rows, w_gate, w_up, w_down):
    n_blocks = src.shape[0]
    _, chunks, _ = h_rows.shape
    e, d, f = w_gate.shape
    grid_spec = pltpu.PrefetchScalarGridSpec(
        num_scalar_prefetch=2,
        grid=(n_blocks,),
        in_specs=[pl.BlockSpec(memory_space=pl.ANY), pl.BlockSpec(memory_space=pl.ANY),
                  pl.BlockSpec((1, d, f), lambda i, be, nu: (be[i], 0, 0)),
                  pl.BlockSpec((1, d, f), lambda i, be, nu: (be[i], 0, 0)),
                  pl.BlockSpec((1, f, d), lambda i, be, nu: (be[i], 0, 0))],
        out_specs=pl.BlockSpec((EXPERT_BLOCK, chunks, LANES), lambda i, be, nu: (i, 0, 0)),
        scratch_shapes=[pltpu.SMEM((2, 1, EXPERT_BLOCK), jnp.int32),
                        pltpu.VMEM((2, EXPERT_BLOCK, chunks, LANES), F32),
                        pltpu.SemaphoreType.DMA((2,)), pltpu.SemaphoreType.DMA((2,))],
    )
    return pl.pallas_call(
        _expert_kernel,
        grid_spec=grid_spec,
        out_shape=jax.ShapeDtypeStruct((n_blocks * EXPERT_BLOCK, chunks, LANES), F32),
        compiler_params=_cparams(("arbitrary",)),
        name="experts",
    )(block_expert, n_used, src, h_rows, w_gate, w_up, w_down)


def _combine_kernel(dest_ref, y_ref, x_ref, gt_ref, w_ref, gf_ref, o_ref,
                    idx_smem, rows, sem_idx, sem_rows, *, tile_off, final):
    nt = pl.num_programs(1)
    i = pl.program_id(0) * nt + pl.program_id(1)
    n_steps = pl.num_programs(0) * nt
    tm = x_ref.shape[1]
    n_rows = TOP_K * tm
    dest_view = dest_ref.at[pl.ds(tile_off, n_steps)]
    _gather_pipeline(i, n_steps, dest_view, y_ref, idx_smem, rows, sem_idx, sem_rows, n_rows)
    slot = i % 2
    w = w_ref[...]
    x = x_ref[0]
    gt = gt_ref[0]
    pieces = []
    for s in range(rows.shape[2]):
        acc = jnp.zeros((tm, LANES), F32)
        for k in range(TOP_K):
            acc = acc + w[:, k:k + 1] * rows[slot, k * tm:(k + 1) * tm, s, :]
        pieces.append(acc)
    out = x + gt * jnp.concatenate(pieces, axis=1)
    if final:
        out = _rms(out, gf_ref[...])
    o_ref[0] = out


def _combine(dest_tiles, y_rows, x, gt, w_tok, g_final, tok_off, tm, final):
    b, l, d = x.shape
    tm = min(tm, l)
    nt = l // tm
    chunks = d // LANES
    off = tok_off // tm
    return pl.pallas_call(
        functools.partial(_combine_kernel, tile_off=off, final=final),
        grid=(b, nt),
        in_specs=[pl.BlockSpec(memory_space=pl.ANY), pl.BlockSpec(memory_space=pl.ANY),
                  pl.BlockSpec((1, tm, d), lambda bi, i: (bi, i, 0)),
                  pl.BlockSpec((1, 1, d), lambda bi, i: (bi, 0, 0)),
                  pl.BlockSpec((tm, TOP_K), lambda bi, i: (off + bi * nt + i, 0)),
                  pl.BlockSpec(g_final.shape, lambda bi, i: (0, 0))],
        out_specs=pl.BlockSpec((1, tm, d), lambda bi, i: (bi, i, 0)),
        out_shape=jax.ShapeDtypeStruct(x.shape, F32),
        scratch_shapes=[pltpu.SMEM((2, 1, TOP_K * tm), jnp.int32),
                        pltpu.VMEM((2, TOP_K * tm, chunks, LANES), F32),
                        pltpu.SemaphoreType.DMA((2,)), pltpu.SemaphoreType.DMA((2,))],
        compiler_params=_cparams(("arbitrary", "arbitrary")),
        name="combine",
    )(dest_tiles, y_rows, x, gt, w_tok, g_final)


def _rope_tables(l):
    rows = l // GRID_W
    row = jnp.repeat(jnp.arange(rows, dtype=F32), GRID_W)
    col = jnp.tile(jnp.arange(GRID_W, dtype=F32), rows)
    inv_freq = ROPE_THETA ** (-jnp.arange(ROPE_FREQS, dtype=F32) * 2.0 / (2 * ROPE_FREQS))
    ar, ac = row[:, None] * inv_freq, col[:, None] * inv_freq
    ones = jnp.ones((l, ROPE_LANE0), F32)
    zpad = jnp.zeros((l, HEAD_PAD - ROPE_LANE0 - QK_ROPE), F32)
    cos_t = jnp.concatenate([ones, jnp.cos(ar), jnp.cos(ar), jnp.cos(ac), jnp.cos(ac), zpad], axis=1)
    sin_t = jnp.concatenate([0.0 * ones, -jnp.sin(ar), jnp.sin(ar), -jnp.sin(ac), jnp.sin(ac), zpad], axis=1)
    return cos_t, sin_t


def _identity_tables(l):
    lane = jnp.arange(HEAD_PAD)
    cos_t = jnp.broadcast_to((lane < ROPE_LANE0 + QK_ROPE).astype(F32), (l, HEAD_PAD))
    return cos_t, jnp.zeros((l, HEAD_PAD), F32)


def _block_diag(w):
    g, a, b = w.shape
    out = jnp.zeros((g * a, g * b), w.dtype)
    for i in range(g):
        out = out.at[i * a:(i + 1) * a, i * b:(i + 1) * b].set(w[i])
    return out


def _prep_layer(w_in, w_pool, w_fno, w_uq, w_ukv):
    d = w_in.shape[0]
    kr = jnp.zeros((d, HEAD_PAD), w_in.dtype).at[:, ROPE_LANE0:ROPE_LANE0 + QK_ROPE].set(w_in[:, COL_KR:])
    w_in_p = jnp.concatenate([w_in[:, :COL_KR], kr], axis=1).astype(BF16)
    qk = QK_NOPE + QK_ROPE
    wq = w_uq.reshape(Q_LORA, N_HEADS, qk)
    wq = jnp.pad(wq, ((0, 0), (0, 0), (0, HEAD_PAD - qk))).reshape(Q_LORA, N_HEADS * HEAD_PAD).astype(BF16)
    wkv = w_ukv.reshape(KV_LORA, N_HEADS, QK_NOPE + V_DIM)
    wk = jnp.pad(wkv[..., :QK_NOPE], ((0, 0), (0, 0), (0, HEAD_PAD - QK_NOPE)))
    wv = jnp.pad(wkv[..., QK_NOPE:], ((0, 0), (0, 0), (0, HEAD_PAD - V_DIM)))
    wkv_p = jnp.concatenate([wk.reshape(KV_LORA, -1), wv.reshape(KV_LORA, -1)], axis=1).astype(BF16)
    return w_in_p, wq, wkv_p, _block_diag(w_pool).astype(BF16), _block_diag(w_fno).astype(BF16)


def _routing_tables(idx_t, rank_t, counts, n_tok, tm):
    counts = counts.reshape(-1).astype(jnp.int32)
    padded = (counts + EXPERT_BLOCK - 1) // EXPERT_BLOCK * EXPERT_BLOCK
    pad_end = jnp.cumsum(padded)
    pad_start = pad_end - padded
    dest = pad_start[idx_t] + rank_t
    n_assign = n_tok * TOP_K
    n_rows = (n_assign + N_EXPERTS * (EXPERT_BLOCK - 1) + EXPERT_BLOCK - 1) // EXPERT_BLOCK * EXPERT_BLOCK
    n_blocks = n_rows // EXPERT_BLOCK
    tok = jnp.broadcast_to(jnp.arange(n_tok, dtype=jnp.int32)[None, :], dest.shape)
    src = jnp.zeros((n_rows,), jnp.int32).at[dest.reshape(-1)].set(tok.reshape(-1))
    block_start = jnp.arange(n_blocks, dtype=jnp.int32) * EXPERT_BLOCK
    block_expert = jnp.minimum(jnp.searchsorted(pad_end, block_start, side='right'),
                               N_EXPERTS - 1).astype(jnp.int32)
    n_used = (pad_end[-1:] // EXPERT_BLOCK).astype(jnp.int32)
    dest_tiles = dest.reshape(TOP_K, n_tok // tm, tm).transpose(1, 0, 2).reshape(n_tok // tm, 1, TOP_K * tm)
    return src.reshape(n_blocks, 1, EXPERT_BLOCK), block_expert, n_used, dest_tiles


TM_PROJ = 256
TM_MIX = 256
TM_FFN = 256
TM_COMBINE = 128
ATTN_TQ = 2048
ATTN_TK = 512


def kernel(x, c, ctx, c_ctx, w_mod, b_mod, g_mix, g_ffn, w_in, w_pool, pool_scale, w_fno, g_q, w_uq,
           g_kv, w_ukv, w_out, w_router, router_bias, w_gate, w_up, w_down, w_sh_gate, w_sh_up,
           w_sh_down, g_final):
    b, l, d = x.shape
    lc = ctx.shape[1]
    depth = w_mod.shape[0]
    cc = jnp.zeros((SUBLANES, d), F32).at[:b].set(c).at[b].set(c_ctx)
    mods = _modulation(cc, w_mod, b_mod)
    cos_t, sin_t = _rope_tables(l)
    cos_i, sin_i = _identity_tables(lc)
    xc = ctx
    row = lambda v: v.reshape(1, -1)
    for li in range(depth):
        last = li == depth - 1
        m = mods[li].reshape(SUBLANES, N_MOD, d)
        lat = [m[:b, j][:, None, :] for j in range(N_MOD)]
        cm = [jnp.broadcast_to(m[b, j][None, None, :], (b, 1, d)) for j in range(N_MOD)]
        w_in_p, wq_p, wkv_p, wp_bd, wf_bd = _prep_layer(w_in[li], w_pool[li], w_fno[li], w_uq[li], w_ukv[li])
        w_out_b = w_out[li].astype(BF16)
        gq, gkv, gm = row(g_q[li]), row(g_kv[li]), row(g_mix[li])

        up, uf, q, k, v = _in_projection(x, lat[0], lat[1], gm, w_in_p, gq, wq_p, gkv, wkv_p,
                                         cos_t, sin_t, TM_PROJ)
        upc, ufc, qc, kc, vc = _in_projection(xc, cm[0], cm[1], gm, w_in_p, gq, wq_p, gkv, wkv_p,
                                              cos_i, sin_i, TM_PROJ)
        attn = _attention(q, kc, vc, k, v, tq=ATTN_TQ, tk=ATTN_TK)
        f = _fourier(uf)
        ps = row(pool_scale[li])
        x = _mixer_output(x, lat[2], up, f, attn, wp_bd, ps, wf_bd, w_out_b, TM_MIX)
        if not last:
            attn_c = _attention(qc, kc, vc, tq=ATTN_TQ, tk=ATTN_TK)
            fc = _fourier(ufc)
            xc = _mixer_output(xc, cm[2], upc, fc, attn_c, wp_bd, ps, wf_bd, w_out_b, TM_MIX)

        n_tok = b * l + (0 if last else b * lc)
        chunks = d // LANES
        bufs = (jnp.zeros((n_tok, chunks, LANES), F32), jnp.zeros((TOP_K, n_tok), jnp.int32),
                jnp.zeros((TOP_K, n_tok), F32), jnp.zeros((TOP_K, n_tok), jnp.int32))
        w_rt = w_router[li].T
        rb = router_bias[li].reshape(N_EXPERTS, 1)
        wsg, wsu, wsd = w_sh_gate[li].astype(BF16), w_sh_up[li].astype(BF16), w_sh_down[li].astype(BF16)
        gf = row(g_ffn[li])
        cnt0 = jnp.zeros((N_EXPERTS, 1), F32)
        x, bufs, cnt = _ffn_front(x, lat[3], lat[4], lat[5], gf, w_rt, rb, wsg, wsu, wsd, cnt0, bufs, 0, TM_FFN)
        if not last:
            xc, bufs, cnt = _ffn_front(xc, cm[3], cm[4], cm[5], gf, w_rt, rb, wsg, wsu, wsd, cnt, bufs,
                                       b * l, TM_FFN)
        h_rows, idx_t, w_t, rank_t = bufs
        src, block_expert, n_used, dest_tiles = _routing_tables(idx_t, rank_t, cnt, n_tok, TM_COMBINE)
        y_rows = _experts(block_expert, n_used, src, h_rows, w_gate[li].astype(BF16),
                          w_up[li].astype(BF16), w_down[li].astype(BF16))
        w_tok = w_t.T
        x = _combine(dest_tiles, y_rows, x, lat[5], w_tok, row(g_final), 0, TM_COMBINE, last)
        if not last:
            xc = _combine(dest_tiles, y_rows, xc, cm[5], w_tok, row(g_final), b * l, TM_COMBINE, False)
    return x
```

```python
import functools
import math

import jax
import jax.numpy as jnp
import numpy as np
from jax import lax
from jax.experimental import pallas as pl
from jax.experimental.pallas import tpu as pltpu

F32 = jnp.float32
BF16 = jnp.bfloat16
HIGHEST = lax.Precision.HIGHEST

EPS = 1e-6
N_MOD = 6
GRID_W = 64
POOL_WINDOWS = (2, 4, 8, 16)
GROUP_DIM = 64
N_GROUPS = 4
MIX_GROUP_WIDTH = N_GROUPS * GROUP_DIM
N_HEADS = 8
QK_NOPE = 64
QK_ROPE = 32
V_DIM = 64
Q_LORA = 384
KV_LORA = 256
ROPE_FREQS = QK_ROPE // 4
ROPE_THETA = 10000.0
SOFTMAX_SCALE = (QK_NOPE + QK_ROPE) ** -0.5
N_EXPERTS = 64
TOP_K = 8
N_EXPERT_GROUPS = 8
TOPK_GROUPS = 4
EXPERTS_PER_GROUP = N_EXPERTS // N_EXPERT_GROUPS
ROUTED_SCALE = 2.5

LANES = 128
SUBLANES = 8
HEAD_PAD = LANES
VMEM_LIMIT = 56 * 1024 * 1024

COL_FNO = MIX_GROUP_WIDTH
COL_Q = COL_FNO + MIX_GROUP_WIDTH
COL_KV = COL_Q + Q_LORA
COL_KR = COL_KV + KV_LORA
IN_PAD = COL_KR + HEAD_PAD
ROPE_LANE0 = QK_NOPE

MOE_TILE = 256
PIECE = 16
PIECES_PER_TILE = MOE_TILE * TOP_K // PIECE + N_EXPERTS
TILE_ROWS = PIECES_PER_TILE * PIECE
BLOCK_PIECES = 16
EXPERT_BLOCK = BLOCK_PIECES * PIECE


def _cparams(sem, vmem=VMEM_LIMIT):
    return pltpu.CompilerParams(dimension_semantics=sem, vmem_limit_bytes=vmem)


def _rms(x, g):
    return x * lax.rsqrt(jnp.mean(x * x, axis=-1, keepdims=True) + EPS) * g


def _silu(x):
    return x * jax.nn.sigmoid(x)


def _mod_kernel(c_ref, w_ref, b_ref, o_ref):
    s = _silu(c_ref[...])
    o_ref[0] = jnp.dot(s, w_ref[0], preferred_element_type=F32, precision=HIGHEST) + b_ref[0]


def _modulation(cc, w_mod, b_mod):
    depth, d, nd = w_mod.shape
    n_chunks = nd // d
    return pl.pallas_call(
        _mod_kernel,
        grid=(depth, n_chunks),
        in_specs=[pl.BlockSpec((SUBLANES, d), lambda l, j: (0, 0)),
                  pl.BlockSpec((1, d, d), lambda l, j: (l, 0, j)),
                  pl.BlockSpec((1, 1, d), lambda l, j: (l, 0, j))],
        out_specs=pl.BlockSpec((1, SUBLANES, d), lambda l, j: (l, 0, j)),
        out_shape=jax.ShapeDtypeStruct((depth, SUBLANES, nd), F32),
        compiler_params=_cparams(("parallel", "parallel")),
        name="modulation",
    )(cc, w_mod, b_mod.reshape(depth, 1, nd))


def _rope(t, c, s):
    lane = lax.broadcasted_iota(jnp.int32, t.shape, 1)
    first_half = (lane % (2 * ROPE_FREQS)) < ROPE_FREQS
    partner = jnp.where(first_half, pltpu.roll(t, LANES - ROPE_FREQS, 1), pltpu.roll(t, ROPE_FREQS, 1))
    return t * c + partner * s


def _inproj_kernel(x_ref, sh_ref, sc_ref, g_ref, win_ref, gq_ref, wuq_ref, gkv_ref, wukv_ref,
                   cos_ref, sin_ref, up_ref, uf_ref, q_ref, k_ref, v_ref):
    x = x_ref[0]
    h = _rms(x, g_ref[...]) * (1.0 + sc_ref[0]) + sh_ref[0]
    u = jnp.dot(h.astype(BF16), win_ref[...], preferred_element_type=F32)
    up_ref[0] = u[:, :COL_FNO]
    uf_ref[0] = u[:, COL_FNO:COL_Q]
    cq = _rms(u[:, COL_Q:COL_KV], gq_ref[...])
    q = jnp.dot(cq.astype(BF16), wuq_ref[...], preferred_element_type=F32)
    ckv = _rms(u[:, COL_KV:COL_KR], gkv_ref[...])
    kv = jnp.dot(ckv.astype(BF16), wukv_ref[...], preferred_element_type=F32)
    cos = cos_ref[...]
    sin = sin_ref[...]
    kr = _rope(u[:, COL_KR:IN_PAD], cos, sin)
    lane = lax.broadcasted_iota(jnp.int32, kr.shape, 1)
    ones_col = (lane == V_DIM).astype(F32)
    q_scale = SOFTMAX_SCALE * math.log2(math.e)
    for hd in range(N_HEADS):
        lo = hd * HEAD_PAD
        q_ref[0, hd] = (_rope(q[:, lo:lo + HEAD_PAD], cos, sin) * q_scale).astype(BF16)
        k_ref[0, hd] = (kv[:, lo:lo + HEAD_PAD] + kr).astype(BF16)
        vo = N_HEADS * HEAD_PAD + lo
        v_ref[0, hd] = (kv[:, vo:vo + HEAD_PAD] + ones_col).astype(BF16)


def _in_projection(x, sh, sc, g, w_in_p, g_q, w_uq_p, g_kv, w_ukv_p, cos_t, sin_t, tm):
    b, l, d = x.shape
    tm = min(tm, l)
    full = lambda a: pl.BlockSpec(a.shape, lambda bi, i: (0,) * a.ndim)
    vec = pl.BlockSpec((1, 1, d), lambda bi, i: (bi, 0, 0))
    head_out = pl.BlockSpec((1, N_HEADS, tm, HEAD_PAD), lambda bi, i: (bi, 0, i, 0))
    head_shape = jax.ShapeDtypeStruct((b, N_HEADS, l, HEAD_PAD), BF16)
    return pl.pallas_call(
        _inproj_kernel,
        grid=(b, l // tm),
        in_specs=[pl.BlockSpec((1, tm, d), lambda bi, i: (bi, i, 0)), vec, vec,
                  full(g), full(w_in_p), full(g_q), full(w_uq_p), full(g_kv), full(w_ukv_p),
                  pl.BlockSpec((tm, HEAD_PAD), lambda bi, i: (i, 0)),
                  pl.BlockSpec((tm, HEAD_PAD), lambda bi, i: (i, 0))],
        out_specs=[pl.BlockSpec((1, tm, MIX_GROUP_WIDTH), lambda bi, i: (bi, i, 0)),
                   pl.BlockSpec((1, tm, MIX_GROUP_WIDTH), lambda bi, i: (bi, i, 0)),
                   head_out, head_out, head_out],
        out_shape=[jax.ShapeDtypeStruct((b, l, MIX_GROUP_WIDTH), F32),
                   jax.ShapeDtypeStruct((b, l, MIX_GROUP_WIDTH), F32),
                   head_shape, head_shape, head_shape],
        compiler_params=_cparams(("parallel", "parallel")),
        name="in_projection",
    )(x, sh, sc, g, w_in_p, g_q, w_uq_p, g_kv, w_ukv_p, cos_t, sin_t)


ATTN_HEADS_PER_STEP = 2


def _attn_kernel(*refs, tk, n_chunks):
    if n_chunks:
        q_ref, kc_ref, vc_ref, k_ref, v_ref, o_ref = refs
    else:
        q_ref, kc_ref, vc_ref, o_ref = refs
    tq = q_ref.shape[2]
    nt = (((1,), (1,)), ((), ()))
    heads = range(ATTN_HEADS_PER_STEP)
    qs = [q_ref[0, hh] for hh in heads]

    def step(q, kc, vc, m, acc):
        s = lax.dot_general(q, kc, nt, preferred_element_type=F32)
        m_new = jnp.maximum(m, jnp.max(s, axis=-1, keepdims=True))
        alpha = jnp.exp2(m - m_new)
        p = jnp.exp2(s - m_new).astype(BF16)
        return m_new, alpha * acc + jnp.dot(p, vc, preferred_element_type=F32)

    carry = tuple(step(qs[hh], kc_ref[0, hh], vc_ref[0, hh],
                       jnp.full((tq, 1), -1e30, F32), jnp.zeros((tq, HEAD_PAD), F32)) for hh in heads)
    if n_chunks:
        def body(j, carry):
            off = pl.multiple_of(j * tk, tk)
            return tuple(step(qs[hh], k_ref[0, hh, pl.ds(off, tk), :], v_ref[0, hh, pl.ds(off, tk), :],
                              *carry[hh]) for hh in heads)

        carry = lax.fori_loop(0, n_chunks, body, carry)
    for hh in heads:
        acc = carry[hh][1]
        out = acc[:, :V_DIM] / acc[:, V_DIM:V_DIM + 1]
        o_ref[0, :, hh * V_DIM:(hh + 1) * V_DIM] = out.astype(o_ref.dtype)


def _attention(q, k_ctx, v_ctx, k=None, v=None, *, tq, tk):
    b, h, l, _ = q.shape
    lc = k_ctx.shape[2]
    tq = min(tq, l)
    hps = ATTN_HEADS_PER_STEP
    qspec = pl.BlockSpec((1, hps, tq, HEAD_PAD), lambda bi, hi, i: (bi, hi, i, 0))
    cspec = pl.BlockSpec((1, hps, lc, HEAD_PAD), lambda bi, hi, i: (bi, hi, 0, 0))
    args, specs, n_chunks = [q, k_ctx, v_ctx], [qspec, cspec, cspec], 0
    if k is not None:
        lk = k.shape[2]
        tk = min(tk, lk)
        n_chunks = lk // tk
        kspec = pl.BlockSpec((1, hps, lk, HEAD_PAD), lambda bi, hi, i: (bi, hi, 0, 0))
        args += [k, v]
        specs += [kspec, kspec]
    return pl.pallas_call(
        functools.partial(_attn_kernel, tk=tk, n_chunks=n_chunks),
        grid=(b, h // hps, l // tq),
        in_specs=specs,
        out_specs=pl.BlockSpec((1, tq, hps * V_DIM), lambda bi, hi, i: (bi, i, hi)),
        out_shape=jax.ShapeDtypeStruct((b, l, h * V_DIM), BF16),
        compiler_params=_cparams(("parallel", "parallel", "arbitrary")),
        name="attention",
    )(*args)


def _dft_step1_kernel(x_ref, c_ref, s_ref, yr_ref, yi_ref):
    xb = x_ref[0].astype(BF16)
    yr_ref[0] = jnp.dot(c_ref[...], xb, preferred_element_type=F32)
    yi_ref[0] = jnp.dot(s_ref[...], xb, preferred_element_type=F32)


def _dft_step2_kernel(yr_ref, yi_ref, tc_ref, ts_ref, c_ref, s_ref, cc_ref, sc_ref, o_ref):
    for j in range(SUBLANES):
        yr, yi = yr_ref[0, j], yi_ref[0, j]
        tc, ts = tc_ref[j], ts_ref[j]
        zr = (yr * tc - yi * ts).astype(BF16)
        zi = (yi * tc + yr * ts).astype(BF16)
        a = (jnp.dot(c_ref[...], zr, preferred_element_type=F32)
             - jnp.dot(s_ref[...], zi, preferred_element_type=F32))
        bm = (jnp.dot(s_ref[...], zr, preferred_element_type=F32)
              + jnp.dot(c_ref[...], zi, preferred_element_type=F32))
        o_ref[0, :, j, :] = (jnp.dot(a.astype(BF16), cc_ref[...], preferred_element_type=F32)
                             - jnp.dot(bm.astype(BF16), sc_ref[...], preferred_element_type=F32))


def _dft_dense_kernel(x_ref, c_ref, s_ref, cc_ref, sc_ref, o_ref):
    xb = x_ref[0].astype(BF16)
    a = jnp.dot(c_ref[...], xb, preferred_element_type=F32)
    bm = jnp.dot(s_ref[...], xb, preferred_element_type=F32)
    o_ref[0] = (jnp.dot(a.astype(BF16), cc_ref[...], preferred_element_type=F32)
                - jnp.dot(bm.astype(BF16), sc_ref[...], preferred_element_type=F32))


def _dft_mats(n):
    ang = 2.0 * np.pi * np.outer(np.arange(n), np.arange(n)) / n
    return np.cos(ang), np.sin(ang)


def _channel_dft(l):
    c, s = _dft_mats(GROUP_DIM)
    eye = np.eye(N_GROUPS)
    norm = 1.0 / math.sqrt(l * GROUP_DIM)
    return (jnp.asarray(np.kron(eye, c) * norm, BF16), jnp.asarray(np.kron(eye, s) * norm, BF16))


def _dft_factors(l):
    n1 = 1 << (int(math.log2(l)) // 2)
    return n1, l // n1


def _fourier(u_fno, lane_block=4096):
    b, l, c = u_fno.shape
    cc, sc = _channel_dft(l)
    full = lambda a: pl.BlockSpec(a.shape, lambda *_: (0,) * a.ndim)
    if l <= 512:
        cm, sm = _dft_mats(l)
        cm, sm = jnp.asarray(cm, BF16), jnp.asarray(sm, BF16)
        blk = pl.BlockSpec((1, l, c), lambda bi: (bi, 0, 0))
        return pl.pallas_call(
            _dft_dense_kernel, grid=(b,),
            in_specs=[blk, full(cm), full(sm), full(cc), full(sc)],
            out_specs=blk, out_shape=jax.ShapeDtypeStruct((b, l, c), F32),
            compiler_params=_cparams(("parallel",)), name="dft_dense",
        )(u_fno, cm, sm, cc, sc)
    n1, n2 = _dft_factors(l)
    c1, s1 = _dft_mats(n1)
    c2, s2 = _dft_mats(n2)
    c1, s1, c2, s2 = (jnp.asarray(m, BF16) for m in (c1, s1, c2, s2))
    ang = 2.0 * np.pi * np.outer(np.arange(n1), np.arange(n2)) / l
    tc = jnp.asarray(np.cos(ang)[:, :, None], F32)
    ts = jnp.asarray(np.sin(ang)[:, :, None], F32)
    w = n2 * c
    tn = min(lane_block, w)
    x2 = u_fno.reshape(b, n1, w)
    yspec = pl.BlockSpec((1, n1, tn), lambda bi, i: (bi, 0, i))
    yr, yi = pl.pallas_call(
        _dft_step1_kernel, grid=(b, w // tn),
        in_specs=[yspec, full(c1), full(s1)],
        out_specs=[yspec, yspec],
        out_shape=[jax.ShapeDtypeStruct((b, n1, w), F32)] * 2,
        compiler_params=_cparams(("parallel", "parallel")), name="dft_step1",
    )(x2, c1, s1)
    yr = yr.reshape(b, n1, n2, c)
    yi = yi.reshape(b, n1, n2, c)
    slab = pl.BlockSpec((1, SUBLANES, n2, c), lambda bi, i: (bi, i, 0, 0))
    tw = pl.BlockSpec((SUBLANES, n2, 1), lambda bi, i: (i, 0, 0))
    out = pl.pallas_call(
        _dft_step2_kernel, grid=(b, n1 // SUBLANES),
        in_specs=[slab, slab, tw, tw, full(c2), full(s2), full(cc), full(sc)],
        out_specs=pl.BlockSpec((1, n2, SUBLANES, c), lambda bi, i: (bi, 0, i, 0)),
        out_shape=jax.ShapeDtypeStruct((b, n2, n1, c), F32),
        compiler_params=_cparams(("parallel", "parallel")), name="dft_step2",
    )(yr, yi, tc, ts, c2, s2, cc, sc)
    return out.reshape(b, l, c)


POOL_HALO = SUBLANES


def _mixout_kernel(x_ref, gt_ref, up_ref, prev_ref, next_ref, f_ref, at_ref,
                   wp_ref, ps_ref, wf_ref, wo_ref, o_ref, *, seq_len):
    i = pl.program_id(1)
    tm = x_ref.shape[1]
    u = up_ref[0]
    prev = jnp.where(i > 0, prev_ref[0], 0.0)
    nxt = jnp.where(i < pl.num_programs(1) - 1, next_ref[0], 0.0)
    p = jnp.concatenate([prev, u, nxt], axis=0)
    n = tm + 2 * POOL_HALO
    s1 = pltpu.roll(p, 1, 0) + p
    s2 = pltpu.roll(s1, 1, 0) + pltpu.roll(s1, n - 1, 0)
    s4 = pltpu.roll(s2, 2, 0) + pltpu.roll(s2, n - 2, 0)
    s8 = pltpu.roll(s4, 4, 0) + pltpu.roll(s4, n - 4, 0)
    lane = lax.broadcasted_iota(jnp.int32, (tm, MIX_GROUP_WIDTH), 1)
    grp = lane // GROUP_DIM
    lo, hi = POOL_HALO, POOL_HALO + tm
    win = jnp.where(grp == 0, s1[lo:hi],
                    jnp.where(grp == 1, s2[lo:hi], jnp.where(grp == 2, s4[lo:hi], s8[lo:hi])))
    half = jnp.left_shift(1, grp)
    t = i * tm + lax.broadcasted_iota(jnp.int32, (tm, MIX_GROUP_WIDTH), 0)
    cnt = (jnp.minimum(t + half, seq_len) - jnp.maximum(t - half, 0)).astype(F32)
    dlt = win / cnt - u
    pool_y = jnp.dot(dlt.astype(BF16), wp_ref[...], preferred_element_type=F32) * ps_ref[...]
    fno_y = jnp.dot(f_ref[0].astype(BF16), wf_ref[...], preferred_element_type=F32)
    w = MIX_GROUP_WIDTH
    y = (jnp.dot(pool_y.astype(BF16), wo_ref[0:w, :], preferred_element_type=F32)
         + jnp.dot(fno_y.astype(BF16), wo_ref[w:2 * w, :], preferred_element_type=F32)
         + jnp.dot(at_ref[0], wo_ref[2 * w:, :], preferred_element_type=F32))
    o_ref[0] = x_ref[0] + gt_ref[0] * y


def _mixer_output(x, gt, u_pool, f, attn, wp_bd, pool_scale, wf_bd, w_out, tm):
    b, l, d = x.shape
    tm = min(tm, l)
    nb = tm // POOL_HALO
    last = l // POOL_HALO - 1
    full = lambda a: pl.BlockSpec(a.shape, lambda bi, i: (0,) * a.ndim)
    w = MIX_GROUP_WIDTH
    return pl.pallas_call(
        functools.partial(_mixout_kernel, seq_len=l),
        grid=(b, l // tm),
        in_specs=[pl.BlockSpec((1, tm, d), lambda bi, i: (bi, i, 0)),
                  pl.BlockSpec((1, 1, d), lambda bi, i: (bi, 0, 0)),
                  pl.BlockSpec((1, tm, w), lambda bi, i: (bi, i, 0)),
                  pl.BlockSpec((1, POOL_HALO, w), lambda bi, i: (bi, jnp.maximum(i * nb - 1, 0), 0)),
                  pl.BlockSpec((1, POOL_HALO, w), lambda bi, i: (bi, jnp.minimum((i + 1) * nb, last), 0)),
                  pl.BlockSpec((1, tm, w), lambda bi, i: (bi, i, 0)),
                  pl.BlockSpec((1, tm, attn.shape[-1]), lambda bi, i: (bi, i, 0)),
                  full(wp_bd), full(pool_scale), full(wf_bd), full(w_out)],
        out_specs=pl.BlockSpec((1, tm, d), lambda bi, i: (bi, i, 0)),
        out_shape=jax.ShapeDtypeStruct((b, l, d), F32),
        compiler_params=_cparams(("parallel", "arbitrary")),
        name="mixer_output",
    )(x, gt, u_pool, u_pool, u_pool, f, attn, wp_bd, pool_scale, wf_bd, w_out)


def _first_index_of_max(vals, iota, n):
    mx = jnp.max(vals, axis=0, keepdims=True)
    ix = jnp.min(jnp.where(vals == mx, iota, n), axis=0, keepdims=True)
    return mx, ix


def _ffn_front_kernel(*refs, n_lat, has_ctx):
    if has_ctx:
        x_ref, xc_ref = refs[:2]
        refs = refs[2:]
    else:
        x_ref, refs = refs[0], refs[1:]
    (sh_ref, sc_ref, gt_ref, g_ref, wr_ref, rb_ref, wsg_ref, wsu_ref, wsd_ref,
     xo_ref, xs_ref, slot_ref, wt_ref, cnt_ref) = refs
    x = x_ref[0]
    if has_ctx:
        x = jnp.where(pl.program_id(0) < n_lat, x, xc_ref[0])
    tm = x.shape[0]
    h = _rms(x, g_ref[...]) * (1.0 + sc_ref[0]) + sh_ref[0]

    logits = lax.dot_general(wr_ref[...], h, (((1,), (1,)), ((), ())),
                             preferred_element_type=F32, precision=HIGHEST)
    aff = jax.nn.sigmoid(logits)
    sel = aff + rb_ref[...]
    e_iota = lax.broadcasted_iota(jnp.int32, (N_EXPERTS, tm), 0).astype(F32)
    neg = jnp.float32(-jnp.inf)
    gscores = []
    for g in range(N_EXPERT_GROUPS):
        blk = sel[g * EXPERTS_PER_GROUP:(g + 1) * EXPERTS_PER_GROUP]
        it = lax.broadcasted_iota(jnp.int32, blk.shape, 0).astype(F32)
        m1, i1 = _first_index_of_max(blk, it, float(EXPERTS_PER_GROUP))
        m2 = jnp.max(jnp.where(it == i1, neg, blk), axis=0, keepdims=True)
        gscores.append(m1 + m2)
    gs = jnp.concatenate(gscores, axis=0)
    g_iota = lax.broadcasted_iota(jnp.int32, gs.shape, 0).astype(F32)
    gself = jnp.zeros(gs.shape, F32)
    for _ in range(TOPK_GROUPS):
        _, ig = _first_index_of_max(gs, g_iota, float(N_EXPERT_GROUPS))
        hit = g_iota == ig
        gself = jnp.where(hit, 1.0, gself)
        gs = jnp.where(hit, neg, gs)
    emask = jnp.concatenate(
        [jnp.broadcast_to(gself[g:g + 1], (EXPERTS_PER_GROUP, tm)) for g in range(N_EXPERT_GROUPS)], axis=0)
    masked = jnp.where(emask > 0.5, sel, neg)
    hits, wts = [], []
    chosen = jnp.zeros((N_EXPERTS, tm), F32)
    for _ in range(TOP_K):
        _, ie = _first_index_of_max(masked, e_iota, float(N_EXPERTS))
        hit = e_iota == ie
        hits.append(hit)
        wts.append(jnp.sum(jnp.where(hit, aff, 0.0), axis=0, keepdims=True))
        chosen = jnp.where(hit, 1.0, chosen)
        masked = jnp.where(hit, neg, masked)
    wsum = wts[0]
    for w in wts[1:]:
        wsum = wsum + w
    scale = ROUTED_SCALE / wsum
    wt_ref[...] = jnp.concatenate([w * scale for w in wts], axis=0)

    r_io = lax.broadcasted_iota(jnp.int32, (tm, tm), 0)
    c_io = lax.broadcasted_iota(jnp.int32, (tm, tm), 1)
    tri = (r_io <= c_io).astype(BF16)
    incl = jnp.dot(chosen.astype(BF16), tri, preferred_element_type=F32)
    count = jnp.sum(chosen, axis=1, keepdims=True)
    n_pieces = jnp.floor((count + (PIECE - 1)) * (1.0 / PIECE))
    cnt_ref[0] = n_pieces
    e_r = lax.broadcasted_iota(jnp.int32, (N_EXPERTS, N_EXPERTS), 0)
    e_c = lax.broadcasted_iota(jnp.int32, (N_EXPERTS, N_EXPERTS), 1)
    before = (e_c < e_r).astype(BF16)
    piece_off = jnp.dot(before, jnp.broadcast_to(n_pieces, (N_EXPERTS, LANES)).astype(BF16),
                        preferred_element_type=F32)[:, :1]
    pos = piece_off * PIECE + incl - chosen
    slots = [jnp.sum(jnp.where(hit, pos, 0.0), axis=0, keepdims=True) for hit in hits]
    slot_ref[...] = jnp.concatenate(slots, axis=0).astype(jnp.int32)

    row_io = lax.broadcasted_iota(jnp.int32, (TILE_ROWS, tm), 0).astype(F32)
    onehot = jnp.zeros((TILE_ROWS, tm), F32)
    for s in slots:
        onehot = jnp.where(row_io == s, 1.0, onehot)
    onehot = onehot.astype(BF16)
    hb = h.astype(BF16)
    d = hb.shape[1]
    nc = 2 * LANES
    for j in range(d // nc):
        xs_ref[0, :, j * nc:(j + 1) * nc] = jnp.dot(
            onehot, hb[:, j * nc:(j + 1) * nc], preferred_element_type=F32).astype(BF16)

    a = jnp.dot(hb, wsg_ref[...], preferred_element_type=F32)
    bb = jnp.dot(hb, wsu_ref[...], preferred_element_type=F32)
    sh_out = jnp.dot((_silu(a) * bb).astype(BF16), wsd_ref[...], preferred_element_type=F32)
    xo_ref[0] = x + gt_ref[0] * sh_out


def _ffn_front(x, xc, sh, sc, gt, g, w_rt, rb, wsg, wsu, wsd):
    b, l, d = x.shape
    tm = MOE_TILE
    nt = l // tm
    n_lat = b * nt
    has_ctx = xc is not None
    ntc = xc.shape[1] // tm if has_ctx else 0
    n_all = n_lat + b * ntc
    full = lambda a: pl.BlockSpec(a.shape, lambda i: (0,) * a.ndim)

    def lat_map(i):
        j = jnp.minimum(i, n_lat - 1)
        return (j // nt, j % nt, 0)

    def ctx_map(i):
        j = jnp.clip(i - n_lat, 0, b * ntc - 1)
        return (j // ntc, j % ntc, 0)

    vec = pl.BlockSpec((1, 1, d), lambda i: (jnp.where(i < n_lat, i // nt, b), 0, 0))
    xspecs = [pl.BlockSpec((1, tm, d), lat_map)] + ([pl.BlockSpec((1, tm, d), ctx_map)] if has_ctx else [])
    xargs = [x] + ([xc] if has_ctx else [])
    tokmap = lambda i: (0, i)
    return pl.pallas_call(
        functools.partial(_ffn_front_kernel, n_lat=n_lat, has_ctx=has_ctx),
        grid=(n_all,),
        in_specs=xspecs + [vec, vec, vec, full(g), full(w_rt), full(rb), full(wsg), full(wsu), full(wsd)],
        out_specs=[pl.BlockSpec((1, tm, d), lambda i: (i, 0, 0)),
                   pl.BlockSpec((1, TILE_ROWS, d), lambda i: (i, 0, 0)),
                   pl.BlockSpec((TOP_K, tm), tokmap), pl.BlockSpec((TOP_K, tm), tokmap),
                   pl.BlockSpec((1, N_EXPERTS, 1), lambda i: (i, 0, 0))],
        out_shape=[jax.ShapeDtypeStruct((n_all, tm, d), F32),
                   jax.ShapeDtypeStruct((n_all, TILE_ROWS, d), BF16),
                   jax.ShapeDtypeStruct((TOP_K, n_all * tm), jnp.int32),
                   jax.ShapeDtypeStruct((TOP_K, n_all * tm), F32),
                   jax.ShapeDtypeStruct((n_all, N_EXPERTS, 1), F32)],
        compiler_params=_cparams(("parallel",)),
        name="ffn_front",
    )(*xargs, sh, sc, gt, g, w_rt, rb, wsg, wsu, wsd)


def _expert_kernel(bexp_ref, nused_ref, rlen_ref, rsrc_ref, xs_ref, wgu_ref, wd_ref, y_ref,
                   state, ids, oids, xbuf, ybuf, sem_in, sem_out, *, runs_per_expert):
    i = pl.program_id(0)
    n_used = nused_ref[0]
    last_run = rlen_ref.shape[0] - 1

    def copy_in(pid, slot, p):
        return pltpu.make_async_copy(xs_ref.at[pid], xbuf.at[slot, pl.ds(p * PIECE, PIECE), :], sem_in.at[slot])

    def copy_out(pid, slot, p):
        return pltpu.make_async_copy(ybuf.at[slot, pl.ds(p * PIECE, PIECE), :], y_ref.at[pid], sem_out.at[slot])

    def issue_gather(blk, slot):
        r_lo = bexp_ref[blk] * runs_per_expert
        r_end = r_lo + runs_per_expert
        fresh = state[0] < r_lo
        r = jnp.where(fresh, r_lo, state[0])
        o = jnp.where(fresh, 0, state[1])
        for p in range(BLOCK_PIECES):
            def exhausted(c):
                return jnp.logical_and(c[0] < r_end, c[1] >= rlen_ref[jnp.minimum(c[0], last_run)])

            r, o = lax.while_loop(exhausted, lambda c: (c[0] + 1, jnp.int32(0)), (r, o))
            valid = r < r_end
            pid = jnp.where(valid, rsrc_ref[jnp.minimum(r, last_run)] + o, -1)
            ids[slot, p] = pid

            @pl.when(valid)
            def _(pid=pid, p=p):
                copy_in(pid, slot, p).start()

            o = o + valid.astype(jnp.int32)
        state[0] = r
        state[1] = o

    def wait_pieces(table, slot, make_copy):
        for p in range(BLOCK_PIECES):
            pid = table[slot, p]

            @pl.when(pid >= 0)
            def _(pid=pid, p=p):
                make_copy(pid, slot, p).wait()

    @pl.when(i < n_used)
    def _():
        slot = i % 2

        @pl.when(i == 0)
        def _():
            xbuf[...] = jnp.zeros(xbuf.shape, xbuf.dtype)
            state[0] = 0
            state[1] = 0
            issue_gather(0, 0)

        @pl.when(i + 1 < n_used)
        def _():
            issue_gather(i + 1, (i + 1) % 2)

        wait_pieces(ids, slot, copy_in)

        @pl.when(i >= 2)
        def _():
            wait_pieces(oids, slot, copy_out)

        gu = jnp.dot(xbuf[slot], wgu_ref[0], preferred_element_type=F32)
        f = gu.shape[1] // 2
        act = (_silu(gu[:, :f]) * gu[:, f:]).astype(BF16)
        ybuf[slot] = jnp.dot(act, wd_ref[0], preferred_element_type=F32).astype(BF16)
        for p in range(BLOCK_PIECES):
            pid = ids[slot, p]
            oids[slot, p] = pid

            @pl.when(pid >= 0)
            def _(pid=pid, p=p):
                copy_out(pid, slot, p).start()

        @pl.when(i == n_used - 1)
        def _():
            wait_pieces(oids, slot, copy_out)

            @pl.when(i >= 1)
            def _():
                wait_pieces(oids, 1 - slot, copy_out)


def _experts(block_expert, n_used, run_len, run_src, xs_pieces, w_gu, w_down, runs_per_expert):
    n_blocks = block_expert.shape[0]
    _, _, d = xs_pieces.shape
    _, _, f2 = w_gu.shape
    grid_spec = pltpu.PrefetchScalarGridSpec(
        num_scalar_prefetch=4,
        grid=(n_blocks,),
        in_specs=[pl.BlockSpec(memory_space=pl.ANY),
                  pl.BlockSpec((1, d, f2), lambda i, be, nu, rl, rs: (be[i], 0, 0)),
                  pl.BlockSpec((1, f2 // 2, d), lambda i, be, nu, rl, rs: (be[i], 0, 0))],
        out_specs=pl.BlockSpec(memory_space=pl.ANY),
        scratch_shapes=[pltpu.SMEM((2,), jnp.int32),
                        pltpu.SMEM((2, BLOCK_PIECES), jnp.int32),
                        pltpu.SMEM((2, BLOCK_PIECES), jnp.int32),
                        pltpu.VMEM((2, EXPERT_BLOCK, d), BF16),
                        pltpu.VMEM((2, EXPERT_BLOCK, d), BF16),
                        pltpu.SemaphoreType.DMA((2,)), pltpu.SemaphoreType.DMA((2,))],
    )
    return pl.pallas_call(
        functools.partial(_expert_kernel, runs_per_expert=runs_per_expert),
        grid_spec=grid_spec,
        out_shape=jax.ShapeDtypeStruct(xs_pieces.shape, xs_pieces.dtype),
        input_output_aliases={4: 0},
        compiler_params=_cparams(("arbitrary",)),
        name="experts",
    )(block_expert, n_used, run_len, run_src, xs_pieces, w_gu, w_down)


def _combine_kernel(y_ref, slot_ref, w_ref, x_ref, gt_ref, gf_ref, o_ref, *, final):
    tm = x_ref.shape[1]
    slot = slot_ref[...].astype(F32)
    w = w_ref[...]
    col_io = lax.broadcasted_iota(jnp.int32, (tm, TILE_ROWS), 1).astype(F32)
    sel = jnp.zeros((tm, TILE_ROWS), F32)
    for k in range(TOP_K):
        sel = jnp.where(col_io == slot[:, k:k + 1], w[:, k:k + 1], sel)
    routed = jnp.dot(sel.astype(BF16), y_ref[0], preferred_element_type=F32)
    out = x_ref[0] + gt_ref[0] * routed
    if final:
        out = _rms(out, gf_ref[...])
    o_ref[0] = out


def _combine(y_tiles, slot_tok, w_tok, x_tiles, gt, g_final, seq_shape, tile_off, final):
    b, l = seq_shape
    _, tm, d = x_tiles.shape
    nt = l // tm
    tile = lambda bi, i: tile_off + bi * nt + i
    return pl.pallas_call(
        functools.partial(_combine_kernel, final=final),
        grid=(b, nt),
        in_specs=[pl.BlockSpec((1, TILE_ROWS, d), lambda bi, i: (tile(bi, i), 0, 0)),
                  pl.BlockSpec((tm, TOP_K), lambda bi, i: (tile(bi, i), 0)),
                  pl.BlockSpec((tm, TOP_K), lambda bi, i: (tile(bi, i), 0)),
                  pl.BlockSpec((1, tm, d), lambda bi, i: (tile(bi, i), 0, 0)),
                  pl.BlockSpec((1, 1, d), lambda bi, i: (bi, 0, 0)),
                  pl.BlockSpec(g_final.shape, lambda bi, i: (0, 0))],
        out_specs=pl.BlockSpec((1, tm, d), lambda bi, i: (bi, i, 0)),
        out_shape=jax.ShapeDtypeStruct((b, l, d), F32),
        compiler_params=_cparams(("parallel", "parallel")),
        name="combine",
    )(y_tiles, slot_tok, w_tok, x_tiles, gt, g_final)


def _rope_tables(l):
    rows = l // GRID_W
    row = jnp.repeat(jnp.arange(rows, dtype=F32), GRID_W)
    col = jnp.tile(jnp.arange(GRID_W, dtype=F32), rows)
    inv_freq = ROPE_THETA ** (-jnp.arange(ROPE_FREQS, dtype=F32) * 2.0 / (2 * ROPE_FREQS))
    ar, ac = row[:, None] * inv_freq, col[:, None] * inv_freq
    ones = jnp.ones((l, ROPE_LANE0), F32)
    zpad = jnp.zeros((l, HEAD_PAD - ROPE_LANE0 - QK_ROPE), F32)
    cos_t = jnp.concatenate([ones, jnp.cos(ar), jnp.cos(ar), jnp.cos(ac), jnp.cos(ac), zpad], axis=1)
    sin_t = jnp.concatenate([0.0 * ones, -jnp.sin(ar), jnp.sin(ar), -jnp.sin(ac), jnp.sin(ac), zpad], axis=1)
    return cos_t, sin_t


def _identity_tables(l):
    lane = jnp.arange(HEAD_PAD)
    cos_t = jnp.broadcast_to((lane < ROPE_LANE0 + QK_ROPE).astype(F32), (l, HEAD_PAD))
    return cos_t, jnp.zeros((l, HEAD_PAD), F32)


def _block_diag(w):
    g, a, b = w.shape
    out = jnp.zeros((g * a, g * b), w.dtype)
    for i in range(g):
        out = out.at[i * a:(i + 1) * a, i * b:(i + 1) * b].set(w[i])
    return out


def _prep_layer(w_in, w_pool, w_fno, w_uq, w_ukv):
    d = w_in.shape[0]
    kr = jnp.zeros((d, HEAD_PAD), w_in.dtype).at[:, ROPE_LANE0:ROPE_LANE0 + QK_ROPE].set(w_in[:, COL_KR:])
    w_in_p = jnp.concatenate([w_in[:, :COL_KR], kr], axis=1).astype(BF16)
    qk = QK_NOPE + QK_ROPE
    wq = w_uq.reshape(Q_LORA, N_HEADS, qk)
    wq = jnp.pad(wq, ((0, 0), (0, 0), (0, HEAD_PAD - qk))).reshape(Q_LORA, N_HEADS * HEAD_PAD).astype(BF16)
    wkv = w_ukv.reshape(KV_LORA, N_HEADS, QK_NOPE + V_DIM)
    wk = jnp.pad(wkv[..., :QK_NOPE], ((0, 0), (0, 0), (0, HEAD_PAD - QK_NOPE)))
    wv = jnp.pad(wkv[..., QK_NOPE:], ((0, 0), (0, 0), (0, HEAD_PAD - V_DIM)))
    wkv_p = jnp.concatenate([wk.reshape(KV_LORA, -1), wv.reshape(KV_LORA, -1)], axis=1).astype(BF16)
    return w_in_p, wq, wkv_p, _block_diag(w_pool).astype(BF16), _block_diag(w_fno).astype(BF16)


def _run_tables(n_pieces):
    n_all = n_pieces.shape[0]
    npc = n_pieces.astype(jnp.int32)
    piece_off = jnp.cumsum(npc, axis=1) - npc
    run_src = (jnp.arange(n_all, dtype=jnp.int32)[:, None] * PIECES_PER_TILE + piece_off).T.reshape(-1)
    run_len = npc.T.reshape(-1)
    blocks = (jnp.sum(npc, axis=0) + BLOCK_PIECES - 1) // BLOCK_PIECES
    block_end = jnp.cumsum(blocks)
    max_blocks = n_all * PIECES_PER_TILE // BLOCK_PIECES + N_EXPERTS
    blk = jnp.arange(max_blocks, dtype=jnp.int32)
    block_expert = jnp.minimum(jnp.sum((block_end[None, :] <= blk[:, None]).astype(jnp.int32), axis=1),
                               N_EXPERTS - 1)
    return block_expert, block_end[-1:], run_len, run_src


TM_PROJ = 256
TM_MIX = 256
ATTN_TQ = 2048
ATTN_TK = 512


def kernel(x, c, ctx, c_ctx, w_mod, b_mod, g_mix, g_ffn, w_in, w_pool, pool_scale, w_fno, g_q, w_uq,
           g_kv, w_ukv, w_out, w_router, router_bias, w_gate, w_up, w_down, w_sh_gate, w_sh_up,
           w_sh_down, g_final):
    b, l, d = x.shape
    lc = ctx.shape[1]
    depth = w_mod.shape[0]
    cc = jnp.zeros((SUBLANES, d), F32).at[:b].set(c).at[b].set(c_ctx)
    mods = _modulation(cc, w_mod, b_mod)
    cos_t, sin_t = _rope_tables(l)
    cos_i, sin_i = _identity_tables(lc)
    xc = ctx
    row = lambda v: v.reshape(1, -1)
    for li in range(depth):
        last = li == depth - 1
        m = mods[li].reshape(SUBLANES, N_MOD, d)
        lat = [m[:b, j][:, None, :] for j in range(N_MOD)]
        cm = [jnp.broadcast_to(m[b, j][None, None, :], (b, 1, d)) for j in range(N_MOD)]
        w_in_p, wq_p, wkv_p, wp_bd, wf_bd = _prep_layer(w_in[li], w_pool[li], w_fno[li], w_uq[li], w_ukv[li])
        w_out_b = w_out[li].astype(BF16)
        gq, gkv, gm = row(g_q[li]), row(g_kv[li]), row(g_mix[li])

        up, uf, q, k, v = _in_projection(x, lat[0], lat[1], gm, w_in_p, gq, wq_p, gkv, wkv_p,
                                         cos_t, sin_t, TM_PROJ)
        upc, ufc, qc, kc, vc = _in_projection(xc, cm[0], cm[1], gm, w_in_p, gq, wq_p, gkv, wkv_p,
                                              cos_i, sin_i, TM_PROJ)
        attn = _attention(q, kc, vc, k, v, tq=ATTN_TQ, tk=ATTN_TK)
        f = _fourier(uf)
        ps = row(pool_scale[li])
        x = _mixer_output(x, lat[2], up, f, attn, wp_bd, ps, wf_bd, w_out_b, TM_MIX)
        if not last:
            attn_c = _attention(qc, kc, vc, tq=ATTN_TQ, tk=ATTN_TK)
            fc = _fourier(ufc)
            xc = _mixer_output(xc, cm[2], upc, fc, attn_c, wp_bd, ps, wf_bd, w_out_b, TM_MIX)

        w_rt = w_router[li].T
        rb = router_bias[li].reshape(N_EXPERTS, 1)
        wsg, wsu, wsd = w_sh_gate[li].astype(BF16), w_sh_up[li].astype(BF16), w_sh_down[li].astype(BF16)
        with_ctx = lambda j: jnp.concatenate([lat[j], cm[j][:1]], axis=0)
        x_tiles, xs, slot_t, w_t, n_pieces = _ffn_front(
            x, None if last else xc, with_ctx(3), with_ctx(4), with_ctx(5), row(g_ffn[li]), w_rt, rb, wsg, wsu, wsd)
        n_all = xs.shape[0]
        block_expert, n_used, run_len, run_src = _run_tables(n_pieces[..., 0])
        w_gu = jnp.concatenate([w_gate[li], w_up[li]], axis=-1).astype(BF16)
        y = _experts(block_expert, n_used, run_len, run_src, xs.reshape(n_all * PIECES_PER_TILE, PIECE, d),
                     w_gu, w_down[li].astype(BF16), n_all)
        y = y.reshape(n_all, TILE_ROWS, d)
        slot_tok, w_tok = slot_t.T, w_t.T
        x = _combine(y, slot_tok, w_tok, x_tiles, lat[5], row(g_final), (b, l), 0, last)
        if not last:
            xc = _combine(y, slot_tok, w_tok, x_tiles, cm[5], row(g_final), (b, lc), b * l // MOE_TILE, False)
    return x
```

```python
import functools
import math

import jax
import jax.numpy as jnp
import numpy as np
from jax import lax
from jax.experimental import pallas as pl
from jax.experimental.pallas import tpu as pltpu

F32 = jnp.float32
BF16 = jnp.bfloat16
HIGHEST = lax.Precision.HIGHEST

EPS = 1e-6
N_MOD = 6
GRID_W = 64
POOL_WINDOWS = (2, 4, 8, 16)
GROUP_DIM = 64
N_GROUPS = 4
MIX_GROUP_WIDTH = N_GROUPS * GROUP_DIM
N_HEADS = 8
QK_NOPE = 64
QK_ROPE = 32
V_DIM = 64
Q_LORA = 384
KV_LORA = 256
ROPE_FREQS = QK_ROPE // 4
ROPE_THETA = 10000.0
SOFTMAX_SCALE = (QK_NOPE + QK_ROPE) ** -0.5
N_EXPERTS = 64
TOP_K = 8
N_EXPERT_GROUPS = 8
TOPK_GROUPS = 4
EXPERTS_PER_GROUP = N_EXPERTS // N_EXPERT_GROUPS
ROUTED_SCALE = 2.5

LANES = 128
SUBLANES = 8
HEAD_PAD = LANES
VMEM_LIMIT = 56 * 1024 * 1024

COL_FNO = MIX_GROUP_WIDTH
COL_Q = COL_FNO + MIX_GROUP_WIDTH
COL_KV = COL_Q + Q_LORA
COL_KR = COL_KV + KV_LORA
IN_PAD = COL_KR + HEAD_PAD
ROPE_LANE0 = QK_NOPE

MOE_TILE = 256
PIECE = 16
PIECES_PER_TILE = MOE_TILE * TOP_K // PIECE + N_EXPERTS
TILE_ROWS = PIECES_PER_TILE * PIECE
BLOCK_PIECES = 16
EXPERT_BLOCK = BLOCK_PIECES * PIECE


def _cparams(sem, vmem=VMEM_LIMIT):
    return pltpu.CompilerParams(dimension_semantics=sem, vmem_limit_bytes=vmem)


def _rms(x, g):
    return x * lax.rsqrt(jnp.mean(x * x, axis=-1, keepdims=True) + EPS) * g


def _silu(x):
    return x * jax.nn.sigmoid(x)


def _mod_kernel(c_ref, w_ref, b_ref, o_ref):
    s = _silu(c_ref[...])
    o_ref[0] = jnp.dot(s, w_ref[0], preferred_element_type=F32, precision=HIGHEST) + b_ref[0]


def _modulation(cc, w_mod, b_mod):
    depth, d, nd = w_mod.shape
    n_chunks = nd // d
    return pl.pallas_call(
        _mod_kernel,
        grid=(depth, n_chunks),
        in_specs=[pl.BlockSpec((SUBLANES, d), lambda l, j: (0, 0)),
                  pl.BlockSpec((1, d, d), lambda l, j: (l, 0, j)),
                  pl.BlockSpec((1, 1, d), lambda l, j: (l, 0, j))],
        out_specs=pl.BlockSpec((1, SUBLANES, d), lambda l, j: (l, 0, j)),
        out_shape=jax.ShapeDtypeStruct((depth, SUBLANES, nd), F32),
        compiler_params=_cparams(("parallel", "parallel")),
        name="modulation",
    )(cc, w_mod, b_mod.reshape(depth, 1, nd))


def _rope(t, c, s):
    lane = lax.broadcasted_iota(jnp.int32, t.shape, 1)
    first_half = (lane % (2 * ROPE_FREQS)) < ROPE_FREQS
    partner = jnp.where(first_half, pltpu.roll(t, LANES - ROPE_FREQS, 1), pltpu.roll(t, ROPE_FREQS, 1))
    return t * c + partner * s


def _inproj_kernel(x_ref, sh_ref, sc_ref, g_ref, win_ref, gq_ref, wuq_ref, gkv_ref, wukv_ref,
                   cos_ref, sin_ref, up_ref, uf_ref, q_ref, k_ref, v_ref):
    x = x_ref[0]
    h = _rms(x, g_ref[...]) * (1.0 + sc_ref[0]) + sh_ref[0]
    u = jnp.dot(h.astype(BF16), win_ref[...], preferred_element_type=F32)
    up_ref[0] = u[:, :COL_FNO]
    uf_ref[0] = u[:, COL_FNO:COL_Q]
    cq = _rms(u[:, COL_Q:COL_KV], gq_ref[...])
    q = jnp.dot(cq.astype(BF16), wuq_ref[...], preferred_element_type=F32)
    ckv = _rms(u[:, COL_KV:COL_KR], gkv_ref[...])
    kv = jnp.dot(ckv.astype(BF16), wukv_ref[...], preferred_element_type=F32)
    cos = cos_ref[...]
    sin = sin_ref[...]
    kr = _rope(u[:, COL_KR:IN_PAD], cos, sin)
    lane = lax.broadcasted_iota(jnp.int32, kr.shape, 1)
    ones_col = (lane == V_DIM).astype(F32)
    q_scale = SOFTMAX_SCALE * math.log2(math.e)
    for hd in range(N_HEADS):
        lo = hd * HEAD_PAD
        q_ref[0, hd] = (_rope(q[:, lo:lo + HEAD_PAD], cos, sin) * q_scale).astype(BF16)
        k_ref[0, hd] = (kv[:, lo:lo + HEAD_PAD] + kr).astype(BF16)
        vo = N_HEADS * HEAD_PAD + lo
        v_ref[0, hd] = (kv[:, vo:vo + HEAD_PAD] + ones_col).astype(BF16)


def _in_projection(x, sh, sc, g, w_in_p, g_q, w_uq_p, g_kv, w_ukv_p, cos_t, sin_t, tm):
    b, l, d = x.shape
    tm = min(tm, l)
    full = lambda a: pl.BlockSpec(a.shape, lambda bi, i: (0,) * a.ndim)
    vec = pl.BlockSpec((1, 1, d), lambda bi, i: (bi, 0, 0))
    head_out = pl.BlockSpec((1, N_HEADS, tm, HEAD_PAD), lambda bi, i: (bi, 0, i, 0))
    head_shape = jax.ShapeDtypeStruct((b, N_HEADS, l, HEAD_PAD), BF16)
    return pl.pallas_call(
        _inproj_kernel,
        grid=(b, l // tm),
        in_specs=[pl.BlockSpec((1, tm, d), lambda bi, i: (bi, i, 0)), vec, vec,
                  full(g), full(w_in_p), full(g_q), full(w_uq_p), full(g_kv), full(w_ukv_p),
                  pl.BlockSpec((tm, HEAD_PAD), lambda bi, i: (i, 0)),
                  pl.BlockSpec((tm, HEAD_PAD), lambda bi, i: (i, 0))],
        out_specs=[pl.BlockSpec((1, tm, MIX_GROUP_WIDTH), lambda bi, i: (bi, i, 0)),
                   pl.BlockSpec((1, tm, MIX_GROUP_WIDTH), lambda bi, i: (bi, i, 0)),
                   head_out, head_out, head_out],
        out_shape=[jax.ShapeDtypeStruct((b, l, MIX_GROUP_WIDTH), F32),
                   jax.ShapeDtypeStruct((b, l, MIX_GROUP_WIDTH), F32),
                   head_shape, head_shape, head_shape],
        compiler_params=_cparams(("parallel", "parallel")),
        name="in_projection",
    )(x, sh, sc, g, w_in_p, g_q, w_uq_p, g_kv, w_ukv_p, cos_t, sin_t)


ATTN_HEADS_PER_STEP = 2


def _attn_kernel(*refs, tk, n_chunks):
    if n_chunks:
        q_ref, kc_ref, vc_ref, k_ref, v_ref, o_ref = refs
    else:
        q_ref, kc_ref, vc_ref, o_ref = refs
    tq = q_ref.shape[2]
    nt = (((1,), (1,)), ((), ()))
    heads = range(ATTN_HEADS_PER_STEP)
    qs = [q_ref[0, hh] for hh in heads]

    def step(q, kc, vc, m, acc):
        s = lax.dot_general(q, kc, nt, preferred_element_type=F32)
        m_new = jnp.maximum(m, jnp.max(s, axis=-1, keepdims=True))
        alpha = jnp.exp2(m - m_new)
        p = jnp.exp2(s - m_new).astype(BF16)
        return m_new, alpha * acc + jnp.dot(p, vc, preferred_element_type=F32)

    carry = tuple(step(qs[hh], kc_ref[0, hh], vc_ref[0, hh],
                       jnp.full((tq, 1), -1e30, F32), jnp.zeros((tq, HEAD_PAD), F32)) for hh in heads)
    if n_chunks:
        def body(j, carry):
            off = pl.multiple_of(j * tk, tk)
            return tuple(step(qs[hh], k_ref[0, hh, pl.ds(off, tk), :], v_ref[0, hh, pl.ds(off, tk), :],
                              *carry[hh]) for hh in heads)

        carry = lax.fori_loop(0, n_chunks, body, carry)
    for hh in heads:
        acc = carry[hh][1]
        out = acc[:, :V_DIM] / acc[:, V_DIM:V_DIM + 1]
        o_ref[0, :, hh * V_DIM:(hh + 1) * V_DIM] = out.astype(o_ref.dtype)


def _attention(q, k_ctx, v_ctx, k=None, v=None, *, tq, tk):
    b, h, l, _ = q.shape
    lc = k_ctx.shape[2]
    tq = min(tq, l)
    hps = ATTN_HEADS_PER_STEP
    qspec = pl.BlockSpec((1, hps, tq, HEAD_PAD), lambda bi, hi, i: (bi, hi, i, 0))
    cspec = pl.BlockSpec((1, hps, lc, HEAD_PAD), lambda bi, hi, i: (bi, hi, 0, 0))
    args, specs, n_chunks = [q, k_ctx, v_ctx], [qspec, cspec, cspec], 0
    if k is not None:
        lk = k.shape[2]
        tk = min(tk, lk)
        n_chunks = lk // tk
        kspec = pl.BlockSpec((1, hps, lk, HEAD_PAD), lambda bi, hi, i: (bi, hi, 0, 0),
                             pipeline_mode=pl.Buffered(1))
        args += [k, v]
        specs += [kspec, kspec]
    return pl.pallas_call(
        functools.partial(_attn_kernel, tk=tk, n_chunks=n_chunks),
        grid=(b, h // hps, l // tq),
        in_specs=specs,
        out_specs=pl.BlockSpec((1, tq, hps * V_DIM), lambda bi, hi, i: (bi, i, hi)),
        out_shape=jax.ShapeDtypeStruct((b, l, h * V_DIM), BF16),
        compiler_params=_cparams(("parallel", "parallel", "arbitrary")),
        name="attention",
    )(*args)


def _dft_step1_kernel(x_ref, c_ref, s_ref, yr_ref, yi_ref):
    xb = x_ref[0].astype(BF16)
    yr_ref[0] = jnp.dot(c_ref[...], xb, preferred_element_type=F32)
    yi_ref[0] = jnp.dot(s_ref[...], xb, preferred_element_type=F32)


def _dft_step2_kernel(yr_ref, yi_ref, tc_ref, ts_ref, c_ref, s_ref, cc_ref, sc_ref, o_ref):
    for j in range(SUBLANES):
        yr, yi = yr_ref[0, j], yi_ref[0, j]
        tc, ts = tc_ref[j], ts_ref[j]
        zr = (yr * tc - yi * ts).astype(BF16)
        zi = (yi * tc + yr * ts).astype(BF16)
        a = (jnp.dot(c_ref[...], zr, preferred_element_type=F32)
             - jnp.dot(s_ref[...], zi, preferred_element_type=F32))
        bm = (jnp.dot(s_ref[...], zr, preferred_element_type=F32)
              + jnp.dot(c_ref[...], zi, preferred_element_type=F32))
        o_ref[0, :, j, :] = (jnp.dot(a.astype(BF16), cc_ref[...], preferred_element_type=F32)
                             - jnp.dot(bm.astype(BF16), sc_ref[...], preferred_element_type=F32))


def _dft_dense_kernel(x_ref, c_ref, s_ref, cc_ref, sc_ref, o_ref):
    xb = x_ref[0].astype(BF16)
    a = jnp.dot(c_ref[...], xb, preferred_element_type=F32)
    bm = jnp.dot(s_ref[...], xb, preferred_element_type=F32)
    o_ref[0] = (jnp.dot(a.astype(BF16), cc_ref[...], preferred_element_type=F32)
                - jnp.dot(bm.astype(BF16), sc_ref[...], preferred_element_type=F32))


def _dft_mats(n):
    ang = 2.0 * np.pi * np.outer(np.arange(n), np.arange(n)) / n
    return np.cos(ang), np.sin(ang)


def _channel_dft(l):
    c, s = _dft_mats(GROUP_DIM)
    eye = np.eye(N_GROUPS)
    norm = 1.0 / math.sqrt(l * GROUP_DIM)
    return (jnp.asarray(np.kron(eye, c) * norm, BF16), jnp.asarray(np.kron(eye, s) * norm, BF16))


def _dft_factors(l):
    n1 = 1 << (int(math.log2(l)) // 2)
    return n1, l // n1


def _fourier(u_fno, lane_block=4096):
    b, l, c = u_fno.shape
    cc, sc = _channel_dft(l)
    full = lambda a: pl.BlockSpec(a.shape, lambda *_: (0,) * a.ndim)
    if l <= 512:
        cm, sm = _dft_mats(l)
        cm, sm = jnp.asarray(cm, BF16), jnp.asarray(sm, BF16)
        blk = pl.BlockSpec((1, l, c), lambda bi: (bi, 0, 0))
        return pl.pallas_call(
            _dft_dense_kernel, grid=(b,),
            in_specs=[blk, full(cm), full(sm), full(cc), full(sc)],
            out_specs=blk, out_shape=jax.ShapeDtypeStruct((b, l, c), F32),
            compiler_params=_cparams(("parallel",)), name="dft_dense",
        )(u_fno, cm, sm, cc, sc)
    n1, n2 = _dft_factors(l)
    c1, s1 = _dft_mats(n1)
    c2, s2 = _dft_mats(n2)
    c1, s1, c2, s2 = (jnp.asarray(m, BF16) for m in (c1, s1, c2, s2))
    ang = 2.0 * np.pi * np.outer(np.arange(n1), np.arange(n2)) / l
    tc = jnp.asarray(np.cos(ang)[:, :, None], F32)
    ts = jnp.asarray(np.sin(ang)[:, :, None], F32)
    w = n2 * c
    tn = min(lane_block, w)
    x2 = u_fno.reshape(b, n1, w)
    yspec = pl.BlockSpec((1, n1, tn), lambda bi, i: (bi, 0, i))
    yr, yi = pl.pallas_call(
        _dft_step1_kernel, grid=(b, w // tn),
        in_specs=[yspec, full(c1), full(s1)],
        out_specs=[yspec, yspec],
        out_shape=[jax.ShapeDtypeStruct((b, n1, w), F32)] * 2,
        compiler_params=_cparams(("parallel", "parallel")), name="dft_step1",
    )(x2, c1, s1)
    yr = yr.reshape(b, n1, n2, c)
    yi = yi.reshape(b, n1, n2, c)
    slab = pl.BlockSpec((1, SUBLANES, n2, c), lambda bi, i: (bi, i, 0, 0))
    tw = pl.BlockSpec((SUBLANES, n2, 1), lambda bi, i: (i, 0, 0))
    out = pl.pallas_call(
        _dft_step2_kernel, grid=(b, n1 // SUBLANES),
        in_specs=[slab, slab, tw, tw, full(c2), full(s2), full(cc), full(sc)],
        out_specs=pl.BlockSpec((1, n2, SUBLANES, c), lambda bi, i: (bi, 0, i, 0)),
        out_shape=jax.ShapeDtypeStruct((b, n2, n1, c), F32),
        compiler_params=_cparams(("parallel", "parallel")), name="dft_step2",
    )(yr, yi, tc, ts, c2, s2, cc, sc)
    return out.reshape(b, l, c)


POOL_HALO = SUBLANES


def _mixout_kernel(x_ref, gt_ref, up_ref, prev_ref, next_ref, f_ref, at_ref,
                   wp_ref, ps_ref, wf_ref, wo_ref, o_ref, *, seq_len):
    i = pl.program_id(1)
    tm = x_ref.shape[1]
    u = up_ref[0]
    prev = jnp.where(i > 0, prev_ref[0], 0.0)
    nxt = jnp.where(i < pl.num_programs(1) - 1, next_ref[0], 0.0)
    p = jnp.concatenate([prev, u, nxt], axis=0)
    n = tm + 2 * POOL_HALO
    s1 = pltpu.roll(p, 1, 0) + p
    s2 = pltpu.roll(s1, 1, 0) + pltpu.roll(s1, n - 1, 0)
    s4 = pltpu.roll(s2, 2, 0) + pltpu.roll(s2, n - 2, 0)
    s8 = pltpu.roll(s4, 4, 0) + pltpu.roll(s4, n - 4, 0)
    lane = lax.broadcasted_iota(jnp.int32, (tm, MIX_GROUP_WIDTH), 1)
    grp = lane // GROUP_DIM
    lo, hi = POOL_HALO, POOL_HALO + tm
    win = jnp.where(grp == 0, s1[lo:hi],
                    jnp.where(grp == 1, s2[lo:hi], jnp.where(grp == 2, s4[lo:hi], s8[lo:hi])))
    half = jnp.left_shift(1, grp)
    t = i * tm + lax.broadcasted_iota(jnp.int32, (tm, MIX_GROUP_WIDTH), 0)
    cnt = (jnp.minimum(t + half, seq_len) - jnp.maximum(t - half, 0)).astype(F32)
    dlt = win / cnt - u
    pool_y = jnp.dot(dlt.astype(BF16), wp_ref[...], preferred_element_type=F32) * ps_ref[...]
    fno_y = jnp.dot(f_ref[0].astype(BF16), wf_ref[...], preferred_element_type=F32)
    w = MIX_GROUP_WIDTH
    y = (jnp.dot(pool_y.astype(BF16), wo_ref[0:w, :], preferred_element_type=F32)
         + jnp.dot(fno_y.astype(BF16), wo_ref[w:2 * w, :], preferred_element_type=F32)
         + jnp.dot(at_ref[0], wo_ref[2 * w:, :], preferred_element_type=F32))
    o_ref[0] = x_ref[0] + gt_ref[0] * y


def _mixer_output(x, gt, u_pool, f, attn, wp_bd, pool_scale, wf_bd, w_out, tm):
    b, l, d = x.shape
    tm = min(tm, l)
    nb = tm // POOL_HALO
    last = l // POOL_HALO - 1
    full = lambda a: pl.BlockSpec(a.shape, lambda bi, i: (0,) * a.ndim)
    w = MIX_GROUP_WIDTH
    return pl.pallas_call(
        functools.partial(_mixout_kernel, seq_len=l),
        grid=(b, l // tm),
        in_specs=[pl.BlockSpec((1, tm, d), lambda bi, i: (bi, i, 0)),
                  pl.BlockSpec((1, 1, d), lambda bi, i: (bi, 0, 0)),
                  pl.BlockSpec((1, tm, w), lambda bi, i: (bi, i, 0)),
                  pl.BlockSpec((1, POOL_HALO, w), lambda bi, i: (bi, jnp.maximum(i * nb - 1, 0), 0)),
                  pl.BlockSpec((1, POOL_HALO, w), lambda bi, i: (bi, jnp.minimum((i + 1) * nb, last), 0)),
                  pl.BlockSpec((1, tm, w), lambda bi, i: (bi, i, 0)),
                  pl.BlockSpec((1, tm, attn.shape[-1]), lambda bi, i: (bi, i, 0)),
                  full(wp_bd), full(pool_scale), full(wf_bd), full(w_out)],
        out_specs=pl.BlockSpec((1, tm, d), lambda bi, i: (bi, i, 0)),
        out_shape=jax.ShapeDtypeStruct((b, l, d), F32),
        compiler_params=_cparams(("parallel", "arbitrary")),
        name="mixer_output",
    )(x, gt, u_pool, u_pool, u_pool, f, attn, wp_bd, pool_scale, wf_bd, w_out)


def _first_index_of_max(vals, iota, n):
    mx = jnp.max(vals, axis=0, keepdims=True)
    ix = jnp.min(jnp.where(vals == mx, iota, n), axis=0, keepdims=True)
    return mx, ix


def _ffn_front_kernel(*refs, n_lat, has_ctx):
    if has_ctx:
        x_ref, xc_ref = refs[:2]
        refs = refs[2:]
    else:
        x_ref, refs = refs[0], refs[1:]
    (sh_ref, sc_ref, gt_ref, g_ref, wr_ref, rb_ref, wsg_ref, wsu_ref, wsd_ref,
     xo_ref, xs_ref, slot_ref, wt_ref, cnt_ref) = refs
    x = x_ref[0]
    if has_ctx:
        x = jnp.where(pl.program_id(0) < n_lat, x, xc_ref[0])
    tm = x.shape[0]
    h = _rms(x, g_ref[...]) * (1.0 + sc_ref[0]) + sh_ref[0]

    logits = lax.dot_general(wr_ref[...], h, (((1,), (1,)), ((), ())),
                             preferred_element_type=F32, precision=HIGHEST)
    aff = jax.nn.sigmoid(logits)
    sel = aff + rb_ref[...]
    e_iota = lax.broadcasted_iota(jnp.int32, (N_EXPERTS, tm), 0).astype(F32)
    neg = jnp.float32(-jnp.inf)
    gscores = []
    for g in range(N_EXPERT_GROUPS):
        blk = sel[g * EXPERTS_PER_GROUP:(g + 1) * EXPERTS_PER_GROUP]
        it = lax.broadcasted_iota(jnp.int32, blk.shape, 0).astype(F32)
        m1, i1 = _first_index_of_max(blk, it, float(EXPERTS_PER_GROUP))
        m2 = jnp.max(jnp.where(it == i1, neg, blk), axis=0, keepdims=True)
        gscores.append(m1 + m2)
    gs = jnp.concatenate(gscores, axis=0)
    g_iota = lax.broadcasted_iota(jnp.int32, gs.shape, 0).astype(F32)
    gself = jnp.zeros(gs.shape, F32)
    for _ in range(TOPK_GROUPS):
        _, ig = _first_index_of_max(gs, g_iota, float(N_EXPERT_GROUPS))
        hit = g_iota == ig
        gself = jnp.where(hit, 1.0, gself)
        gs = jnp.where(hit, neg, gs)
    emask = jnp.concatenate(
        [jnp.broadcast_to(gself[g:g + 1], (EXPERTS_PER_GROUP, tm)) for g in range(N_EXPERT_GROUPS)], axis=0)
    masked = jnp.where(emask > 0.5, sel, neg)
    hits, wts = [], []
    chosen = jnp.zeros((N_EXPERTS, tm), F32)
    for _ in range(TOP_K):
        _, ie = _first_index_of_max(masked, e_iota, float(N_EXPERTS))
        hit = e_iota == ie
        hits.append(hit)
        wts.append(jnp.sum(jnp.where(hit, aff, 0.0), axis=0, keepdims=True))
        chosen = jnp.where(hit, 1.0, chosen)
        masked = jnp.where(hit, neg, masked)
    wsum = wts[0]
    for w in wts[1:]:
        wsum = wsum + w
    scale = ROUTED_SCALE / wsum
    wt_ref[...] = jnp.concatenate([w * scale for w in wts], axis=0)

    r_io = lax.broadcasted_iota(jnp.int32, (tm, tm), 0)
    c_io = lax.broadcasted_iota(jnp.int32, (tm, tm), 1)
    tri = (r_io <= c_io).astype(BF16)
    incl = jnp.dot(chosen.astype(BF16), tri, preferred_element_type=F32)
    count = jnp.sum(chosen, axis=1, keepdims=True)
    n_pieces = jnp.floor((count + (PIECE - 1)) * (1.0 / PIECE))
    cnt_ref[0] = n_pieces
    e_r = lax.broadcasted_iota(jnp.int32, (N_EXPERTS, N_EXPERTS), 0)
    e_c = lax.broadcasted_iota(jnp.int32, (N_EXPERTS, N_EXPERTS), 1)
    before = (e_c < e_r).astype(BF16)
    piece_off = jnp.dot(before, jnp.broadcast_to(n_pieces, (N_EXPERTS, LANES)).astype(BF16),
                        preferred_element_type=F32)[:, :1]
    pos = piece_off * PIECE + incl - chosen
    slots = [jnp.sum(jnp.where(hit, pos, 0.0), axis=0, keepdims=True) for hit in hits]
    slot_ref[...] = jnp.concatenate(slots, axis=0).astype(jnp.int32)

    row_io = lax.broadcasted_iota(jnp.int32, (TILE_ROWS, tm), 0).astype(F32)
    onehot = jnp.zeros((TILE_ROWS, tm), F32)
    for s in slots:
        onehot = jnp.where(row_io == s, 1.0, onehot)
    onehot = onehot.astype(BF16)
    hb = h.astype(BF16)
    d = hb.shape[1]
    nc = 2 * LANES
    for j in range(d // nc):
        xs_ref[0, :, j * nc:(j + 1) * nc] = jnp.dot(
            onehot, hb[:, j * nc:(j + 1) * nc], preferred_element_type=F32).astype(BF16)

    a = jnp.dot(hb, wsg_ref[...], preferred_element_type=F32)
    bb = jnp.dot(hb, wsu_ref[...], preferred_element_type=F32)
    sh_out = jnp.dot((_silu(a) * bb).astype(BF16), wsd_ref[...], preferred_element_type=F32)
    xo_ref[0] = x + gt_ref[0] * sh_out


def _ffn_front(x, xc, sh, sc, gt, g, w_rt, rb, wsg, wsu, wsd):
    b, l, d = x.shape
    tm = MOE_TILE
    nt = l // tm
    n_lat = b * nt
    has_ctx = xc is not None
    ntc = xc.shape[1] // tm if has_ctx else 0
    n_all = n_lat + b * ntc
    full = lambda a: pl.BlockSpec(a.shape, lambda i: (0,) * a.ndim)

    def lat_map(i):
        j = jnp.minimum(i, n_lat - 1)
        return (j // nt, j % nt, 0)

    def ctx_map(i):
        j = jnp.clip(i - n_lat, 0, b * ntc - 1)
        return (j // ntc, j % ntc, 0)

    vec = pl.BlockSpec((1, 1, d), lambda i: (jnp.where(i < n_lat, i // nt, b), 0, 0))
    xspecs = [pl.BlockSpec((1, tm, d), lat_map)] + ([pl.BlockSpec((1, tm, d), ctx_map)] if has_ctx else [])
    xargs = [x] + ([xc] if has_ctx else [])
    tokmap = lambda i: (0, i)
    return pl.pallas_call(
        functools.partial(_ffn_front_kernel, n_lat=n_lat, has_ctx=has_ctx),
        grid=(n_all,),
        in_specs=xspecs + [vec, vec, vec, full(g), full(w_rt), full(rb), full(wsg), full(wsu), full(wsd)],
        out_specs=[pl.BlockSpec((1, tm, d), lambda i: (i, 0, 0)),
                   pl.BlockSpec((1, TILE_ROWS, d), lambda i: (i, 0, 0)),
                   pl.BlockSpec((TOP_K, tm), tokmap), pl.BlockSpec((TOP_K, tm), tokmap),
                   pl.BlockSpec((1, N_EXPERTS, 1), lambda i: (i, 0, 0))],
        out_shape=[jax.ShapeDtypeStruct((n_all, tm, d), F32),
                   jax.ShapeDtypeStruct((n_all, TILE_ROWS, d), BF16),
                   jax.ShapeDtypeStruct((TOP_K, n_all * tm), jnp.int32),
                   jax.ShapeDtypeStruct((TOP_K, n_all * tm), F32),
                   jax.ShapeDtypeStruct((n_all, N_EXPERTS, 1), F32)],
        compiler_params=_cparams(("parallel",)),
        name="ffn_front",
    )(*xargs, sh, sc, gt, g, w_rt, rb, wsg, wsu, wsd)


def _expert_kernel(bexp_ref, nused_ref, rlen_ref, rsrc_ref, rnext_ref, xs_ref, wgu_ref, wd_ref, y_ref,
                   state, ids, oids, xbuf, ybuf, sem_in, sem_out, *, runs_per_expert):
    i = pl.program_id(0)
    n_used = nused_ref[0]
    end_run = rlen_ref.shape[0] - 1

    def copy_in(pid, slot, p):
        return pltpu.make_async_copy(xs_ref.at[pid], xbuf.at[slot, pl.ds(p * PIECE, PIECE), :], sem_in.at[slot])

    def copy_out(pid, slot, p):
        return pltpu.make_async_copy(ybuf.at[slot, pl.ds(p * PIECE, PIECE), :], y_ref.at[pid], sem_out.at[slot])

    def issue_gather(blk, slot):
        r_lo = bexp_ref[blk] * runs_per_expert
        r_end = r_lo + runs_per_expert
        fresh = state[0] < r_lo
        r = jnp.where(fresh, r_lo, state[0])
        o = jnp.where(fresh, 0, state[1])
        read_pid = jnp.int32(0)
        for p in range(BLOCK_PIECES):
            exhausted = o >= rlen_ref[jnp.minimum(r, end_run)]
            r = jnp.where(exhausted, rnext_ref[jnp.minimum(r + 1, end_run)], r)
            o = jnp.where(exhausted, 0, o)
            valid = r < r_end
            pid = rsrc_ref[jnp.minimum(r, end_run)] + o
            read_pid = jnp.where(valid, pid, read_pid)
            ids[slot, p] = jnp.where(valid, pid, -1)
            copy_in(read_pid, slot, p).start()
            o = o + valid.astype(jnp.int32)
        state[0] = r
        state[1] = o

    def wait_gather(slot):
        pltpu.make_async_copy(xbuf.at[slot], xbuf.at[slot], sem_in.at[slot]).wait()

    def wait_pieces(table, slot, make_copy):
        for p in range(BLOCK_PIECES):
            pid = table[slot, p]

            @pl.when(pid >= 0)
            def _(pid=pid, p=p):
                make_copy(pid, slot, p).wait()

    @pl.when(i < n_used)
    def _():
        slot = i % 2

        @pl.when(i == 0)
        def _():
            state[0] = 0
            state[1] = 0
            issue_gather(0, 0)

        @pl.when(i + 1 < n_used)
        def _():
            issue_gather(i + 1, (i + 1) % 2)

        wait_gather(slot)

        @pl.when(i >= 2)
        def _():
            wait_pieces(oids, slot, copy_out)

        gu = jnp.dot(xbuf[slot], wgu_ref[0], preferred_element_type=F32)
        f = gu.shape[1] // 2
        act = (_silu(gu[:, :f]) * gu[:, f:]).astype(BF16)
        ybuf[slot] = jnp.dot(act, wd_ref[0], preferred_element_type=F32).astype(BF16)
        for p in range(BLOCK_PIECES):
            pid = ids[slot, p]
            oids[slot, p] = pid

            @pl.when(pid >= 0)
            def _(pid=pid, p=p):
                copy_out(pid, slot, p).start()

        @pl.when(i == n_used - 1)
        def _():
            wait_pieces(oids, slot, copy_out)

            @pl.when(i >= 1)
            def _():
                wait_pieces(oids, 1 - slot, copy_out)


def _experts(block_expert, n_used, run_len, run_src, run_next, xs_pieces, w_gu, w_down, runs_per_expert):
    n_blocks = block_expert.shape[0]
    _, _, d = xs_pieces.shape
    _, _, f2 = w_gu.shape
    grid_spec = pltpu.PrefetchScalarGridSpec(
        num_scalar_prefetch=5,
        grid=(n_blocks,),
        in_specs=[pl.BlockSpec(memory_space=pl.ANY),
                  pl.BlockSpec((1, d, f2), lambda i, be, *_: (be[i], 0, 0)),
                  pl.BlockSpec((1, f2 // 2, d), lambda i, be, *_: (be[i], 0, 0))],
        out_specs=pl.BlockSpec(memory_space=pl.ANY),
        scratch_shapes=[pltpu.SMEM((2,), jnp.int32),
                        pltpu.SMEM((2, BLOCK_PIECES), jnp.int32),
                        pltpu.SMEM((2, BLOCK_PIECES), jnp.int32),
                        pltpu.VMEM((2, EXPERT_BLOCK, d), BF16),
                        pltpu.VMEM((2, EXPERT_BLOCK, d), BF16),
                        pltpu.SemaphoreType.DMA((2,)), pltpu.SemaphoreType.DMA((2,))],
    )
    return pl.pallas_call(
        functools.partial(_expert_kernel, runs_per_expert=runs_per_expert),
        grid_spec=grid_spec,
        out_shape=jax.ShapeDtypeStruct(xs_pieces.shape, xs_pieces.dtype),
        input_output_aliases={5: 0},
        compiler_params=_cparams(("arbitrary",)),
        name="experts",
    )(block_expert, n_used, run_len, run_src, run_next, xs_pieces, w_gu, w_down)


def _combine_kernel(y_ref, slot_ref, w_ref, x_ref, gt_ref, gf_ref, o_ref, *, final):
    tm = x_ref.shape[1]
    slot = slot_ref[...].astype(F32)
    w = w_ref[...]
    col_io = lax.broadcasted_iota(jnp.int32, (tm, TILE_ROWS), 1).astype(F32)
    sel = jnp.zeros((tm, TILE_ROWS), F32)
    for k in range(TOP_K):
        sel = jnp.where(col_io == slot[:, k:k + 1], w[:, k:k + 1], sel)
    routed = jnp.dot(sel.astype(BF16), y_ref[0], preferred_element_type=F32)
    out = x_ref[0] + gt_ref[0] * routed
    if final:
        out = _rms(out, gf_ref[...])
    o_ref[0] = out


def _combine(y_tiles, slot_tok, w_tok, x_tiles, gt, g_final, seq_shape, tile_off, final):
    b, l = seq_shape
    _, tm, d = x_tiles.shape
    nt = l // tm
    tile = lambda bi, i: tile_off + bi * nt + i
    return pl.pallas_call(
        functools.partial(_combine_kernel, final=final),
        grid=(b, nt),
        in_specs=[pl.BlockSpec((1, TILE_ROWS, d), lambda bi, i: (tile(bi, i), 0, 0)),
                  pl.BlockSpec((tm, TOP_K), lambda bi, i: (tile(bi, i), 0)),
                  pl.BlockSpec((tm, TOP_K), lambda bi, i: (tile(bi, i), 0)),
                  pl.BlockSpec((1, tm, d), lambda bi, i: (tile(bi, i), 0, 0)),
                  pl.BlockSpec((1, 1, d), lambda bi, i: (bi, 0, 0)),
                  pl.BlockSpec(g_final.shape, lambda bi, i: (0, 0))],
        out_specs=pl.BlockSpec((1, tm, d), lambda bi, i: (bi, i, 0)),
        out_shape=jax.ShapeDtypeStruct((b, l, d), F32),
        compiler_params=_cparams(("parallel", "parallel")),
        name="combine",
    )(y_tiles, slot_tok, w_tok, x_tiles, gt, g_final)


def _rope_tables(l):
    rows = l // GRID_W
    row = jnp.repeat(jnp.arange(rows, dtype=F32), GRID_W)
    col = jnp.tile(jnp.arange(GRID_W, dtype=F32), rows)
    inv_freq = ROPE_THETA ** (-jnp.arange(ROPE_FREQS, dtype=F32) * 2.0 / (2 * ROPE_FREQS))
    ar, ac = row[:, None] * inv_freq, col[:, None] * inv_freq
    ones = jnp.ones((l, ROPE_LANE0), F32)
    zpad = jnp.zeros((l, HEAD_PAD - ROPE_LANE0 - QK_ROPE), F32)
    cos_t = jnp.concatenate([ones, jnp.cos(ar), jnp.cos(ar), jnp.cos(ac), jnp.cos(ac), zpad], axis=1)
    sin_t = jnp.concatenate([0.0 * ones, -jnp.sin(ar), jnp.sin(ar), -jnp.sin(ac), jnp.sin(ac), zpad], axis=1)
    return cos_t, sin_t


def _identity_tables(l):
    lane = jnp.arange(HEAD_PAD)
    cos_t = jnp.broadcast_to((lane < ROPE_LANE0 + QK_ROPE).astype(F32), (l, HEAD_PAD))
    return cos_t, jnp.zeros((l, HEAD_PAD), F32)


def _block_diag(w):
    g, a, b = w.shape
    out = jnp.zeros((g * a, g * b), w.dtype)
    for i in range(g):
        out = out.at[i * a:(i + 1) * a, i * b:(i + 1) * b].set(w[i])
    return out


def _prep_layer(w_in, w_pool, w_fno, w_uq, w_ukv):
    d = w_in.shape[0]
    kr = jnp.zeros((d, HEAD_PAD), w_in.dtype).at[:, ROPE_LANE0:ROPE_LANE0 + QK_ROPE].set(w_in[:, COL_KR:])
    w_in_p = jnp.concatenate([w_in[:, :COL_KR], kr], axis=1).astype(BF16)
    qk = QK_NOPE + QK_ROPE
    wq = w_uq.reshape(Q_LORA, N_HEADS, qk)
    wq = jnp.pad(wq, ((0, 0), (0, 0), (0, HEAD_PAD - qk))).reshape(Q_LORA, N_HEADS * HEAD_PAD).astype(BF16)
    wkv = w_ukv.reshape(KV_LORA, N_HEADS, QK_NOPE + V_DIM)
    wk = jnp.pad(wkv[..., :QK_NOPE], ((0, 0), (0, 0), (0, HEAD_PAD - QK_NOPE)))
    wv = jnp.pad(wkv[..., QK_NOPE:], ((0, 0), (0, 0), (0, HEAD_PAD - V_DIM)))
    wkv_p = jnp.concatenate([wk.reshape(KV_LORA, -1), wv.reshape(KV_LORA, -1)], axis=1).astype(BF16)
    return w_in_p, wq, wkv_p, _block_diag(w_pool).astype(BF16), _block_diag(w_fno).astype(BF16)


def _run_tables(n_pieces):
    n_all = n_pieces.shape[0]
    npc = n_pieces.astype(jnp.int32)
    piece_off = jnp.cumsum(npc, axis=1) - npc
    run_src = (jnp.arange(n_all, dtype=jnp.int32)[:, None] * PIECES_PER_TILE + piece_off).T.reshape(-1)
    run_len = npc.T.reshape(-1)
    n_runs = run_len.shape[0]
    run_id = jnp.arange(n_runs, dtype=jnp.int32)
    run_next = lax.cummin(jnp.where(run_len > 0, run_id, n_runs), axis=0, reverse=True)
    pad1 = lambda a, v: jnp.concatenate([a, jnp.full((1,), v, jnp.int32)])
    run_len, run_src, run_next = pad1(run_len, 0), pad1(run_src, 0), pad1(run_next, n_runs)
    blocks = (jnp.sum(npc, axis=0) + BLOCK_PIECES - 1) // BLOCK_PIECES
    block_end = jnp.cumsum(blocks)
    max_blocks = n_all * PIECES_PER_TILE // BLOCK_PIECES + N_EXPERTS
    blk = jnp.arange(max_blocks, dtype=jnp.int32)
    block_expert = jnp.minimum(jnp.sum((block_end[None, :] <= blk[:, None]).astype(jnp.int32), axis=1),
                               N_EXPERTS - 1)
    return block_expert, block_end[-1:], run_len, run_src, run_next


TM_PROJ = 256
TM_MIX = 256
ATTN_TQ = 2048
ATTN_TK = 1024


def kernel(x, c, ctx, c_ctx, w_mod, b_mod, g_mix, g_ffn, w_in, w_pool, pool_scale, w_fno, g_q, w_uq,
           g_kv, w_ukv, w_out, w_router, router_bias, w_gate, w_up, w_down, w_sh_gate, w_sh_up,
           w_sh_down, g_final):
    b, l, d = x.shape
    lc = ctx.shape[1]
    depth = w_mod.shape[0]
    cc = jnp.zeros((SUBLANES, d), F32).at[:b].set(c).at[b].set(c_ctx)
    mods = _modulation(cc, w_mod, b_mod)
    cos_t, sin_t = _rope_tables(l)
    cos_i, sin_i = _identity_tables(lc)
    xc = ctx
    row = lambda v: v.reshape(1, -1)
    for li in range(depth):
        last = li == depth - 1
        m = mods[li].reshape(SUBLANES, N_MOD, d)
        lat = [m[:b, j][:, None, :] for j in range(N_MOD)]
        cm = [jnp.broadcast_to(m[b, j][None, None, :], (b, 1, d)) for j in range(N_MOD)]
        w_in_p, wq_p, wkv_p, wp_bd, wf_bd = _prep_layer(w_in[li], w_pool[li], w_fno[li], w_uq[li], w_ukv[li])
        w_out_b = w_out[li].astype(BF16)
        gq, gkv, gm = row(g_q[li]), row(g_kv[li]), row(g_mix[li])

        up, uf, q, k, v = _in_projection(x, lat[0], lat[1], gm, w_in_p, gq, wq_p, gkv, wkv_p,
                                         cos_t, sin_t, TM_PROJ)
        upc, ufc, qc, kc, vc = _in_projection(xc, cm[0], cm[1], gm, w_in_p, gq, wq_p, gkv, wkv_p,
                                              cos_i, sin_i, TM_PROJ)
        attn = _attention(q, kc, vc, k, v, tq=ATTN_TQ, tk=ATTN_TK)
        f = _fourier(uf)
        ps = row(pool_scale[li])
        x = _mixer_output(x, lat[2], up, f, attn, wp_bd, ps, wf_bd, w_out_b, TM_MIX)
        if not last:
            attn_c = _attention(qc, kc, vc, tq=ATTN_TQ, tk=ATTN_TK)
            fc = _fourier(ufc)
            xc = _mixer_output(xc, cm[2], upc, fc, attn_c, wp_bd, ps, wf_bd, w_out_b, TM_MIX)

        w_rt = w_router[li].T
        rb = router_bias[li].reshape(N_EXPERTS, 1)
        wsg, wsu, wsd = w_sh_gate[li].astype(BF16), w_sh_up[li].astype(BF16), w_sh_down[li].astype(BF16)
        with_ctx = lambda j: jnp.concatenate([lat[j], cm[j][:1]], axis=0)
        x_tiles, xs, slot_t, w_t, n_pieces = _ffn_front(
            x, None if last else xc, with_ctx(3), with_ctx(4), with_ctx(5), row(g_ffn[li]), w_rt, rb, wsg, wsu, wsd)
        n_all = xs.shape[0]
        block_expert, n_used, run_len, run_src, run_next = _run_tables(n_pieces[..., 0])
        w_gu = jnp.concatenate([w_gate[li], w_up[li]], axis=-1).astype(BF16)
        y = _experts(block_expert, n_used, run_len, run_src, run_next,
                     xs.reshape(n_all * PIECES_PER_TILE, PIECE, d), w_gu, w_down[li].astype(BF16), n_all)
        y = y.reshape(n_all, TILE_ROWS, d)
        slot_tok, w_tok = slot_t.T, w_t.T
        x = _combine(y, slot_tok, w_tok, x_tiles, lat[5], row(g_final), (b, l), 0, last)
        if not last:
            xc = _combine(y, slot_tok, w_tok, x_tiles, cm[5], row(g_final), (b, lc), b * l // MOE_TILE, False)
    return x
```

```python
import functools
import math

import jax
import jax.numpy as jnp
import numpy as np
from jax import lax
from jax.experimental import pallas as pl
from jax.experimental.pallas import tpu as pltpu

F32 = jnp.float32
BF16 = jnp.bfloat16
HIGHEST = lax.Precision.HIGHEST

EPS = 1e-6
N_MOD = 6
GRID_W = 64
POOL_WINDOWS = (2, 4, 8, 16)
GROUP_DIM = 64
N_GROUPS = 4
MIX_GROUP_WIDTH = N_GROUPS * GROUP_DIM
N_HEADS = 8
QK_NOPE = 64
QK_ROPE = 32
V_DIM = 64
Q_LORA = 384
KV_LORA = 256
ROPE_FREQS = QK_ROPE // 4
ROPE_THETA = 10000.0
SOFTMAX_SCALE = (QK_NOPE + QK_ROPE) ** -0.5
N_EXPERTS = 64
TOP_K = 8
N_EXPERT_GROUPS = 8
TOPK_GROUPS = 4
EXPERTS_PER_GROUP = N_EXPERTS // N_EXPERT_GROUPS
ROUTED_SCALE = 2.5

LANES = 128
SUBLANES = 8
HEAD_PAD = LANES
VMEM_LIMIT = 56 * 1024 * 1024

COL_FNO = MIX_GROUP_WIDTH
COL_Q = COL_FNO + MIX_GROUP_WIDTH
COL_KV = COL_Q + Q_LORA
COL_KR = COL_KV + KV_LORA
IN_PAD = COL_KR + HEAD_PAD
ROPE_LANE0 = QK_NOPE

MOE_TILE = 256
PIECE = 16
PIECES_PER_TILE = MOE_TILE * TOP_K // PIECE + N_EXPERTS
TILE_ROWS = PIECES_PER_TILE * PIECE
BLOCK_PIECES = 16
EXPERT_BLOCK = BLOCK_PIECES * PIECE


def _cparams(sem, vmem=VMEM_LIMIT):
    return pltpu.CompilerParams(dimension_semantics=sem, vmem_limit_bytes=vmem)


def _rms(x, g):
    return x * lax.rsqrt(jnp.mean(x * x, axis=-1, keepdims=True) + EPS) * g


def _silu(x):
    return x * jax.nn.sigmoid(x)


def _mod_kernel(c_ref, w_ref, b_ref, o_ref):
    s = _silu(c_ref[...])
    o_ref[0] = jnp.dot(s, w_ref[0], preferred_element_type=F32, precision=HIGHEST) + b_ref[0]


def _modulation(cc, w_mod, b_mod):
    depth, d, nd = w_mod.shape
    n_chunks = nd // d
    return pl.pallas_call(
        _mod_kernel,
        grid=(depth, n_chunks),
        in_specs=[pl.BlockSpec((SUBLANES, d), lambda l, j: (0, 0)),
                  pl.BlockSpec((1, d, d), lambda l, j: (l, 0, j)),
                  pl.BlockSpec((1, 1, d), lambda l, j: (l, 0, j))],
        out_specs=pl.BlockSpec((1, SUBLANES, d), lambda l, j: (l, 0, j)),
        out_shape=jax.ShapeDtypeStruct((depth, SUBLANES, nd), F32),
        compiler_params=_cparams(("parallel", "parallel")),
        name="modulation",
    )(cc, w_mod, b_mod.reshape(depth, 1, nd))


def _rope(t, c, s):
    lane = lax.broadcasted_iota(jnp.int32, t.shape, 1)
    first_half = (lane % (2 * ROPE_FREQS)) < ROPE_FREQS
    partner = jnp.where(first_half, pltpu.roll(t, LANES - ROPE_FREQS, 1), pltpu.roll(t, ROPE_FREQS, 1))
    return t * c + partner * s


def _rope_rows(t, c, s):
    rw = lax.broadcasted_iota(jnp.int32, t.shape, 0)
    first_half = (rw % (2 * ROPE_FREQS)) < ROPE_FREQS
    n = t.shape[0]
    partner = jnp.where(first_half, pltpu.roll(t, n - ROPE_FREQS, 0), pltpu.roll(t, ROPE_FREQS, 0))
    return t * c + partner * s


V_ROWS = V_DIM + 16


def _inproj_kernel(x_ref, sh_ref, sc_ref, g_ref, win_ref, gq_ref, wuqt_ref, gkv_ref, wuk_ref, wuvt_ref,
                   cos_ref, sin_ref, cost_ref, sint_ref, up_ref, uf_ref, q_ref, k_ref, v_ref):
    x = x_ref[0]
    h = _rms(x, g_ref[...]) * (1.0 + sc_ref[0]) + sh_ref[0]
    u = jnp.dot(h.astype(BF16), win_ref[...], preferred_element_type=F32)
    up_ref[0] = u[:, :COL_FNO]
    uf_ref[0] = u[:, COL_FNO:COL_Q]
    nt = (((1,), (1,)), ((), ()))
    cq = _rms(u[:, COL_Q:COL_KV], gq_ref[...]).astype(BF16)
    qt = lax.dot_general(wuqt_ref[...], cq, nt, preferred_element_type=F32)
    ckv = _rms(u[:, COL_KV:COL_KR], gkv_ref[...]).astype(BF16)
    kn = jnp.dot(ckv, wuk_ref[...], preferred_element_type=F32)
    vt = lax.dot_general(wuvt_ref[...], ckv, nt, preferred_element_type=F32)
    kr = _rope(u[:, COL_KR:IN_PAD], cos_ref[...], sin_ref[...])
    cos_t, sin_t = cost_ref[...], sint_ref[...]
    tm = x.shape[0]
    ones_row = (lax.broadcasted_iota(jnp.int32, (V_ROWS, tm), 0) == V_DIM).astype(F32)
    q_scale = SOFTMAX_SCALE * math.log2(math.e)
    for hd in range(N_HEADS):
        lo = hd * HEAD_PAD
        q_ref[0, hd] = (_rope_rows(qt[lo:lo + HEAD_PAD], cos_t, sin_t) * q_scale).astype(BF16)
        k_ref[0, hd] = (kn[:, lo:lo + HEAD_PAD] + kr).astype(BF16)
        v_ref[0, hd] = (vt[hd * V_ROWS:(hd + 1) * V_ROWS] + ones_row).astype(BF16)


def _in_projection(x, sh, sc, g, w_in_p, g_q, w_uq_t, g_kv, w_uk_p, w_uv_t, tables, tm):
    b, l, d = x.shape
    tm = min(tm, l)
    cos_r, sin_r, cos_c, sin_c = tables
    full = lambda a: pl.BlockSpec(a.shape, lambda bi, i: (0,) * a.ndim)
    vec = pl.BlockSpec((1, 1, d), lambda bi, i: (bi, 0, 0))
    rtab = pl.BlockSpec((tm, HEAD_PAD), lambda bi, i: (i, 0))
    ctab = pl.BlockSpec((HEAD_PAD, tm), lambda bi, i: (0, i))
    return pl.pallas_call(
        _inproj_kernel,
        grid=(b, l // tm),
        in_specs=[pl.BlockSpec((1, tm, d), lambda bi, i: (bi, i, 0)), vec, vec,
                  full(g), full(w_in_p), full(g_q), full(w_uq_t), full(g_kv), full(w_uk_p), full(w_uv_t),
                  rtab, rtab, ctab, ctab],
        out_specs=[pl.BlockSpec((1, tm, MIX_GROUP_WIDTH), lambda bi, i: (bi, i, 0)),
                   pl.BlockSpec((1, tm, MIX_GROUP_WIDTH), lambda bi, i: (bi, i, 0)),
                   pl.BlockSpec((1, N_HEADS, HEAD_PAD, tm), lambda bi, i: (bi, 0, 0, i)),
                   pl.BlockSpec((1, N_HEADS, tm, HEAD_PAD), lambda bi, i: (bi, 0, i, 0)),
                   pl.BlockSpec((1, N_HEADS, V_ROWS, tm), lambda bi, i: (bi, 0, 0, i))],
        out_shape=[jax.ShapeDtypeStruct((b, l, MIX_GROUP_WIDTH), F32),
                   jax.ShapeDtypeStruct((b, l, MIX_GROUP_WIDTH), F32),
                   jax.ShapeDtypeStruct((b, N_HEADS, HEAD_PAD, l), BF16),
                   jax.ShapeDtypeStruct((b, N_HEADS, l, HEAD_PAD), BF16),
                   jax.ShapeDtypeStruct((b, N_HEADS, V_ROWS, l), BF16)],
        compiler_params=_cparams(("parallel", "parallel")),
        name="in_projection",
    )(x, sh, sc, g, w_in_p, g_q, w_uq_t, g_kv, w_uk_p, w_uv_t, cos_r, sin_r, cos_c, sin_c)


ATTN_HEADS_PER_STEP = 2


def _attn_kernel(*refs, tk, n_chunks):
    if n_chunks:
        q_ref, kc_ref, vc_ref, k_ref, v_ref, o_ref = refs
    else:
        q_ref, kc_ref, vc_ref, o_ref = refs
    tq = q_ref.shape[3]
    heads = range(ATTN_HEADS_PER_STEP)
    qs = [q_ref[0, hh] for hh in heads]

    def step(qt, kc, vt, m, acc):
        st = jnp.dot(kc, qt, preferred_element_type=F32)
        m_new = jnp.maximum(m, jnp.max(st, axis=0, keepdims=True))
        alpha = jnp.exp2(m - m_new)
        pt = jnp.exp2(st - m_new).astype(BF16)
        return m_new, alpha * acc + jnp.dot(vt, pt, preferred_element_type=F32)

    carry = tuple(step(qs[hh], kc_ref[0, hh], vc_ref[0, hh],
                       jnp.full((1, tq), -1e30, F32), jnp.zeros((V_ROWS, tq), F32)) for hh in heads)
    if n_chunks:
        def body(j, carry):
            off = pl.multiple_of(j * tk, tk)
            return tuple(step(qs[hh], k_ref[0, hh, pl.ds(off, tk), :], v_ref[0, hh, :, pl.ds(off, tk)],
                              *carry[hh]) for hh in heads)

        carry = lax.fori_loop(0, n_chunks, body, carry)
    for hh in heads:
        acc = carry[hh][1]
        out = acc[:V_DIM] / acc[V_DIM:V_DIM + 1]
        o_ref[0, hh * V_DIM:(hh + 1) * V_DIM, :] = out.astype(o_ref.dtype)


def _attention(q_t, k_ctx, v_ctx_t, k=None, v_t=None, *, tq, tk):
    b, h, _, l = q_t.shape
    lc = k_ctx.shape[2]
    tq = min(tq, l)
    hps = ATTN_HEADS_PER_STEP
    qspec = pl.BlockSpec((1, hps, HEAD_PAD, tq), lambda bi, hi, i: (bi, hi, 0, i))
    kcspec = pl.BlockSpec((1, hps, lc, HEAD_PAD), lambda bi, hi, i: (bi, hi, 0, 0))
    vcspec = pl.BlockSpec((1, hps, V_ROWS, lc), lambda bi, hi, i: (bi, hi, 0, 0))
    args, specs, n_chunks = [q_t, k_ctx, v_ctx_t], [qspec, kcspec, vcspec], 0
    if k is not None:
        lk = k.shape[2]
        tk = min(tk, lk)
        n_chunks = lk // tk
        once = pl.Buffered(1)
        args += [k, v_t]
        specs += [pl.BlockSpec((1, hps, lk, HEAD_PAD), lambda bi, hi, i: (bi, hi, 0, 0), pipeline_mode=once),
                  pl.BlockSpec((1, hps, V_ROWS, lk), lambda bi, hi, i: (bi, hi, 0, 0), pipeline_mode=once)]
    return pl.pallas_call(
        functools.partial(_attn_kernel, tk=tk, n_chunks=n_chunks),
        grid=(b, h // hps, l // tq),
        in_specs=specs,
        out_specs=pl.BlockSpec((1, hps * V_DIM, tq), lambda bi, hi, i: (bi, hi, i)),
        out_shape=jax.ShapeDtypeStruct((b, h * V_DIM, l), BF16),
        compiler_params=_cparams(("parallel", "parallel", "arbitrary")),
        name="attention",
    )(*args)


def _dft_step1_kernel(x_ref, c_ref, s_ref, yr_ref, yi_ref):
    xb = x_ref[0].astype(BF16)
    yr_ref[0] = jnp.dot(c_ref[...], xb, preferred_element_type=F32)
    yi_ref[0] = jnp.dot(s_ref[...], xb, preferred_element_type=F32)


def _dft_step2_kernel(yr_ref, yi_ref, tc_ref, ts_ref, c_ref, s_ref, cc_ref, sc_ref, o_ref):
    for j in range(SUBLANES):
        yr, yi = yr_ref[0, j], yi_ref[0, j]
        tc, ts = tc_ref[j], ts_ref[j]
        zr = (yr * tc - yi * ts).astype(BF16)
        zi = (yi * tc + yr * ts).astype(BF16)
        a = (jnp.dot(c_ref[...], zr, preferred_element_type=F32)
             - jnp.dot(s_ref[...], zi, preferred_element_type=F32))
        bm = (jnp.dot(s_ref[...], zr, preferred_element_type=F32)
              + jnp.dot(c_ref[...], zi, preferred_element_type=F32))
        o_ref[0, :, j, :] = (jnp.dot(a.astype(BF16), cc_ref[...], preferred_element_type=F32)
                             - jnp.dot(bm.astype(BF16), sc_ref[...], preferred_element_type=F32))


def _dft_dense_kernel(x_ref, c_ref, s_ref, cc_ref, sc_ref, o_ref):
    xb = x_ref[0].astype(BF16)
    a = jnp.dot(c_ref[...], xb, preferred_element_type=F32)
    bm = jnp.dot(s_ref[...], xb, preferred_element_type=F32)
    o_ref[0] = (jnp.dot(a.astype(BF16), cc_ref[...], preferred_element_type=F32)
                - jnp.dot(bm.astype(BF16), sc_ref[...], preferred_element_type=F32))


def _dft_mats(n):
    ang = 2.0 * np.pi * np.outer(np.arange(n), np.arange(n)) / n
    return np.cos(ang), np.sin(ang)


def _channel_dft(l):
    c, s = _dft_mats(GROUP_DIM)
    eye = np.eye(N_GROUPS)
    norm = 1.0 / math.sqrt(l * GROUP_DIM)
    return (jnp.asarray(np.kron(eye, c) * norm, BF16), jnp.asarray(np.kron(eye, s) * norm, BF16))


def _dft_factors(l):
    n1 = 1 << (int(math.log2(l)) // 2)
    return n1, l // n1


def _fourier(u_fno, lane_block=4096):
    b, l, c = u_fno.shape
    cc, sc = _channel_dft(l)
    full = lambda a: pl.BlockSpec(a.shape, lambda *_: (0,) * a.ndim)
    if l <= 512:
        cm, sm = _dft_mats(l)
        cm, sm = jnp.asarray(cm, BF16), jnp.asarray(sm, BF16)
        blk = pl.BlockSpec((1, l, c), lambda bi: (bi, 0, 0))
        return pl.pallas_call(
            _dft_dense_kernel, grid=(b,),
            in_specs=[blk, full(cm), full(sm), full(cc), full(sc)],
            out_specs=blk, out_shape=jax.ShapeDtypeStruct((b, l, c), F32),
            compiler_params=_cparams(("parallel",)), name="dft_dense",
        )(u_fno, cm, sm, cc, sc)
    n1, n2 = _dft_factors(l)
    c1, s1 = _dft_mats(n1)
    c2, s2 = _dft_mats(n2)
    c1, s1, c2, s2 = (jnp.asarray(m, BF16) for m in (c1, s1, c2, s2))
    ang = 2.0 * np.pi * np.outer(np.arange(n1), np.arange(n2)) / l
    tc = jnp.asarray(np.cos(ang)[:, :, None], F32)
    ts = jnp.asarray(np.sin(ang)[:, :, None], F32)
    w = n2 * c
    tn = min(lane_block, w)
    x2 = u_fno.reshape(b, n1, w)
    yspec = pl.BlockSpec((1, n1, tn), lambda bi, i: (bi, 0, i))
    yr, yi = pl.pallas_call(
        _dft_step1_kernel, grid=(b, w // tn),
        in_specs=[yspec, full(c1), full(s1)],
        out_specs=[yspec, yspec],
        out_shape=[jax.ShapeDtypeStruct((b, n1, w), F32)] * 2,
        compiler_params=_cparams(("parallel", "parallel")), name="dft_step1",
    )(x2, c1, s1)
    yr = yr.reshape(b, n1, n2, c)
    yi = yi.reshape(b, n1, n2, c)
    slab = pl.BlockSpec((1, SUBLANES, n2, c), lambda bi, i: (bi, i, 0, 0))
    tw = pl.BlockSpec((SUBLANES, n2, 1), lambda bi, i: (i, 0, 0))
    out = pl.pallas_call(
        _dft_step2_kernel, grid=(b, n1 // SUBLANES),
        in_specs=[slab, slab, tw, tw, full(c2), full(s2), full(cc), full(sc)],
        out_specs=pl.BlockSpec((1, n2, SUBLANES, c), lambda bi, i: (bi, 0, i, 0)),
        out_shape=jax.ShapeDtypeStruct((b, n2, n1, c), F32),
        compiler_params=_cparams(("parallel", "parallel")), name="dft_step2",
    )(yr, yi, tc, ts, c2, s2, cc, sc)
    return out.reshape(b, l, c)


POOL_HALO = SUBLANES


def _mixout_kernel(x_ref, gt_ref, up_ref, prev_ref, next_ref, f_ref, at_ref,
                   wp_ref, ps_ref, wf_ref, wo_ref, o_ref, *, seq_len):
    i = pl.program_id(1)
    tm = x_ref.shape[1]
    u = up_ref[0]
    prev = jnp.where(i > 0, prev_ref[0], 0.0)
    nxt = jnp.where(i < pl.num_programs(1) - 1, next_ref[0], 0.0)
    p = jnp.concatenate([prev, u, nxt], axis=0)
    n = tm + 2 * POOL_HALO
    s1 = pltpu.roll(p, 1, 0) + p
    s2 = pltpu.roll(s1, 1, 0) + pltpu.roll(s1, n - 1, 0)
    s4 = pltpu.roll(s2, 2, 0) + pltpu.roll(s2, n - 2, 0)
    s8 = pltpu.roll(s4, 4, 0) + pltpu.roll(s4, n - 4, 0)
    lane = lax.broadcasted_iota(jnp.int32, (tm, MIX_GROUP_WIDTH), 1)
    grp = lane // GROUP_DIM
    lo, hi = POOL_HALO, POOL_HALO + tm
    win = jnp.where(grp == 0, s1[lo:hi],
                    jnp.where(grp == 1, s2[lo:hi], jnp.where(grp == 2, s4[lo:hi], s8[lo:hi])))
    half = jnp.left_shift(1, grp)
    t = i * tm + lax.broadcasted_iota(jnp.int32, (tm, MIX_GROUP_WIDTH), 0)
    cnt = (jnp.minimum(t + half, seq_len) - jnp.maximum(t - half, 0)).astype(F32)
    dlt = win / cnt - u
    pool_y = jnp.dot(dlt.astype(BF16), wp_ref[...], preferred_element_type=F32) * ps_ref[...]
    fno_y = jnp.dot(f_ref[0].astype(BF16), wf_ref[...], preferred_element_type=F32)
    w = MIX_GROUP_WIDTH
    y = (jnp.dot(pool_y.astype(BF16), wo_ref[0:w, :], preferred_element_type=F32)
         + jnp.dot(fno_y.astype(BF16), wo_ref[w:2 * w, :], preferred_element_type=F32)
         + lax.dot_general(at_ref[0], wo_ref[2 * w:, :], (((0,), (0,)), ((), ())),
                           preferred_element_type=F32))
    o_ref[0] = x_ref[0] + gt_ref[0] * y


def _mixer_output(x, gt, u_pool, f, attn, wp_bd, pool_scale, wf_bd, w_out, tm):
    b, l, d = x.shape
    tm = min(tm, l)
    nb = tm // POOL_HALO
    last = l // POOL_HALO - 1
    full = lambda a: pl.BlockSpec(a.shape, lambda bi, i: (0,) * a.ndim)
    w = MIX_GROUP_WIDTH
    return pl.pallas_call(
        functools.partial(_mixout_kernel, seq_len=l),
        grid=(b, l // tm),
        in_specs=[pl.BlockSpec((1, tm, d), lambda bi, i: (bi, i, 0)),
                  pl.BlockSpec((1, 1, d), lambda bi, i: (bi, 0, 0)),
                  pl.BlockSpec((1, tm, w), lambda bi, i: (bi, i, 0)),
                  pl.BlockSpec((1, POOL_HALO, w), lambda bi, i: (bi, jnp.maximum(i * nb - 1, 0), 0)),
                  pl.BlockSpec((1, POOL_HALO, w), lambda bi, i: (bi, jnp.minimum((i + 1) * nb, last), 0)),
                  pl.BlockSpec((1, tm, w), lambda bi, i: (bi, i, 0)),
                  pl.BlockSpec((1, attn.shape[1], tm), lambda bi, i: (bi, 0, i)),
                  full(wp_bd), full(pool_scale), full(wf_bd), full(w_out)],
        out_specs=pl.BlockSpec((1, tm, d), lambda bi, i: (bi, i, 0)),
        out_shape=jax.ShapeDtypeStruct((b, l, d), F32),
        compiler_params=_cparams(("parallel", "arbitrary")),
        name="mixer_output",
    )(x, gt, u_pool, u_pool, u_pool, f, attn, wp_bd, pool_scale, wf_bd, w_out)


def _first_index_of_max(vals, iota, n):
    mx = jnp.max(vals, axis=0, keepdims=True)
    ix = jnp.min(jnp.where(vals == mx, iota, n), axis=0, keepdims=True)
    return mx, ix


def _ffn_front_kernel(*refs, n_lat, has_ctx):
    if has_ctx:
        x_ref, xc_ref = refs[:2]
        refs = refs[2:]
    else:
        x_ref, refs = refs[0], refs[1:]
    (sh_ref, sc_ref, gt_ref, g_ref, wr_ref, rb_ref, wsg_ref, wsu_ref, wsd_ref,
     xo_ref, xs_ref, slot_ref, wt_ref, cnt_ref) = refs
    x = x_ref[0]
    if has_ctx:
        x = jnp.where(pl.program_id(0) < n_lat, x, xc_ref[0])
    tm = x.shape[0]
    h = _rms(x, g_ref[...]) * (1.0 + sc_ref[0]) + sh_ref[0]

    logits = lax.dot_general(wr_ref[...], h, (((1,), (1,)), ((), ())),
                             preferred_element_type=F32, precision=HIGHEST)
    aff = jax.nn.sigmoid(logits)
    sel = aff + rb_ref[...]
    e_iota = lax.broadcasted_iota(jnp.int32, (N_EXPERTS, tm), 0).astype(F32)
    neg = jnp.float32(-jnp.inf)
    gscores = []
    for g in range(N_EXPERT_GROUPS):
        blk = sel[g * EXPERTS_PER_GROUP:(g + 1) * EXPERTS_PER_GROUP]
        it = lax.broadcasted_iota(jnp.int32, blk.shape, 0).astype(F32)
        m1, i1 = _first_index_of_max(blk, it, float(EXPERTS_PER_GROUP))
        m2 = jnp.max(jnp.where(it == i1, neg, blk), axis=0, keepdims=True)
        gscores.append(m1 + m2)
    gs = jnp.concatenate(gscores, axis=0)
    g_iota = lax.broadcasted_iota(jnp.int32, gs.shape, 0).astype(F32)
    gself = jnp.zeros(gs.shape, F32)
    for _ in range(TOPK_GROUPS):
        _, ig = _first_index_of_max(gs, g_iota, float(N_EXPERT_GROUPS))
        hit = g_iota == ig
        gself = jnp.where(hit, 1.0, gself)
        gs = jnp.where(hit, neg, gs)
    emask = jnp.concatenate(
        [jnp.broadcast_to(gself[g:g + 1], (EXPERTS_PER_GROUP, tm)) for g in range(N_EXPERT_GROUPS)], axis=0)
    masked = jnp.where(emask > 0.5, sel, neg)
    hits, wts = [], []
    chosen = jnp.zeros((N_EXPERTS, tm), F32)
    for _ in range(TOP_K):
        _, ie = _first_index_of_max(masked, e_iota, float(N_EXPERTS))
        hit = e_iota == ie
        hits.append(hit)
        wts.append(jnp.sum(jnp.where(hit, aff, 0.0), axis=0, keepdims=True))
        chosen = jnp.where(hit, 1.0, chosen)
        masked = jnp.where(hit, neg, masked)
    wsum = wts[0]
    for w in wts[1:]:
        wsum = wsum + w
    scale = ROUTED_SCALE / wsum
    wt_ref[...] = jnp.concatenate([w * scale for w in wts], axis=0)

    r_io = lax.broadcasted_iota(jnp.int32, (tm, tm), 0)
    c_io = lax.broadcasted_iota(jnp.int32, (tm, tm), 1)
    tri = (r_io <= c_io).astype(BF16)
    incl = jnp.dot(chosen.astype(BF16), tri, preferred_element_type=F32)
    count = jnp.sum(chosen, axis=1, keepdims=True)
    n_pieces = jnp.floor((count + (PIECE - 1)) * (1.0 / PIECE))
    cnt_ref[0] = n_pieces
    e_r = lax.broadcasted_iota(jnp.int32, (N_EXPERTS, N_EXPERTS), 0)
    e_c = lax.broadcasted_iota(jnp.int32, (N_EXPERTS, N_EXPERTS), 1)
    before = (e_c < e_r).astype(BF16)
    piece_off = jnp.dot(before, jnp.broadcast_to(n_pieces, (N_EXPERTS, LANES)).astype(BF16),
                        preferred_element_type=F32)[:, :1]
    pos = piece_off * PIECE + incl - chosen
    slots = [jnp.sum(jnp.where(hit, pos, 0.0), axis=0, keepdims=True) for hit in hits]
    slot_ref[...] = jnp.concatenate(slots, axis=0).astype(jnp.int32)

    row_io = lax.broadcasted_iota(jnp.int32, (TILE_ROWS, tm), 0).astype(F32)
    onehot = jnp.zeros((TILE_ROWS, tm), F32)
    for s in slots:
        onehot = jnp.where(row_io == s, 1.0, onehot)
    onehot = onehot.astype(BF16)
    hb = h.astype(BF16)
    d = hb.shape[1]
    nc = 2 * LANES
    for j in range(d // nc):
        xs_ref[0, :, j * nc:(j + 1) * nc] = jnp.dot(
            onehot, hb[:, j * nc:(j + 1) * nc], preferred_element_type=F32).astype(BF16)

    a = jnp.dot(hb, wsg_ref[...], preferred_element_type=F32)
    bb = jnp.dot(hb, wsu_ref[...], preferred_element_type=F32)
    sh_out = jnp.dot((_silu(a) * bb).astype(BF16), wsd_ref[...], preferred_element_type=F32)
    xo_ref[0] = x + gt_ref[0] * sh_out


def _ffn_front(x, xc, sh, sc, gt, g, w_rt, rb, wsg, wsu, wsd):
    b, l, d = x.shape
    tm = MOE_TILE
    nt = l // tm
    n_lat = b * nt
    has_ctx = xc is not None
    ntc = xc.shape[1] // tm if has_ctx else 0
    n_all = n_lat + b * ntc
    full = lambda a: pl.BlockSpec(a.shape, lambda i: (0,) * a.ndim)

    def lat_map(i):
        j = jnp.minimum(i, n_lat - 1)
        return (j // nt, j % nt, 0)

    def ctx_map(i):
        j = jnp.clip(i - n_lat, 0, b * ntc - 1)
        return (j // ntc, j % ntc, 0)

    vec = pl.BlockSpec((1, 1, d), lambda i: (jnp.where(i < n_lat, i // nt, b), 0, 0))
    xspecs = [pl.BlockSpec((1, tm, d), lat_map)] + ([pl.BlockSpec((1, tm, d), ctx_map)] if has_ctx else [])
    xargs = [x] + ([xc] if has_ctx else [])
    tokmap = lambda i: (0, i)
    return pl.pallas_call(
        functools.partial(_ffn_front_kernel, n_lat=n_lat, has_ctx=has_ctx),
        grid=(n_all,),
        in_specs=xspecs + [vec, vec, vec, full(g), full(w_rt), full(rb), full(wsg), full(wsu), full(wsd)],
        out_specs=[pl.BlockSpec((1, tm, d), lambda i: (i, 0, 0)),
                   pl.BlockSpec((1, TILE_ROWS, d), lambda i: (i, 0, 0)),
                   pl.BlockSpec((TOP_K, tm), tokmap), pl.BlockSpec((TOP_K, tm), tokmap),
                   pl.BlockSpec((1, N_EXPERTS, 1), lambda i: (i, 0, 0))],
        out_shape=[jax.ShapeDtypeStruct((n_all, tm, d), F32),
                   jax.ShapeDtypeStruct((n_all, TILE_ROWS, d), BF16),
                   jax.ShapeDtypeStruct((TOP_K, n_all * tm), jnp.int32),
                   jax.ShapeDtypeStruct((TOP_K, n_all * tm), F32),
                   jax.ShapeDtypeStruct((n_all, N_EXPERTS, 1), F32)],
        compiler_params=_cparams(("parallel",)),
        name="ffn_front",
    )(*xargs, sh, sc, gt, g, w_rt, rb, wsg, wsu, wsd)


def _expert_kernel(bexp_ref, nused_ref, rlen_ref, rsrc_ref, rnext_ref, xs_ref, wgu_ref, wd_ref, y_ref,
                   state, ids, n_real, n_sent, xbuf, ybuf, sem_in, sem_out, *, runs_per_expert):
    i = pl.program_id(0)
    n_used = nused_ref[0]
    end_run = rlen_ref.shape[0] - 1

    def copy_in(pid, slot, p):
        return pltpu.make_async_copy(xs_ref.at[pid], xbuf.at[slot, pl.ds(p * PIECE, PIECE), :], sem_in.at[slot])

    def copy_out(pid, slot, p):
        off = pl.multiple_of(p * PIECE, PIECE)
        return pltpu.make_async_copy(ybuf.at[slot, pl.ds(off, PIECE), :], y_ref.at[pid], sem_out.at[slot])

    def issue_gather(blk, slot):
        r_lo = bexp_ref[blk] * runs_per_expert
        r_end = r_lo + runs_per_expert
        fresh = state[0] < r_lo
        r = jnp.where(fresh, r_lo, state[0])
        o = jnp.where(fresh, 0, state[1])
        read_pid = jnp.int32(0)
        count = jnp.int32(0)
        for p in range(BLOCK_PIECES):
            exhausted = o >= rlen_ref[jnp.minimum(r, end_run)]
            r = jnp.where(exhausted, rnext_ref[jnp.minimum(r + 1, end_run)], r)
            o = jnp.where(exhausted, 0, o)
            valid = (r < r_end).astype(jnp.int32)
            read_pid = jnp.where(valid > 0, rsrc_ref[jnp.minimum(r, end_run)] + o, read_pid)
            ids[slot, p] = read_pid
            copy_in(read_pid, slot, p).start()
            o = o + valid
            count = count + valid
        state[0] = r
        state[1] = o
        n_real[slot] = count

    def wait_gather(slot):
        pltpu.make_async_copy(xbuf.at[slot], xbuf.at[slot], sem_in.at[slot]).wait()

    def send_results(slot):
        n = n_real[slot]

        def body(p, c):
            copy_out(ids[slot, p], slot, p).start()
            return c

        lax.fori_loop(0, n, body, 0)
        n_sent[slot] = n

    def wait_results(slot):
        def body(p, c):
            copy_out(0, slot, 0).wait()
            return c

        lax.fori_loop(0, n_sent[slot], body, 0)

    @pl.when(i < n_used)
    def _():
        slot = i % 2

        @pl.when(i == 0)
        def _():
            state[0] = 0
            state[1] = 0
            issue_gather(0, 0)

        @pl.when(i + 1 < n_used)
        def _():
            issue_gather(i + 1, (i + 1) % 2)

        wait_gather(slot)

        @pl.when(i >= 2)
        def _():
            wait_results(slot)

        gu = jnp.dot(xbuf[slot], wgu_ref[0], preferred_element_type=F32)
        f = gu.shape[1] // 2
        act = (_silu(gu[:, :f]) * gu[:, f:]).astype(BF16)
        ybuf[slot] = jnp.dot(act, wd_ref[0], preferred_element_type=F32).astype(BF16)
        send_results(slot)

        @pl.when(i == n_used - 1)
        def _():
            wait_results(slot)

            @pl.when(i >= 1)
            def _():
                wait_results(1 - slot)


def _experts(block_expert, n_used, run_len, run_src, run_next, xs_pieces, w_gu, w_down, runs_per_expert):
    n_blocks = block_expert.shape[0]
    _, _, d = xs_pieces.shape
    _, _, f2 = w_gu.shape
    grid_spec = pltpu.PrefetchScalarGridSpec(
        num_scalar_prefetch=5,
        grid=(n_blocks,),
        in_specs=[pl.BlockSpec(memory_space=pl.ANY),
                  pl.BlockSpec((1, d, f2), lambda i, be, *_: (be[i], 0, 0)),
                  pl.BlockSpec((1, f2 // 2, d), lambda i, be, *_: (be[i], 0, 0))],
        out_specs=pl.BlockSpec(memory_space=pl.ANY),
        scratch_shapes=[pltpu.SMEM((2,), jnp.int32),
                        pltpu.SMEM((2, BLOCK_PIECES), jnp.int32),
                        pltpu.SMEM((2,), jnp.int32),
                        pltpu.SMEM((2,), jnp.int32),
                        pltpu.VMEM((2, EXPERT_BLOCK, d), BF16),
                        pltpu.VMEM((2, EXPERT_BLOCK, d), BF16),
                        pltpu.SemaphoreType.DMA((2,)), pltpu.SemaphoreType.DMA((2,))],
    )
    return pl.pallas_call(
        functools.partial(_expert_kernel, runs_per_expert=runs_per_expert),
        grid_spec=grid_spec,
        out_shape=jax.ShapeDtypeStruct(xs_pieces.shape, xs_pieces.dtype),
        input_output_aliases={5: 0},
        compiler_params=_cparams(("arbitrary",)),
        name="experts",
    )(block_expert, n_used, run_len, run_src, run_next, xs_pieces, w_gu, w_down)


def _combine_kernel(y_ref, slot_ref, w_ref, x_ref, gt_ref, gf_ref, o_ref, *, final):
    tm = x_ref.shape[1]
    slot = slot_ref[...].astype(F32)
    w = w_ref[...]
    col_io = lax.broadcasted_iota(jnp.int32, (tm, TILE_ROWS), 1).astype(F32)
    sel = jnp.zeros((tm, TILE_ROWS), F32)
    for k in range(TOP_K):
        sel = jnp.where(col_io == slot[:, k:k + 1], w[:, k:k + 1], sel)
    routed = jnp.dot(sel.astype(BF16), y_ref[0], preferred_element_type=F32)
    out = x_ref[0] + gt_ref[0] * routed
    if final:
        out = _rms(out, gf_ref[...])
    o_ref[0] = out


def _combine(y_tiles, slot_tok, w_tok, x_tiles, gt, g_final, seq_shape, tile_off, final):
    b, l = seq_shape
    _, tm, d = x_tiles.shape
    nt = l // tm
    tile = lambda bi, i: tile_off + bi * nt + i
    return pl.pallas_call(
        functools.partial(_combine_kernel, final=final),
        grid=(b, nt),
        in_specs=[pl.BlockSpec((1, TILE_ROWS, d), lambda bi, i: (tile(bi, i), 0, 0)),
                  pl.BlockSpec((tm, TOP_K), lambda bi, i: (tile(bi, i), 0)),
                  pl.BlockSpec((tm, TOP_K), lambda bi, i: (tile(bi, i), 0)),
                  pl.BlockSpec((1, tm, d), lambda bi, i: (tile(bi, i), 0, 0)),
                  pl.BlockSpec((1, 1, d), lambda bi, i: (bi, 0, 0)),
                  pl.BlockSpec(g_final.shape, lambda bi, i: (0, 0))],
        out_specs=pl.BlockSpec((1, tm, d), lambda bi, i: (bi, i, 0)),
        out_shape=jax.ShapeDtypeStruct((b, l, d), F32),
        compiler_params=_cparams(("parallel", "parallel")),
        name="combine",
    )(y_tiles, slot_tok, w_tok, x_tiles, gt, g_final)


def _rope_tables(l):
    rows = l // GRID_W
    row = jnp.repeat(jnp.arange(rows, dtype=F32), GRID_W)
    col = jnp.tile(jnp.arange(GRID_W, dtype=F32), rows)
    inv_freq = ROPE_THETA ** (-jnp.arange(ROPE_FREQS, dtype=F32) * 2.0 / (2 * ROPE_FREQS))
    ar, ac = row[:, None] * inv_freq, col[:, None] * inv_freq
    ones = jnp.ones((l, ROPE_LANE0), F32)
    zpad = jnp.zeros((l, HEAD_PAD - ROPE_LANE0 - QK_ROPE), F32)
    cos_t = jnp.concatenate([ones, jnp.cos(ar), jnp.cos(ar), jnp.cos(ac), jnp.cos(ac), zpad], axis=1)
    sin_t = jnp.concatenate([0.0 * ones, -jnp.sin(ar), jnp.sin(ar), -jnp.sin(ac), jnp.sin(ac), zpad], axis=1)
    return cos_t, sin_t


def _identity_tables(l):
    lane = jnp.arange(HEAD_PAD)
    cos_t = jnp.broadcast_to((lane < ROPE_LANE0 + QK_ROPE).astype(F32), (l, HEAD_PAD))
    return cos_t, jnp.zeros((l, HEAD_PAD), F32)


def _block_diag(w):
    g, a, b = w.shape
    out = jnp.zeros((g * a, g * b), w.dtype)
    for i in range(g):
        out = out.at[i * a:(i + 1) * a, i * b:(i + 1) * b].set(w[i])
    return out


def _prep_layer(w_in, w_pool, w_fno, w_uq, w_ukv):
    d = w_in.shape[0]
    kr = jnp.zeros((d, HEAD_PAD), w_in.dtype).at[:, ROPE_LANE0:ROPE_LANE0 + QK_ROPE].set(w_in[:, COL_KR:])
    w_in_p = jnp.concatenate([w_in[:, :COL_KR], kr], axis=1).astype(BF16)
    qk = QK_NOPE + QK_ROPE
    wq = w_uq.reshape(Q_LORA, N_HEADS, qk)
    wq = jnp.pad(wq, ((0, 0), (0, 0), (0, HEAD_PAD - qk))).reshape(Q_LORA, N_HEADS * HEAD_PAD)
    wkv = w_ukv.reshape(KV_LORA, N_HEADS, QK_NOPE + V_DIM)
    wk = jnp.pad(wkv[..., :QK_NOPE], ((0, 0), (0, 0), (0, HEAD_PAD - QK_NOPE))).reshape(KV_LORA, -1)
    wv = jnp.pad(wkv[..., QK_NOPE:], ((0, 0), (0, 0), (0, V_ROWS - V_DIM))).reshape(KV_LORA, -1)
    return (w_in_p, wq.T.astype(BF16), wk.astype(BF16), wv.T.astype(BF16),
            _block_diag(w_pool).astype(BF16), _block_diag(w_fno).astype(BF16))


def _run_tables(n_pieces):
    n_all = n_pieces.shape[0]
    npc = n_pieces.astype(jnp.int32)
    piece_off = jnp.cumsum(npc, axis=1) - npc
    run_src = (jnp.arange(n_all, dtype=jnp.int32)[:, None] * PIECES_PER_TILE + piece_off).T.reshape(-1)
    run_len = npc.T.reshape(-1)
    n_runs = run_len.shape[0]
    run_id = jnp.arange(n_runs, dtype=jnp.int32)
    run_next = lax.cummin(jnp.where(run_len > 0, run_id, n_runs), axis=0, reverse=True)
    pad1 = lambda a, v: jnp.concatenate([a, jnp.full((1,), v, jnp.int32)])
    run_len, run_src, run_next = pad1(run_len, 0), pad1(run_src, 0), pad1(run_next, n_runs)
    blocks = (jnp.sum(npc, axis=0) + BLOCK_PIECES - 1) // BLOCK_PIECES
    block_end = jnp.cumsum(blocks)
    max_blocks = n_all * PIECES_PER_TILE // BLOCK_PIECES + N_EXPERTS
    blk = jnp.arange(max_blocks, dtype=jnp.int32)
    block_expert = jnp.minimum(jnp.sum((block_end[None, :] <= blk[:, None]).astype(jnp.int32), axis=1),
                               N_EXPERTS - 1)
    return block_expert, block_end[-1:], run_len, run_src, run_next


TM_PROJ = 256
TM_MIX = 256
ATTN_TQ = 2048
ATTN_TK = 1024


def kernel(x, c, ctx, c_ctx, w_mod, b_mod, g_mix, g_ffn, w_in, w_pool, pool_scale, w_fno, g_q, w_uq,
           g_kv, w_ukv, w_out, w_router, router_bias, w_gate, w_up, w_down, w_sh_gate, w_sh_up,
           w_sh_down, g_final):
    b, l, d = x.shape
    lc = ctx.shape[1]
    depth = w_mod.shape[0]
    cc = jnp.zeros((SUBLANES, d), F32).at[:b].set(c).at[b].set(c_ctx)
    mods = _modulation(cc, w_mod, b_mod)
    with_transposed = lambda cs: (cs[0], cs[1], cs[0].T, cs[1].T)
    tab_lat = with_transposed(_rope_tables(l))
    tab_ctx = with_transposed(_identity_tables(lc))
    xc = ctx
    row = lambda v: v.reshape(1, -1)
    for li in range(depth):
        last = li == depth - 1
        m = mods[li].reshape(SUBLANES, N_MOD, d)
        lat = [m[:b, j][:, None, :] for j in range(N_MOD)]
        cm = [jnp.broadcast_to(m[b, j][None, None, :], (b, 1, d)) for j in range(N_MOD)]
        w_in_p, wq_t, wk_p, wv_t, wp_bd, wf_bd = _prep_layer(w_in[li], w_pool[li], w_fno[li], w_uq[li], w_ukv[li])
        w_out_b = w_out[li].astype(BF16)
        gq, gkv, gm = row(g_q[li]), row(g_kv[li]), row(g_mix[li])

        up, uf, q, k, v = _in_projection(x, lat[0], lat[1], gm, w_in_p, gq, wq_t, gkv, wk_p, wv_t,
                                         tab_lat, TM_PROJ)
        upc, ufc, qc, kc, vc = _in_projection(xc, cm[0], cm[1], gm, w_in_p, gq, wq_t, gkv, wk_p, wv_t,
                                              tab_ctx, TM_PROJ)
        attn = _attention(q, kc, vc, k, v, tq=ATTN_TQ, tk=ATTN_TK)
        f = _fourier(uf)
        ps = row(pool_scale[li])
        x = _mixer_output(x, lat[2], up, f, attn, wp_bd, ps, wf_bd, w_out_b, TM_MIX)
        if not last:
            attn_c = _attention(qc, kc, vc, tq=ATTN_TQ, tk=ATTN_TK)
            fc = _fourier(ufc)
            xc = _mixer_output(xc, cm[2], upc, fc, attn_c, wp_bd, ps, wf_bd, w_out_b, TM_MIX)

        w_rt = w_router[li].T
        rb = router_bias[li].reshape(N_EXPERTS, 1)
        wsg, wsu, wsd = w_sh_gate[li].astype(BF16), w_sh_up[li].astype(BF16), w_sh_down[li].astype(BF16)
        with_ctx = lambda j: jnp.concatenate([lat[j], cm[j][:1]], axis=0)
        x_tiles, xs, slot_t, w_t, n_pieces = _ffn_front(
            x, None if last else xc, with_ctx(3), with_ctx(4), with_ctx(5), row(g_ffn[li]), w_rt, rb, wsg, wsu, wsd)
        n_all = xs.shape[0]
        block_expert, n_used, run_len, run_src, run_next = _run_tables(n_pieces[..., 0])
        w_gu = jnp.concatenate([w_gate[li], w_up[li]], axis=-1).astype(BF16)
        y = _experts(block_expert, n_used, run_len, run_src, run_next,
                     xs.reshape(n_all * PIECES_PER_TILE, PIECE, d), w_gu, w_down[li].astype(BF16), n_all)
        y = y.reshape(n_all, TILE_ROWS, d)
        slot_tok, w_tok = slot_t.T, w_t.T
        x = _combine(y, slot_tok, w_tok, x_tiles, lat[5], row(g_final), (b, l), 0, last)
        if not last:
            xc = _combine(y, slot_tok, w_tok, x_tiles, cm[5], row(g_final), (b, lc), b * l // MOE_TILE, False)
    return x
```

```python
import functools
import math

import jax
import jax.numpy as jnp
import numpy as np
from jax import lax
from jax.experimental import pallas as pl
from jax.experimental.pallas import tpu as pltpu

F32 = jnp.float32
BF16 = jnp.bfloat16
HIGHEST = lax.Precision.HIGHEST

EPS = 1e-6
N_MOD = 6
GRID_W = 64
POOL_WINDOWS = (2, 4, 8, 16)
GROUP_DIM = 64
N_GROUPS = 4
MIX_GROUP_WIDTH = N_GROUPS * GROUP_DIM
N_HEADS = 8
QK_NOPE = 64
QK_ROPE = 32
V_DIM = 64
Q_LORA = 384
KV_LORA = 256
ROPE_FREQS = QK_ROPE // 4
ROPE_THETA = 10000.0
SOFTMAX_SCALE = (QK_NOPE + QK_ROPE) ** -0.5
N_EXPERTS = 64
TOP_K = 8
N_EXPERT_GROUPS = 8
TOPK_GROUPS = 4
EXPERTS_PER_GROUP = N_EXPERTS // N_EXPERT_GROUPS
ROUTED_SCALE = 2.5

LANES = 128
SUBLANES = 8
HEAD_PAD = LANES
VMEM_LIMIT = 56 * 1024 * 1024

COL_FNO = MIX_GROUP_WIDTH
COL_Q = COL_FNO + MIX_GROUP_WIDTH
COL_KV = COL_Q + Q_LORA
COL_KR = COL_KV + KV_LORA
IN_PAD = COL_KR + HEAD_PAD
ROPE_LANE0 = QK_NOPE

MOE_TILE = 256
PIECE = 16
PIECES_PER_TILE = MOE_TILE * TOP_K // PIECE + N_EXPERTS
TILE_ROWS = PIECES_PER_TILE * PIECE
BLOCK_PIECES = 32
EXPERT_BLOCK = BLOCK_PIECES * PIECE


def _cparams(sem, vmem=VMEM_LIMIT):
    return pltpu.CompilerParams(dimension_semantics=sem, vmem_limit_bytes=vmem)


def _rms(x, g):
    return x * lax.rsqrt(jnp.mean(x * x, axis=-1, keepdims=True) + EPS) * g


def _silu(x):
    return x * jax.nn.sigmoid(x)


def _mod_kernel(c_ref, w_ref, b_ref, o_ref):
    s = _silu(c_ref[...])
    o_ref[0] = jnp.dot(s, w_ref[0], preferred_element_type=F32, precision=HIGHEST) + b_ref[0]


def _modulation(cc, w_mod, b_mod):
    depth, d, nd = w_mod.shape
    n_chunks = nd // d
    return pl.pallas_call(
        _mod_kernel,
        grid=(depth, n_chunks),
        in_specs=[pl.BlockSpec((SUBLANES, d), lambda l, j: (0, 0)),
                  pl.BlockSpec((1, d, d), lambda l, j: (l, 0, j)),
                  pl.BlockSpec((1, 1, d), lambda l, j: (l, 0, j))],
        out_specs=pl.BlockSpec((1, SUBLANES, d), lambda l, j: (l, 0, j)),
        out_shape=jax.ShapeDtypeStruct((depth, SUBLANES, nd), F32),
        compiler_params=_cparams(("parallel", "parallel")),
        name="modulation",
    )(cc, w_mod, b_mod.reshape(depth, 1, nd))


def _rope(t, c, s):
    lane = lax.broadcasted_iota(jnp.int32, t.shape, 1)
    first_half = (lane % (2 * ROPE_FREQS)) < ROPE_FREQS
    partner = jnp.where(first_half, pltpu.roll(t, LANES - ROPE_FREQS, 1), pltpu.roll(t, ROPE_FREQS, 1))
    return t * c + partner * s


def _rope_rows(t, c, s):
    rw = lax.broadcasted_iota(jnp.int32, t.shape, 0)
    first_half = (rw % (2 * ROPE_FREQS)) < ROPE_FREQS
    n = t.shape[0]
    partner = jnp.where(first_half, pltpu.roll(t, n - ROPE_FREQS, 0), pltpu.roll(t, ROPE_FREQS, 0))
    return t * c + partner * s


V_ROWS = V_DIM + 16


def _inproj_kernel(x_ref, sh_ref, sc_ref, g_ref, win_ref, gq_ref, wuqt_ref, gkv_ref, wuk_ref, wuvt_ref,
                   cos_ref, sin_ref, cost_ref, sint_ref, up_ref, uf_ref, q_ref, k_ref, v_ref):
    x = x_ref[0]
    h = _rms(x, g_ref[...]) * (1.0 + sc_ref[0]) + sh_ref[0]
    u = jnp.dot(h.astype(BF16), win_ref[...], preferred_element_type=F32)
    up_ref[0] = u[:, :COL_FNO]
    uf_ref[0] = u[:, COL_FNO:COL_Q]
    nt = (((1,), (1,)), ((), ()))
    cq = _rms(u[:, COL_Q:COL_KV], gq_ref[...]).astype(BF16)
    qt = lax.dot_general(wuqt_ref[...], cq, nt, preferred_element_type=F32)
    ckv = _rms(u[:, COL_KV:COL_KR], gkv_ref[...]).astype(BF16)
    kn = jnp.dot(ckv, wuk_ref[...], preferred_element_type=F32)
    vt = lax.dot_general(wuvt_ref[...], ckv, nt, preferred_element_type=F32)
    kr = _rope(u[:, COL_KR:IN_PAD], cos_ref[...], sin_ref[...])
    cos_t, sin_t = cost_ref[...], sint_ref[...]
    tm = x.shape[0]
    ones_row = (lax.broadcasted_iota(jnp.int32, (V_ROWS, tm), 0) == V_DIM).astype(F32)
    q_scale = SOFTMAX_SCALE * math.log2(math.e)
    for hd in range(N_HEADS):
        lo = hd * HEAD_PAD
        q_ref[0, hd] = (_rope_rows(qt[lo:lo + HEAD_PAD], cos_t, sin_t) * q_scale).astype(BF16)
        k_ref[0, hd] = (kn[:, lo:lo + HEAD_PAD] + kr).astype(BF16)
        v_ref[0, hd] = (vt[hd * V_ROWS:(hd + 1) * V_ROWS] + ones_row).astype(BF16)


def _in_projection(x, sh, sc, g, w_in_p, g_q, w_uq_t, g_kv, w_uk_p, w_uv_t, tables, tm):
    b, l, d = x.shape
    tm = min(tm, l)
    cos_r, sin_r, cos_c, sin_c = tables
    full = lambda a: pl.BlockSpec(a.shape, lambda bi, i: (0,) * a.ndim)
    vec = pl.BlockSpec((1, 1, d), lambda bi, i: (bi, 0, 0))
    rtab = pl.BlockSpec((tm, HEAD_PAD), lambda bi, i: (i, 0))
    ctab = pl.BlockSpec((HEAD_PAD, tm), lambda bi, i: (0, i))
    return pl.pallas_call(
        _inproj_kernel,
        grid=(b, l // tm),
        in_specs=[pl.BlockSpec((1, tm, d), lambda bi, i: (bi, i, 0)), vec, vec,
                  full(g), full(w_in_p), full(g_q), full(w_uq_t), full(g_kv), full(w_uk_p), full(w_uv_t),
                  rtab, rtab, ctab, ctab],
        out_specs=[pl.BlockSpec((1, tm, MIX_GROUP_WIDTH), lambda bi, i: (bi, i, 0)),
                   pl.BlockSpec((1, tm, MIX_GROUP_WIDTH), lambda bi, i: (bi, i, 0)),
                   pl.BlockSpec((1, N_HEADS, HEAD_PAD, tm), lambda bi, i: (bi, 0, 0, i)),
                   pl.BlockSpec((1, N_HEADS, tm, HEAD_PAD), lambda bi, i: (bi, 0, i, 0)),
                   pl.BlockSpec((1, N_HEADS, V_ROWS, tm), lambda bi, i: (bi, 0, 0, i))],
        out_shape=[jax.ShapeDtypeStruct((b, l, MIX_GROUP_WIDTH), F32),
                   jax.ShapeDtypeStruct((b, l, MIX_GROUP_WIDTH), F32),
                   jax.ShapeDtypeStruct((b, N_HEADS, HEAD_PAD, l), BF16),
                   jax.ShapeDtypeStruct((b, N_HEADS, l, HEAD_PAD), BF16),
                   jax.ShapeDtypeStruct((b, N_HEADS, V_ROWS, l), BF16)],
        compiler_params=_cparams(("parallel", "parallel")),
        name="in_projection",
    )(x, sh, sc, g, w_in_p, g_q, w_uq_t, g_kv, w_uk_p, w_uv_t, cos_r, sin_r, cos_c, sin_c)


ATTN_HEADS_PER_STEP = 2


def _attn_kernel(*refs, tk, n_chunks):
    if n_chunks:
        q_ref, kc_ref, vc_ref, k_ref, v_ref, o_ref = refs
    else:
        q_ref, kc_ref, vc_ref, o_ref = refs
    tq = q_ref.shape[3]
    heads = range(ATTN_HEADS_PER_STEP)
    qs = [q_ref[0, hh] for hh in heads]

    def step(qt, kc, vt, m, acc):
        st = jnp.dot(kc, qt, preferred_element_type=F32)
        m_new = jnp.maximum(m, jnp.max(st, axis=0, keepdims=True))
        alpha = jnp.exp2(m - m_new)
        pt = jnp.exp2(st - m_new).astype(BF16)
        return m_new, alpha * acc + jnp.dot(vt, pt, preferred_element_type=F32)

    carry = tuple(step(qs[hh], kc_ref[0, hh], vc_ref[0, hh],
                       jnp.full((1, tq), -1e30, F32), jnp.zeros((V_ROWS, tq), F32)) for hh in heads)
    if n_chunks:
        def body(j, carry):
            off = pl.multiple_of(j * tk, tk)
            return tuple(step(qs[hh], k_ref[0, hh, pl.ds(off, tk), :], v_ref[0, hh, :, pl.ds(off, tk)],
                              *carry[hh]) for hh in heads)

        carry = lax.fori_loop(0, n_chunks, body, carry)
    for hh in heads:
        acc = carry[hh][1]
        out = acc[:V_DIM] / acc[V_DIM:V_DIM + 1]
        o_ref[0, hh * V_DIM:(hh + 1) * V_DIM, :] = out.astype(o_ref.dtype)


def _attention(q_t, k_ctx, v_ctx_t, k=None, v_t=None, *, tq, tk):
    b, h, _, l = q_t.shape
    lc = k_ctx.shape[2]
    tq = min(tq, l)
    hps = ATTN_HEADS_PER_STEP
    qspec = pl.BlockSpec((1, hps, HEAD_PAD, tq), lambda bi, hi, i: (bi, hi, 0, i))
    kcspec = pl.BlockSpec((1, hps, lc, HEAD_PAD), lambda bi, hi, i: (bi, hi, 0, 0))
    vcspec = pl.BlockSpec((1, hps, V_ROWS, lc), lambda bi, hi, i: (bi, hi, 0, 0))
    args, specs, n_chunks = [q_t, k_ctx, v_ctx_t], [qspec, kcspec, vcspec], 0
    if k is not None:
        lk = k.shape[2]
        tk = min(tk, lk)
        n_chunks = lk // tk
        once = pl.Buffered(1)
        args += [k, v_t]
        specs += [pl.BlockSpec((1, hps, lk, HEAD_PAD), lambda bi, hi, i: (bi, hi, 0, 0), pipeline_mode=once),
                  pl.BlockSpec((1, hps, V_ROWS, lk), lambda bi, hi, i: (bi, hi, 0, 0), pipeline_mode=once)]
    return pl.pallas_call(
        functools.partial(_attn_kernel, tk=tk, n_chunks=n_chunks),
        grid=(b, h // hps, l // tq),
        in_specs=specs,
        out_specs=pl.BlockSpec((1, hps * V_DIM, tq), lambda bi, hi, i: (bi, hi, i)),
        out_shape=jax.ShapeDtypeStruct((b, h * V_DIM, l), BF16),
        compiler_params=_cparams(("parallel", "parallel", "arbitrary")),
        name="attention",
    )(*args)


def _dft_step1_kernel(x_ref, c_ref, s_ref, yr_ref, yi_ref):
    xb = x_ref[0].astype(BF16)
    yr_ref[0] = jnp.dot(c_ref[...], xb, preferred_element_type=F32)
    yi_ref[0] = jnp.dot(s_ref[...], xb, preferred_element_type=F32)


def _dft_step2_kernel(yr_ref, yi_ref, tc_ref, ts_ref, c_ref, s_ref, cc_ref, sc_ref, o_ref):
    for j in range(SUBLANES):
        yr, yi = yr_ref[0, j], yi_ref[0, j]
        tc, ts = tc_ref[j], ts_ref[j]
        zr = (yr * tc - yi * ts).astype(BF16)
        zi = (yi * tc + yr * ts).astype(BF16)
        a = (jnp.dot(c_ref[...], zr, preferred_element_type=F32)
             - jnp.dot(s_ref[...], zi, preferred_element_type=F32))
        bm = (jnp.dot(s_ref[...], zr, preferred_element_type=F32)
              + jnp.dot(c_ref[...], zi, preferred_element_type=F32))
        o_ref[0, :, j, :] = (jnp.dot(a.astype(BF16), cc_ref[...], preferred_element_type=F32)
                             - jnp.dot(bm.astype(BF16), sc_ref[...], preferred_element_type=F32))


def _dft_dense_kernel(x_ref, c_ref, s_ref, cc_ref, sc_ref, o_ref):
    xb = x_ref[0].astype(BF16)
    a = jnp.dot(c_ref[...], xb, preferred_element_type=F32)
    bm = jnp.dot(s_ref[...], xb, preferred_element_type=F32)
    o_ref[0] = (jnp.dot(a.astype(BF16), cc_ref[...], preferred_element_type=F32)
                - jnp.dot(bm.astype(BF16), sc_ref[...], preferred_element_type=F32))


def _dft_mats(n):
    ang = 2.0 * np.pi * np.outer(np.arange(n), np.arange(n)) / n
    return np.cos(ang), np.sin(ang)


def _channel_dft(l):
    c, s = _dft_mats(GROUP_DIM)
    eye = np.eye(N_GROUPS)
    norm = 1.0 / math.sqrt(l * GROUP_DIM)
    return (jnp.asarray(np.kron(eye, c) * norm, BF16), jnp.asarray(np.kron(eye, s) * norm, BF16))


def _dft_factors(l):
    n1 = 1 << (int(math.log2(l)) // 2)
    return n1, l // n1


def _fourier(u_fno, lane_block=4096):
    b, l, c = u_fno.shape
    cc, sc = _channel_dft(l)
    full = lambda a: pl.BlockSpec(a.shape, lambda *_: (0,) * a.ndim)
    if l <= 512:
        cm, sm = _dft_mats(l)
        cm, sm = jnp.asarray(cm, BF16), jnp.asarray(sm, BF16)
        blk = pl.BlockSpec((1, l, c), lambda bi: (bi, 0, 0))
        return pl.pallas_call(
            _dft_dense_kernel, grid=(b,),
            in_specs=[blk, full(cm), full(sm), full(cc), full(sc)],
            out_specs=blk, out_shape=jax.ShapeDtypeStruct((b, l, c), F32),
            compiler_params=_cparams(("parallel",)), name="dft_dense",
        )(u_fno, cm, sm, cc, sc)
    n1, n2 = _dft_factors(l)
    c1, s1 = _dft_mats(n1)
    c2, s2 = _dft_mats(n2)
    c1, s1, c2, s2 = (jnp.asarray(m, BF16) for m in (c1, s1, c2, s2))
    ang = 2.0 * np.pi * np.outer(np.arange(n1), np.arange(n2)) / l
    tc = jnp.asarray(np.cos(ang)[:, :, None], F32)
    ts = jnp.asarray(np.sin(ang)[:, :, None], F32)
    w = n2 * c
    tn = min(lane_block, w)
    x2 = u_fno.reshape(b, n1, w)
    yspec = pl.BlockSpec((1, n1, tn), lambda bi, i: (bi, 0, i))
    yr, yi = pl.pallas_call(
        _dft_step1_kernel, grid=(b, w // tn),
        in_specs=[yspec, full(c1), full(s1)],
        out_specs=[yspec, yspec],
        out_shape=[jax.ShapeDtypeStruct((b, n1, w), F32)] * 2,
        compiler_params=_cparams(("parallel", "parallel")), name="dft_step1",
    )(x2, c1, s1)
    yr = yr.reshape(b, n1, n2, c)
    yi = yi.reshape(b, n1, n2, c)
    slab = pl.BlockSpec((1, SUBLANES, n2, c), lambda bi, i: (bi, i, 0, 0))
    tw = pl.BlockSpec((SUBLANES, n2, 1), lambda bi, i: (i, 0, 0))
    out = pl.pallas_call(
        _dft_step2_kernel, grid=(b, n1 // SUBLANES),
        in_specs=[slab, slab, tw, tw, full(c2), full(s2), full(cc), full(sc)],
        out_specs=pl.BlockSpec((1, n2, SUBLANES, c), lambda bi, i: (bi, 0, i, 0)),
        out_shape=jax.ShapeDtypeStruct((b, n2, n1, c), F32),
        compiler_params=_cparams(("parallel", "parallel")), name="dft_step2",
    )(yr, yi, tc, ts, c2, s2, cc, sc)
    return out.reshape(b, l, c)


POOL_HALO = SUBLANES


def _mixout_kernel(x_ref, gt_ref, up_ref, prev_ref, next_ref, f_ref, at_ref,
                   wp_ref, ps_ref, wf_ref, wo_ref, o_ref, *, seq_len):
    i = pl.program_id(1)
    tm = x_ref.shape[1]
    u = up_ref[0]
    prev = jnp.where(i > 0, prev_ref[0], 0.0)
    nxt = jnp.where(i < pl.num_programs(1) - 1, next_ref[0], 0.0)
    p = jnp.concatenate([prev, u, nxt], axis=0)
    n = tm + 2 * POOL_HALO
    s1 = pltpu.roll(p, 1, 0) + p
    s2 = pltpu.roll(s1, 1, 0) + pltpu.roll(s1, n - 1, 0)
    s4 = pltpu.roll(s2, 2, 0) + pltpu.roll(s2, n - 2, 0)
    s8 = pltpu.roll(s4, 4, 0) + pltpu.roll(s4, n - 4, 0)
    lane = lax.broadcasted_iota(jnp.int32, (tm, MIX_GROUP_WIDTH), 1)
    grp = lane // GROUP_DIM
    lo, hi = POOL_HALO, POOL_HALO + tm
    win = jnp.where(grp == 0, s1[lo:hi],
                    jnp.where(grp == 1, s2[lo:hi], jnp.where(grp == 2, s4[lo:hi], s8[lo:hi])))
    half = jnp.left_shift(1, grp)
    t = i * tm + lax.broadcasted_iota(jnp.int32, (tm, MIX_GROUP_WIDTH), 0)
    cnt = (jnp.minimum(t + half, seq_len) - jnp.maximum(t - half, 0)).astype(F32)
    dlt = win / cnt - u
    pool_y = jnp.dot(dlt.astype(BF16), wp_ref[...], preferred_element_type=F32) * ps_ref[...]
    fno_y = jnp.dot(f_ref[0].astype(BF16), wf_ref[...], preferred_element_type=F32)
    w = MIX_GROUP_WIDTH
    y = (jnp.dot(pool_y.astype(BF16), wo_ref[0:w, :], preferred_element_type=F32)
         + jnp.dot(fno_y.astype(BF16), wo_ref[w:2 * w, :], preferred_element_type=F32)
         + lax.dot_general(at_ref[0], wo_ref[2 * w:, :], (((0,), (0,)), ((), ())),
                           preferred_element_type=F32))
    o_ref[0] = x_ref[0] + gt_ref[0] * y


def _mixer_output(x, gt, u_pool, f, attn, wp_bd, pool_scale, wf_bd, w_out, tm):
    b, l, d = x.shape
    tm = min(tm, l)
    nb = tm // POOL_HALO
    last = l // POOL_HALO - 1
    full = lambda a: pl.BlockSpec(a.shape, lambda bi, i: (0,) * a.ndim)
    w = MIX_GROUP_WIDTH
    return pl.pallas_call(
        functools.partial(_mixout_kernel, seq_len=l),
        grid=(b, l // tm),
        in_specs=[pl.BlockSpec((1, tm, d), lambda bi, i: (bi, i, 0)),
                  pl.BlockSpec((1, 1, d), lambda bi, i: (bi, 0, 0)),
                  pl.BlockSpec((1, tm, w), lambda bi, i: (bi, i, 0)),
                  pl.BlockSpec((1, POOL_HALO, w), lambda bi, i: (bi, jnp.maximum(i * nb - 1, 0), 0)),
                  pl.BlockSpec((1, POOL_HALO, w), lambda bi, i: (bi, jnp.minimum((i + 1) * nb, last), 0)),
                  pl.BlockSpec((1, tm, w), lambda bi, i: (bi, i, 0)),
                  pl.BlockSpec((1, attn.shape[1], tm), lambda bi, i: (bi, 0, i)),
                  full(wp_bd), full(pool_scale), full(wf_bd), full(w_out)],
        out_specs=pl.BlockSpec((1, tm, d), lambda bi, i: (bi, i, 0)),
        out_shape=jax.ShapeDtypeStruct((b, l, d), F32),
        compiler_params=_cparams(("parallel", "arbitrary")),
        name="mixer_output",
    )(x, gt, u_pool, u_pool, u_pool, f, attn, wp_bd, pool_scale, wf_bd, w_out)


def _first_index_of_max(vals, iota, n):
    mx = jnp.max(vals, axis=0, keepdims=True)
    ix = jnp.min(jnp.where(vals == mx, iota, n), axis=0, keepdims=True)
    return mx, ix


def _ffn_front_kernel(*refs, n_lat, has_ctx):
    if has_ctx:
        x_ref, xc_ref = refs[:2]
        refs = refs[2:]
    else:
        x_ref, refs = refs[0], refs[1:]
    (sh_ref, sc_ref, gt_ref, g_ref, wr_ref, rb_ref, wsg_ref, wsu_ref, wsd_ref,
     xo_ref, xs_ref, slot_ref, wt_ref, cnt_ref) = refs
    x = x_ref[0]
    if has_ctx:
        x = jnp.where(pl.program_id(0) < n_lat, x, xc_ref[0])
    tm = x.shape[0]
    h = _rms(x, g_ref[...]) * (1.0 + sc_ref[0]) + sh_ref[0]

    logits = lax.dot_general(wr_ref[...], h, (((1,), (1,)), ((), ())),
                             preferred_element_type=F32, precision=HIGHEST)
    aff = jax.nn.sigmoid(logits)
    sel = aff + rb_ref[...]
    e_iota = lax.broadcasted_iota(jnp.int32, (N_EXPERTS, tm), 0).astype(F32)
    neg = jnp.float32(-jnp.inf)
    gscores = []
    for g in range(N_EXPERT_GROUPS):
        blk = sel[g * EXPERTS_PER_GROUP:(g + 1) * EXPERTS_PER_GROUP]
        it = lax.broadcasted_iota(jnp.int32, blk.shape, 0).astype(F32)
        m1, i1 = _first_index_of_max(blk, it, float(EXPERTS_PER_GROUP))
        m2 = jnp.max(jnp.where(it == i1, neg, blk), axis=0, keepdims=True)
        gscores.append(m1 + m2)
    gs = jnp.concatenate(gscores, axis=0)
    g_iota = lax.broadcasted_iota(jnp.int32, gs.shape, 0).astype(F32)
    gself = jnp.zeros(gs.shape, F32)
    for _ in range(TOPK_GROUPS):
        _, ig = _first_index_of_max(gs, g_iota, float(N_EXPERT_GROUPS))
        hit = g_iota == ig
        gself = jnp.where(hit, 1.0, gself)
        gs = jnp.where(hit, neg, gs)
    emask = jnp.concatenate(
        [jnp.broadcast_to(gself[g:g + 1], (EXPERTS_PER_GROUP, tm)) for g in range(N_EXPERT_GROUPS)], axis=0)
    masked = jnp.where(emask > 0.5, sel, neg)
    hits, wts = [], []
    chosen = jnp.zeros((N_EXPERTS, tm), F32)
    for _ in range(TOP_K):
        _, ie = _first_index_of_max(masked, e_iota, float(N_EXPERTS))
        hit = e_iota == ie
        hits.append(hit)
        wts.append(jnp.sum(jnp.where(hit, aff, 0.0), axis=0, keepdims=True))
        chosen = jnp.where(hit, 1.0, chosen)
        masked = jnp.where(hit, neg, masked)
    wsum = wts[0]
    for w in wts[1:]:
        wsum = wsum + w
    scale = ROUTED_SCALE / wsum
    wt_ref[...] = jnp.concatenate([w * scale for w in wts], axis=0)

    r_io = lax.broadcasted_iota(jnp.int32, (tm, tm), 0)
    c_io = lax.broadcasted_iota(jnp.int32, (tm, tm), 1)
    tri = (r_io <= c_io).astype(BF16)
    incl = jnp.dot(chosen.astype(BF16), tri, preferred_element_type=F32)
    count = jnp.sum(chosen, axis=1, keepdims=True)
    n_pieces = jnp.floor((count + (PIECE - 1)) * (1.0 / PIECE))
    cnt_ref[0] = n_pieces
    e_r = lax.broadcasted_iota(jnp.int32, (N_EXPERTS, N_EXPERTS), 0)
    e_c = lax.broadcasted_iota(jnp.int32, (N_EXPERTS, N_EXPERTS), 1)
    before = (e_c < e_r).astype(BF16)
    piece_off = jnp.dot(before, jnp.broadcast_to(n_pieces, (N_EXPERTS, LANES)).astype(BF16),
                        preferred_element_type=F32)[:, :1]
    pos = piece_off * PIECE + incl - chosen
    slots = [jnp.sum(jnp.where(hit, pos, 0.0), axis=0, keepdims=True) for hit in hits]
    slot_ref[...] = jnp.concatenate(slots, axis=0).astype(jnp.int32)

    row_io = lax.broadcasted_iota(jnp.int32, (TILE_ROWS, tm), 0).astype(F32)
    onehot = jnp.zeros((TILE_ROWS, tm), F32)
    keep = (pl.program_id(0) < pl.num_programs(0) - 1).astype(F32)
    for s in slots:
        onehot = jnp.where(row_io == s, keep, onehot)
    onehot = onehot.astype(BF16)
    hb = h.astype(BF16)
    d = hb.shape[1]
    nc = 2 * LANES
    for j in range(d // nc):
        xs_ref[0, :, j * nc:(j + 1) * nc] = jnp.dot(
            onehot, hb[:, j * nc:(j + 1) * nc], preferred_element_type=F32).astype(BF16)

    a = jnp.dot(hb, wsg_ref[...], preferred_element_type=F32)
    bb = jnp.dot(hb, wsu_ref[...], preferred_element_type=F32)
    sh_out = jnp.dot((_silu(a) * bb).astype(BF16), wsd_ref[...], preferred_element_type=F32)
    xo_ref[0] = x + gt_ref[0] * sh_out


def _ffn_front(x, xc, sh, sc, gt, g, w_rt, rb, wsg, wsu, wsd):
    b, l, d = x.shape
    tm = MOE_TILE
    nt = l // tm
    n_lat = b * nt
    has_ctx = xc is not None
    ntc = xc.shape[1] // tm if has_ctx else 0
    n_all = n_lat + b * ntc
    full = lambda a: pl.BlockSpec(a.shape, lambda i: (0,) * a.ndim)

    real = lambda i: jnp.minimum(i, n_all - 1)

    def lat_map(i):
        j = jnp.minimum(i, n_lat - 1)
        return (j // nt, j % nt, 0)

    def ctx_map(i):
        j = jnp.clip(i - n_lat, 0, b * ntc - 1)
        return (j // ntc, j % ntc, 0)

    vec = pl.BlockSpec((1, 1, d), lambda i: (jnp.where(real(i) < n_lat, real(i) // nt, b), 0, 0))
    xspecs = [pl.BlockSpec((1, tm, d), lat_map)] + ([pl.BlockSpec((1, tm, d), ctx_map)] if has_ctx else [])
    xargs = [x] + ([xc] if has_ctx else [])
    tokmap = lambda i: (0, real(i))
    return pl.pallas_call(
        functools.partial(_ffn_front_kernel, n_lat=n_lat, has_ctx=has_ctx),
        grid=(n_all + 1,),
        in_specs=xspecs + [vec, vec, vec, full(g), full(w_rt), full(rb), full(wsg), full(wsu), full(wsd)],
        out_specs=[pl.BlockSpec((1, tm, d), lambda i: (real(i), 0, 0)),
                   pl.BlockSpec((1, TILE_ROWS, d), lambda i: (i, 0, 0)),
                   pl.BlockSpec((TOP_K, tm), tokmap), pl.BlockSpec((TOP_K, tm), tokmap),
                   pl.BlockSpec((1, N_EXPERTS, 1), lambda i: (real(i), 0, 0))],
        out_shape=[jax.ShapeDtypeStruct((n_all, tm, d), F32),
                   jax.ShapeDtypeStruct((n_all + 1, TILE_ROWS, d), BF16),
                   jax.ShapeDtypeStruct((TOP_K, n_all * tm), jnp.int32),
                   jax.ShapeDtypeStruct((TOP_K, n_all * tm), F32),
                   jax.ShapeDtypeStruct((n_all, N_EXPERTS, 1), F32)],
        compiler_params=_cparams(("arbitrary",)),
        name="ffn_front",
    )(*xargs, sh, sc, gt, g, w_rt, rb, wsg, wsu, wsd)


def _expert_kernel(bexp_ref, nused_ref, rlen_ref, rsrc_ref, rnext_ref, xs_ref, wgu_ref, wd_ref, y_ref,
                   state, dst, xbuf, ybuf, sem_in, sem_out, *, runs_per_expert, spare_piece):
    i = pl.program_id(0)
    n_used = nused_ref[0]
    end_run = rlen_ref.shape[0] - 1

    def copy_in(pid, slot, p):
        return pltpu.make_async_copy(xs_ref.at[pid], xbuf.at[slot, pl.ds(p * PIECE, PIECE), :], sem_in.at[slot])

    def copy_out(pid, slot, p):
        return pltpu.make_async_copy(ybuf.at[slot, pl.ds(p * PIECE, PIECE), :], y_ref.at[pid], sem_out.at[slot])

    def issue_gather(blk, slot):
        r_lo = bexp_ref[blk] * runs_per_expert
        r_end = r_lo + runs_per_expert
        fresh = state[0] < r_lo
        r = jnp.where(fresh, r_lo, state[0])
        o = jnp.where(fresh, 0, state[1])
        read_pid = jnp.int32(0)
        for p in range(BLOCK_PIECES):
            exhausted = o >= rlen_ref[jnp.minimum(r, end_run)]
            r = jnp.where(exhausted, rnext_ref[jnp.minimum(r + 1, end_run)], r)
            o = jnp.where(exhausted, 0, o)
            valid = r < r_end
            pid = rsrc_ref[jnp.minimum(r, end_run)] + o
            read_pid = jnp.where(valid, pid, read_pid)
            dst[slot, p] = jnp.where(valid, pid, spare_piece + slot * BLOCK_PIECES + p)
            copy_in(read_pid, slot, p).start()
            o = o + valid.astype(jnp.int32)
        state[0] = r
        state[1] = o

    def wait_gather(slot):
        pltpu.make_async_copy(xbuf.at[slot], xbuf.at[slot], sem_in.at[slot]).wait()

    def send_results(slot):
        for p in range(BLOCK_PIECES):
            copy_out(dst[slot, p], slot, p).start()

    def wait_results(slot):
        pltpu.make_async_copy(ybuf.at[slot], ybuf.at[slot], sem_out.at[slot]).wait()

    @pl.when(i < n_used)
    def _():
        slot = i % 2

        @pl.when(i == 0)
        def _():
            state[0] = 0
            state[1] = 0
            issue_gather(0, 0)

        @pl.when(i + 1 < n_used)
        def _():
            issue_gather(i + 1, (i + 1) % 2)

        wait_gather(slot)

        @pl.when(i >= 2)
        def _():
            wait_results(slot)

        gu = jnp.dot(xbuf[slot], wgu_ref[0], preferred_element_type=F32)
        f = gu.shape[1] // 2
        act = (_silu(gu[:, :f]) * gu[:, f:]).astype(BF16)
        ybuf[slot] = jnp.dot(act, wd_ref[0], preferred_element_type=F32).astype(BF16)
        send_results(slot)

        @pl.when(i == n_used - 1)
        def _():
            wait_results(slot)

            @pl.when(i >= 1)
            def _():
                wait_results(1 - slot)


def _experts(block_expert, n_used, run_len, run_src, run_next, xs_pieces, w_gu, w_down, runs_per_expert):
    n_blocks = block_expert.shape[0]
    _, _, d = xs_pieces.shape
    _, _, f2 = w_gu.shape
    grid_spec = pltpu.PrefetchScalarGridSpec(
        num_scalar_prefetch=5,
        grid=(n_blocks,),
        in_specs=[pl.BlockSpec(memory_space=pl.ANY),
                  pl.BlockSpec((1, d, f2), lambda i, be, *_: (be[i], 0, 0)),
                  pl.BlockSpec((1, f2 // 2, d), lambda i, be, *_: (be[i], 0, 0))],
        out_specs=pl.BlockSpec(memory_space=pl.ANY),
        scratch_shapes=[pltpu.SMEM((2,), jnp.int32),
                        pltpu.SMEM((2, BLOCK_PIECES), jnp.int32),
                        pltpu.VMEM((2, EXPERT_BLOCK, d), BF16),
                        pltpu.VMEM((2, EXPERT_BLOCK, d), BF16),
                        pltpu.SemaphoreType.DMA((2,)), pltpu.SemaphoreType.DMA((2,))],
    )
    return pl.pallas_call(
        functools.partial(_expert_kernel, runs_per_expert=runs_per_expert,
                          spare_piece=runs_per_expert * PIECES_PER_TILE),
        grid_spec=grid_spec,
        out_shape=jax.ShapeDtypeStruct(xs_pieces.shape, xs_pieces.dtype),
        input_output_aliases={5: 0},
        compiler_params=_cparams(("arbitrary",)),
        name="experts",
    )(block_expert, n_used, run_len, run_src, run_next, xs_pieces, w_gu, w_down)


def _combine_kernel(y_ref, slot_ref, w_ref, x_ref, gt_ref, gf_ref, o_ref, *, final):
    tm = x_ref.shape[1]
    slot = slot_ref[...].astype(F32)
    w = w_ref[...]
    col_io = lax.broadcasted_iota(jnp.int32, (tm, TILE_ROWS), 1).astype(F32)
    sel = jnp.zeros((tm, TILE_ROWS), F32)
    for k in range(TOP_K):
        sel = jnp.where(col_io == slot[:, k:k + 1], w[:, k:k + 1], sel)
    routed = jnp.dot(sel.astype(BF16), y_ref[0], preferred_element_type=F32)
    out = x_ref[0] + gt_ref[0] * routed
    if final:
        out = _rms(out, gf_ref[...])
    o_ref[0] = out


def _combine(y_tiles, slot_tok, w_tok, x_tiles, gt, g_final, seq_shape, tile_off, final):
    b, l = seq_shape
    _, tm, d = x_tiles.shape
    nt = l // tm
    tile = lambda bi, i: tile_off + bi * nt + i
    return pl.pallas_call(
        functools.partial(_combine_kernel, final=final),
        grid=(b, nt),
        in_specs=[pl.BlockSpec((1, TILE_ROWS, d), lambda bi, i: (tile(bi, i), 0, 0)),
                  pl.BlockSpec((tm, TOP_K), lambda bi, i: (tile(bi, i), 0)),
                  pl.BlockSpec((tm, TOP_K), lambda bi, i: (tile(bi, i), 0)),
                  pl.BlockSpec((1, tm, d), lambda bi, i: (tile(bi, i), 0, 0)),
                  pl.BlockSpec((1, 1, d), lambda bi, i: (bi, 0, 0)),
                  pl.BlockSpec(g_final.shape, lambda bi, i: (0, 0))],
        out_specs=pl.BlockSpec((1, tm, d), lambda bi, i: (bi, i, 0)),
        out_shape=jax.ShapeDtypeStruct((b, l, d), F32),
        compiler_params=_cparams(("parallel", "parallel")),
        name="combine",
    )(y_tiles, slot_tok, w_tok, x_tiles, gt, g_final)


def _rope_tables(l):
    rows = l // GRID_W
    row = jnp.repeat(jnp.arange(rows, dtype=F32), GRID_W)
    col = jnp.tile(jnp.arange(GRID_W, dtype=F32), rows)
    inv_freq = ROPE_THETA ** (-jnp.arange(ROPE_FREQS, dtype=F32) * 2.0 / (2 * ROPE_FREQS))
    ar, ac = row[:, None] * inv_freq, col[:, None] * inv_freq
    ones = jnp.ones((l, ROPE_LANE0), F32)
    zpad = jnp.zeros((l, HEAD_PAD - ROPE_LANE0 - QK_ROPE), F32)
    cos_t = jnp.concatenate([ones, jnp.cos(ar), jnp.cos(ar), jnp.cos(ac), jnp.cos(ac), zpad], axis=1)
    sin_t = jnp.concatenate([0.0 * ones, -jnp.sin(ar), jnp.sin(ar), -jnp.sin(ac), jnp.sin(ac), zpad], axis=1)
    return cos_t, sin_t


def _identity_tables(l):
    lane = jnp.arange(HEAD_PAD)
    cos_t = jnp.broadcast_to((lane < ROPE_LANE0 + QK_ROPE).astype(F32), (l, HEAD_PAD))
    return cos_t, jnp.zeros((l, HEAD_PAD), F32)


def _block_diag(w):
    g, a, b = w.shape
    out = jnp.zeros((g * a, g * b), w.dtype)
    for i in range(g):
        out = out.at[i * a:(i + 1) * a, i * b:(i + 1) * b].set(w[i])
    return out


def _prep_layer(w_in, w_pool, w_fno, w_uq, w_ukv):
    d = w_in.shape[0]
    kr = jnp.zeros((d, HEAD_PAD), w_in.dtype).at[:, ROPE_LANE0:ROPE_LANE0 + QK_ROPE].set(w_in[:, COL_KR:])
    w_in_p = jnp.concatenate([w_in[:, :COL_KR], kr], axis=1).astype(BF16)
    qk = QK_NOPE + QK_ROPE
    wq = w_uq.reshape(Q_LORA, N_HEADS, qk)
    wq = jnp.pad(wq, ((0, 0), (0, 0), (0, HEAD_PAD - qk))).reshape(Q_LORA, N_HEADS * HEAD_PAD)
    wkv = w_ukv.reshape(KV_LORA, N_HEADS, QK_NOPE + V_DIM)
    wk = jnp.pad(wkv[..., :QK_NOPE], ((0, 0), (0, 0), (0, HEAD_PAD - QK_NOPE))).reshape(KV_LORA, -1)
    wv = jnp.pad(wkv[..., QK_NOPE:], ((0, 0), (0, 0), (0, V_ROWS - V_DIM))).reshape(KV_LORA, -1)
    return (w_in_p, wq.T.astype(BF16), wk.astype(BF16), wv.T.astype(BF16),
            _block_diag(w_pool).astype(BF16), _block_diag(w_fno).astype(BF16))


def _run_tables(n_pieces):
    n_all = n_pieces.shape[0]
    npc = n_pieces.astype(jnp.int32)
    piece_off = jnp.cumsum(npc, axis=1) - npc
    run_src = (jnp.arange(n_all, dtype=jnp.int32)[:, None] * PIECES_PER_TILE + piece_off).T.reshape(-1)
    run_len = npc.T.reshape(-1)
    n_runs = run_len.shape[0]
    run_id = jnp.arange(n_runs, dtype=jnp.int32)
    run_next = lax.cummin(jnp.where(run_len > 0, run_id, n_runs), axis=0, reverse=True)
    pad1 = lambda a, v: jnp.concatenate([a, jnp.full((1,), v, jnp.int32)])
    run_len, run_src, run_next = pad1(run_len, 0), pad1(run_src, 0), pad1(run_next, n_runs)
    blocks = (jnp.sum(npc, axis=0) + BLOCK_PIECES - 1) // BLOCK_PIECES
    block_end = jnp.cumsum(blocks)
    max_blocks = n_all * PIECES_PER_TILE // BLOCK_PIECES + N_EXPERTS
    blk = jnp.arange(max_blocks, dtype=jnp.int32)
    block_expert = jnp.minimum(jnp.sum((block_end[None, :] <= blk[:, None]).astype(jnp.int32), axis=1),
                               N_EXPERTS - 1)
    return block_expert, block_end[-1:], run_len, run_src, run_next


TM_PROJ = 256
TM_MIX = 256
ATTN_TQ = 2048
ATTN_TK = 1024


def kernel(x, c, ctx, c_ctx, w_mod, b_mod, g_mix, g_ffn, w_in, w_pool, pool_scale, w_fno, g_q, w_uq,
           g_kv, w_ukv, w_out, w_router, router_bias, w_gate, w_up, w_down, w_sh_gate, w_sh_up,
           w_sh_down, g_final):
    b, l, d = x.shape
    lc = ctx.shape[1]
    depth = w_mod.shape[0]
    cc = jnp.zeros((SUBLANES, d), F32).at[:b].set(c).at[b].set(c_ctx)
    mods = _modulation(cc, w_mod, b_mod)
    with_transposed = lambda cs: (cs[0], cs[1], cs[0].T, cs[1].T)
    tab_lat = with_transposed(_rope_tables(l))
    tab_ctx = with_transposed(_identity_tables(lc))
    xc = ctx
    row = lambda v: v.reshape(1, -1)
    for li in range(depth):
        last = li == depth - 1
        m = mods[li].reshape(SUBLANES, N_MOD, d)
        lat = [m[:b, j][:, None, :] for j in range(N_MOD)]
        cm = [jnp.broadcast_to(m[b, j][None, None, :], (b, 1, d)) for j in range(N_MOD)]
        w_in_p, wq_t, wk_p, wv_t, wp_bd, wf_bd = _prep_layer(w_in[li], w_pool[li], w_fno[li], w_uq[li], w_ukv[li])
        w_out_b = w_out[li].astype(BF16)
        gq, gkv, gm = row(g_q[li]), row(g_kv[li]), row(g_mix[li])

        up, uf, q, k, v = _in_projection(x, lat[0], lat[1], gm, w_in_p, gq, wq_t, gkv, wk_p, wv_t,
                                         tab_lat, TM_PROJ)
        upc, ufc, qc, kc, vc = _in_projection(xc, cm[0], cm[1], gm, w_in_p, gq, wq_t, gkv, wk_p, wv_t,
                                              tab_ctx, TM_PROJ)
        attn = _attention(q, kc, vc, k, v, tq=ATTN_TQ, tk=ATTN_TK)
        f = _fourier(uf)
        ps = row(pool_scale[li])
        x = _mixer_output(x, lat[2], up, f, attn, wp_bd, ps, wf_bd, w_out_b, TM_MIX)
        if not last:
            attn_c = _attention(qc, kc, vc, tq=ATTN_TQ, tk=ATTN_TK)
            fc = _fourier(ufc)
            xc = _mixer_output(xc, cm[2], upc, fc, attn_c, wp_bd, ps, wf_bd, w_out_b, TM_MIX)

        w_rt = w_router[li].T
        rb = router_bias[li].reshape(N_EXPERTS, 1)
        wsg, wsu, wsd = w_sh_gate[li].astype(BF16), w_sh_up[li].astype(BF16), w_sh_down[li].astype(BF16)
        with_ctx = lambda j: jnp.concatenate([lat[j], cm[j][:1]], axis=0)
        x_tiles, xs, slot_t, w_t, n_pieces = _ffn_front(
            x, None if last else xc, with_ctx(3), with_ctx(4), with_ctx(5), row(g_ffn[li]), w_rt, rb, wsg, wsu, wsd)
        n_all = xs.shape[0] - 1
        block_expert, n_used, run_len, run_src, run_next = _run_tables(n_pieces[..., 0])
        w_gu = jnp.concatenate([w_gate[li], w_up[li]], axis=-1).astype(BF16)
        y = _experts(block_expert, n_used, run_len, run_src, run_next,
                     xs.reshape((n_all + 1) * PIECES_PER_TILE, PIECE, d), w_gu, w_down[li].astype(BF16), n_all)
        y = y.reshape(n_all + 1, TILE_ROWS, d)
        slot_tok, w_tok = slot_t.T, w_t.T
        x = _combine(y, slot_tok, w_tok, x_tiles, lat[5], row(g_final), (b, l), 0, last)
        if not last:
            xc = _combine(y, slot_tok, w_tok, x_tiles, cm[5], row(g_final), (b, lc), b * l // MOE_TILE, False)
    return x
```

```python
import functools
import math

import jax
import jax.numpy as jnp
import numpy as np
from jax import lax
from jax.experimental import pallas as pl
from jax.experimental.pallas import tpu as pltpu

F32 = jnp.float32
BF16 = jnp.bfloat16
HIGHEST = lax.Precision.HIGHEST

EPS = 1e-6
N_MOD = 6
GRID_W = 64
POOL_WINDOWS = (2, 4, 8, 16)
GROUP_DIM = 64
N_GROUPS = 4
MIX_GROUP_WIDTH = N_GROUPS * GROUP_DIM
N_HEADS = 8
QK_NOPE = 64
QK_ROPE = 32
V_DIM = 64
Q_LORA = 384
KV_LORA = 256
ROPE_FREQS = QK_ROPE // 4
ROPE_THETA = 10000.0
SOFTMAX_SCALE = (QK_NOPE + QK_ROPE) ** -0.5
N_EXPERTS = 64
TOP_K = 8
N_EXPERT_GROUPS = 8
TOPK_GROUPS = 4
EXPERTS_PER_GROUP = N_EXPERTS // N_EXPERT_GROUPS
ROUTED_SCALE = 2.5

LANES = 128
SUBLANES = 8
HEAD_PAD = LANES
VMEM_LIMIT = 56 * 1024 * 1024

COL_FNO = MIX_GROUP_WIDTH
COL_Q = COL_FNO + MIX_GROUP_WIDTH
COL_KV = COL_Q + Q_LORA
COL_KR = COL_KV + KV_LORA
IN_PAD = COL_KR + HEAD_PAD
ROPE_LANE0 = QK_NOPE

MOE_TILE = 256
PIECE = 16
PIECES_PER_TILE = MOE_TILE * TOP_K // PIECE + N_EXPERTS
TILE_ROWS = PIECES_PER_TILE * PIECE
BLOCK_PIECES = 32
EXPERT_BLOCK = BLOCK_PIECES * PIECE


def _cparams(sem, vmem=VMEM_LIMIT):
    return pltpu.CompilerParams(dimension_semantics=sem, vmem_limit_bytes=vmem)


def _rms(x, g):
    return x * lax.rsqrt(jnp.mean(x * x, axis=-1, keepdims=True) + EPS) * g


def _silu(x):
    return x * jax.nn.sigmoid(x)


def _mod_kernel(c_ref, w_ref, b_ref, o_ref):
    s = _silu(c_ref[...])
    o_ref[0] = jnp.dot(s, w_ref[0], preferred_element_type=F32, precision=HIGHEST) + b_ref[0]


def _modulation(cc, w_mod, b_mod):
    depth, d, nd = w_mod.shape
    n_chunks = nd // d
    return pl.pallas_call(
        _mod_kernel,
        grid=(depth, n_chunks),
        in_specs=[pl.BlockSpec((SUBLANES, d), lambda l, j: (0, 0)),
                  pl.BlockSpec((1, d, d), lambda l, j: (l, 0, j)),
                  pl.BlockSpec((1, 1, d), lambda l, j: (l, 0, j))],
        out_specs=pl.BlockSpec((1, SUBLANES, d), lambda l, j: (l, 0, j)),
        out_shape=jax.ShapeDtypeStruct((depth, SUBLANES, nd), F32),
        compiler_params=_cparams(("parallel", "parallel")),
        name="modulation",
    )(cc, w_mod, b_mod.reshape(depth, 1, nd))


def _rope(t, c, s):
    lane = lax.broadcasted_iota(jnp.int32, t.shape, 1)
    first_half = (lane % (2 * ROPE_FREQS)) < ROPE_FREQS
    partner = jnp.where(first_half, pltpu.roll(t, LANES - ROPE_FREQS, 1), pltpu.roll(t, ROPE_FREQS, 1))
    return t * c + partner * s


def _rope_rows(t, c, s):
    rw = lax.broadcasted_iota(jnp.int32, t.shape, 0)
    first_half = (rw % (2 * ROPE_FREQS)) < ROPE_FREQS
    n = t.shape[0]
    partner = jnp.where(first_half, pltpu.roll(t, n - ROPE_FREQS, 0), pltpu.roll(t, ROPE_FREQS, 0))
    return t * c + partner * s


V_ROWS = V_DIM + 16


def _inproj_kernel(x_ref, sh_ref, sc_ref, g_ref, win_ref, gq_ref, wuqt_ref, gkv_ref, wuk_ref, wuvt_ref,
                   cos_ref, sin_ref, cost_ref, sint_ref, up_ref, uf_ref, q_ref, k_ref, v_ref):
    x = x_ref[0]
    h = _rms(x, g_ref[...]) * (1.0 + sc_ref[0]) + sh_ref[0]
    u = jnp.dot(h.astype(BF16), win_ref[...], preferred_element_type=F32)
    up_ref[0] = u[:, :COL_FNO]
    uf_ref[0] = u[:, COL_FNO:COL_Q]
    nt = (((1,), (1,)), ((), ()))
    cq = _rms(u[:, COL_Q:COL_KV], gq_ref[...]).astype(BF16)
    qt = lax.dot_general(wuqt_ref[...], cq, nt, preferred_element_type=F32)
    ckv = _rms(u[:, COL_KV:COL_KR], gkv_ref[...]).astype(BF16)
    kn = jnp.dot(ckv, wuk_ref[...], preferred_element_type=F32)
    vt = lax.dot_general(wuvt_ref[...], ckv, nt, preferred_element_type=F32)
    kr = _rope(u[:, COL_KR:IN_PAD], cos_ref[...], sin_ref[...])
    cos_t, sin_t = cost_ref[...], sint_ref[...]
    tm = x.shape[0]
    ones_row = (lax.broadcasted_iota(jnp.int32, (V_ROWS, tm), 0) == V_DIM).astype(F32)
    q_scale = SOFTMAX_SCALE * math.log2(math.e)
    for hd in range(N_HEADS):
        lo = hd * HEAD_PAD
        q_ref[0, hd] = (_rope_rows(qt[lo:lo + HEAD_PAD], cos_t, sin_t) * q_scale).astype(BF16)
        k_ref[0, hd] = (kn[:, lo:lo + HEAD_PAD] + kr).astype(BF16)
        v_ref[0, hd] = (vt[hd * V_ROWS:(hd + 1) * V_ROWS] + ones_row).astype(BF16)


def _in_projection(x, sh, sc, g, w_in_p, g_q, w_uq_t, g_kv, w_uk_p, w_uv_t, tables, tm):
    b, l, d = x.shape
    tm = min(tm, l)
    cos_r, sin_r, cos_c, sin_c = tables
    full = lambda a: pl.BlockSpec(a.shape, lambda bi, i: (0,) * a.ndim)
    vec = pl.BlockSpec((1, 1, d), lambda bi, i: (bi, 0, 0))
    rtab = pl.BlockSpec((tm, HEAD_PAD), lambda bi, i: (i, 0))
    ctab = pl.BlockSpec((HEAD_PAD, tm), lambda bi, i: (0, i))
    return pl.pallas_call(
        _inproj_kernel,
        grid=(b, l // tm),
        in_specs=[pl.BlockSpec((1, tm, d), lambda bi, i: (bi, i, 0)), vec, vec,
                  full(g), full(w_in_p), full(g_q), full(w_uq_t), full(g_kv), full(w_uk_p), full(w_uv_t),
                  rtab, rtab, ctab, ctab],
        out_specs=[pl.BlockSpec((1, tm, MIX_GROUP_WIDTH), lambda bi, i: (bi, i, 0)),
                   pl.BlockSpec((1, tm, MIX_GROUP_WIDTH), lambda bi, i: (bi, i, 0)),
                   pl.BlockSpec((1, N_HEADS, HEAD_PAD, tm), lambda bi, i: (bi, 0, 0, i)),
                   pl.BlockSpec((1, N_HEADS, tm, HEAD_PAD), lambda bi, i: (bi, 0, i, 0)),
                   pl.BlockSpec((1, N_HEADS, V_ROWS, tm), lambda bi, i: (bi, 0, 0, i))],
        out_shape=[jax.ShapeDtypeStruct((b, l, MIX_GROUP_WIDTH), F32),
                   jax.ShapeDtypeStruct((b, l, MIX_GROUP_WIDTH), F32),
                   jax.ShapeDtypeStruct((b, N_HEADS, HEAD_PAD, l), BF16),
                   jax.ShapeDtypeStruct((b, N_HEADS, l, HEAD_PAD), BF16),
                   jax.ShapeDtypeStruct((b, N_HEADS, V_ROWS, l), BF16)],
        compiler_params=_cparams(("parallel", "parallel")),
        name="in_projection",
    )(x, sh, sc, g, w_in_p, g_q, w_uq_t, g_kv, w_uk_p, w_uv_t, cos_r, sin_r, cos_c, sin_c)


ATTN_HEADS_PER_STEP = 2


def _attn_kernel(*refs, tk, n_chunks):
    if n_chunks:
        q_ref, kc_ref, vc_ref, k_ref, v_ref, o_ref, s_scr = refs
    else:
        q_ref, kc_ref, vc_ref, o_ref = refs
    tq = q_ref.shape[3]
    heads = range(ATTN_HEADS_PER_STEP)
    qs = [q_ref[0, hh] for hh in heads]

    def scores(hh, kc):
        return jnp.dot(kc, qs[hh], preferred_element_type=F32)

    def absorb(st, vt, m, acc):
        m_new = jnp.maximum(m, jnp.max(st, axis=0, keepdims=True))
        alpha = jnp.exp2(m - m_new)
        pt = jnp.exp2(st - m_new).astype(BF16)
        return m_new, alpha * acc + jnp.dot(vt, pt, preferred_element_type=F32)

    carry = tuple(absorb(scores(hh, kc_ref[0, hh]), vc_ref[0, hh],
                         jnp.full((1, tq), -1e30, F32), jnp.zeros((V_ROWS, tq), F32)) for hh in heads)
    if n_chunks:
        def produce(slot, chunk):
            off = pl.multiple_of(chunk * tk, tk)
            for hh in heads:
                s_scr[slot, hh] = scores(hh, k_ref[0, hh, pl.ds(off, tk), :])

        def consume(slot, chunk, carry):
            off = pl.multiple_of(chunk * tk, tk)
            return tuple(absorb(s_scr[slot, hh], v_ref[0, hh, :, pl.ds(off, tk)], *carry[hh]) for hh in heads)

        produce(0, 0)

        def body(j, carry):
            a = 2 * j
            produce(1, a + 1)
            carry = consume(0, a, carry)
            produce(0, jnp.minimum(a + 2, n_chunks - 1))
            return consume(1, a + 1, carry)

        carry = lax.fori_loop(0, n_chunks // 2, body, carry)
    for hh in heads:
        acc = carry[hh][1]
        out = acc[:V_DIM] / acc[V_DIM:V_DIM + 1]
        o_ref[0, hh * V_DIM:(hh + 1) * V_DIM, :] = out.astype(o_ref.dtype)


def _attention(q_t, k_ctx, v_ctx_t, k=None, v_t=None, *, tq, tk):
    b, h, _, l = q_t.shape
    lc = k_ctx.shape[2]
    tq = min(tq, l)
    hps = ATTN_HEADS_PER_STEP
    qspec = pl.BlockSpec((1, hps, HEAD_PAD, tq), lambda bi, hi, i: (bi, hi, 0, i))
    kcspec = pl.BlockSpec((1, hps, lc, HEAD_PAD), lambda bi, hi, i: (bi, hi, 0, 0))
    vcspec = pl.BlockSpec((1, hps, V_ROWS, lc), lambda bi, hi, i: (bi, hi, 0, 0))
    args, specs, n_chunks, scratch = [q_t, k_ctx, v_ctx_t], [qspec, kcspec, vcspec], 0, []
    if k is not None:
        lk = k.shape[2]
        tk = min(tk, lk // 2)
        n_chunks = lk // tk
        assert n_chunks % 2 == 0 and n_chunks * tk == lk
        scratch = [pltpu.VMEM((2, hps, tk, tq), F32)]
        once = pl.Buffered(1)
        args += [k, v_t]
        specs += [pl.BlockSpec((1, hps, lk, HEAD_PAD), lambda bi, hi, i: (bi, hi, 0, 0), pipeline_mode=once),
                  pl.BlockSpec((1, hps, V_ROWS, lk), lambda bi, hi, i: (bi, hi, 0, 0), pipeline_mode=once)]
    return pl.pallas_call(
        functools.partial(_attn_kernel, tk=tk, n_chunks=n_chunks),
        grid=(b, h // hps, l // tq),
        in_specs=specs,
        out_specs=pl.BlockSpec((1, hps * V_DIM, tq), lambda bi, hi, i: (bi, hi, i)),
        out_shape=jax.ShapeDtypeStruct((b, h * V_DIM, l), BF16),
        scratch_shapes=scratch,
        compiler_params=_cparams(("parallel", "parallel", "arbitrary")),
        name="attention",
    )(*args)


def _dft_step1_kernel(x_ref, c_ref, s_ref, yr_ref, yi_ref):
    xb = x_ref[0].astype(BF16)
    yr_ref[0] = jnp.dot(c_ref[...], xb, preferred_element_type=F32)
    yi_ref[0] = jnp.dot(s_ref[...], xb, preferred_element_type=F32)


def _dft_step2_kernel(yr_ref, yi_ref, tc_ref, ts_ref, c_ref, s_ref, cc_ref, sc_ref, o_ref):
    for j in range(SUBLANES):
        yr, yi = yr_ref[0, j], yi_ref[0, j]
        tc, ts = tc_ref[j], ts_ref[j]
        zr = (yr * tc - yi * ts).astype(BF16)
        zi = (yi * tc + yr * ts).astype(BF16)
        a = (jnp.dot(c_ref[...], zr, preferred_element_type=F32)
             - jnp.dot(s_ref[...], zi, preferred_element_type=F32))
        bm = (jnp.dot(s_ref[...], zr, preferred_element_type=F32)
              + jnp.dot(c_ref[...], zi, preferred_element_type=F32))
        o_ref[0, :, j, :] = (jnp.dot(a.astype(BF16), cc_ref[...], preferred_element_type=F32)
                             - jnp.dot(bm.astype(BF16), sc_ref[...], preferred_element_type=F32))


def _dft_dense_kernel(x_ref, c_ref, s_ref, cc_ref, sc_ref, o_ref):
    xb = x_ref[0].astype(BF16)
    a = jnp.dot(c_ref[...], xb, preferred_element_type=F32)
    bm = jnp.dot(s_ref[...], xb, preferred_element_type=F32)
    o_ref[0] = (jnp.dot(a.astype(BF16), cc_ref[...], preferred_element_type=F32)
                - jnp.dot(bm.astype(BF16), sc_ref[...], preferred_element_type=F32))


def _dft_mats(n):
    ang = 2.0 * np.pi * np.outer(np.arange(n), np.arange(n)) / n
    return np.cos(ang), np.sin(ang)


def _channel_dft(l):
    c, s = _dft_mats(GROUP_DIM)
    eye = np.eye(N_GROUPS)
    norm = 1.0 / math.sqrt(l * GROUP_DIM)
    return (jnp.asarray(np.kron(eye, c) * norm, BF16), jnp.asarray(np.kron(eye, s) * norm, BF16))


def _dft_factors(l):
    n1 = 1 << (int(math.log2(l)) // 2)
    return n1, l // n1


def _fourier(u_fno, lane_block=4096):
    b, l, c = u_fno.shape
    cc, sc = _channel_dft(l)
    full = lambda a: pl.BlockSpec(a.shape, lambda *_: (0,) * a.ndim)
    if l <= 512:
        cm, sm = _dft_mats(l)
        cm, sm = jnp.asarray(cm, BF16), jnp.asarray(sm, BF16)
        blk = pl.BlockSpec((1, l, c), lambda bi: (bi, 0, 0))
        return pl.pallas_call(
            _dft_dense_kernel, grid=(b,),
            in_specs=[blk, full(cm), full(sm), full(cc), full(sc)],
            out_specs=blk, out_shape=jax.ShapeDtypeStruct((b, l, c), F32),
            compiler_params=_cparams(("parallel",)), name="dft_dense",
        )(u_fno, cm, sm, cc, sc)
    n1, n2 = _dft_factors(l)
    c1, s1 = _dft_mats(n1)
    c2, s2 = _dft_mats(n2)
    c1, s1, c2, s2 = (jnp.asarray(m, BF16) for m in (c1, s1, c2, s2))
    ang = 2.0 * np.pi * np.outer(np.arange(n1), np.arange(n2)) / l
    tc = jnp.asarray(np.cos(ang)[:, :, None], F32)
    ts = jnp.asarray(np.sin(ang)[:, :, None], F32)
    w = n2 * c
    tn = min(lane_block, w)
    x2 = u_fno.reshape(b, n1, w)
    yspec = pl.BlockSpec((1, n1, tn), lambda bi, i: (bi, 0, i))
    yr, yi = pl.pallas_call(
        _dft_step1_kernel, grid=(b, w // tn),
        in_specs=[yspec, full(c1), full(s1)],
        out_specs=[yspec, yspec],
        out_shape=[jax.ShapeDtypeStruct((b, n1, w), F32)] * 2,
        compiler_params=_cparams(("parallel", "parallel")), name="dft_step1",
    )(x2, c1, s1)
    yr = yr.reshape(b, n1, n2, c)
    yi = yi.reshape(b, n1, n2, c)
    slab = pl.BlockSpec((1, SUBLANES, n2, c), lambda bi, i: (bi, i, 0, 0))
    tw = pl.BlockSpec((SUBLANES, n2, 1), lambda bi, i: (i, 0, 0))
    out = pl.pallas_call(
        _dft_step2_kernel, grid=(b, n1 // SUBLANES),
        in_specs=[slab, slab, tw, tw, full(c2), full(s2), full(cc), full(sc)],
        out_specs=pl.BlockSpec((1, n2, SUBLANES, c), lambda bi, i: (bi, 0, i, 0)),
        out_shape=jax.ShapeDtypeStruct((b, n2, n1, c), F32),
        compiler_params=_cparams(("parallel", "parallel")), name="dft_step2",
    )(yr, yi, tc, ts, c2, s2, cc, sc)
    return out.reshape(b, l, c)


POOL_HALO = SUBLANES


def _mixout_kernel(x_ref, gt_ref, up_ref, prev_ref, next_ref, f_ref, at_ref,
                   wp_ref, ps_ref, wf_ref, wo_ref, o_ref, *, seq_len):
    i = pl.program_id(1)
    tm = x_ref.shape[1]
    u = up_ref[0]
    prev = jnp.where(i > 0, prev_ref[0], 0.0)
    nxt = jnp.where(i < pl.num_programs(1) - 1, next_ref[0], 0.0)
    p = jnp.concatenate([prev, u, nxt], axis=0)
    n = tm + 2 * POOL_HALO
    s1 = pltpu.roll(p, 1, 0) + p
    s2 = pltpu.roll(s1, 1, 0) + pltpu.roll(s1, n - 1, 0)
    s4 = pltpu.roll(s2, 2, 0) + pltpu.roll(s2, n - 2, 0)
    s8 = pltpu.roll(s4, 4, 0) + pltpu.roll(s4, n - 4, 0)
    lane = lax.broadcasted_iota(jnp.int32, (tm, MIX_GROUP_WIDTH), 1)
    grp = lane // GROUP_DIM
    lo, hi = POOL_HALO, POOL_HALO + tm
    win = jnp.where(grp == 0, s1[lo:hi],
                    jnp.where(grp == 1, s2[lo:hi], jnp.where(grp == 2, s4[lo:hi], s8[lo:hi])))
    half = jnp.left_shift(1, grp)
    t = i * tm + lax.broadcasted_iota(jnp.int32, (tm, MIX_GROUP_WIDTH), 0)
    cnt = (jnp.minimum(t + half, seq_len) - jnp.maximum(t - half, 0)).astype(F32)
    dlt = win / cnt - u
    pool_y = jnp.dot(dlt.astype(BF16), wp_ref[...], preferred_element_type=F32) * ps_ref[...]
    fno_y = jnp.dot(f_ref[0].astype(BF16), wf_ref[...], preferred_element_type=F32)
    w = MIX_GROUP_WIDTH
    y = (jnp.dot(pool_y.astype(BF16), wo_ref[0:w, :], preferred_element_type=F32)
         + jnp.dot(fno_y.astype(BF16), wo_ref[w:2 * w, :], preferred_element_type=F32)
         + lax.dot_general(at_ref[0], wo_ref[2 * w:, :], (((0,), (0,)), ((), ())),
                           preferred_element_type=F32))
    o_ref[0] = x_ref[0] + gt_ref[0] * y


def _mixer_output(x, gt, u_pool, f, attn, wp_bd, pool_scale, wf_bd, w_out, tm):
    b, l, d = x.shape
    tm = min(tm, l)
    nb = tm // POOL_HALO
    last = l // POOL_HALO - 1
    full = lambda a: pl.BlockSpec(a.shape, lambda bi, i: (0,) * a.ndim)
    w = MIX_GROUP_WIDTH
    return pl.pallas_call(
        functools.partial(_mixout_kernel, seq_len=l),
        grid=(b, l // tm),
        in_specs=[pl.BlockSpec((1, tm, d), lambda bi, i: (bi, i, 0)),
                  pl.BlockSpec((1, 1, d), lambda bi, i: (bi, 0, 0)),
                  pl.BlockSpec((1, tm, w), lambda bi, i: (bi, i, 0)),
                  pl.BlockSpec((1, POOL_HALO, w), lambda bi, i: (bi, jnp.maximum(i * nb - 1, 0), 0)),
                  pl.BlockSpec((1, POOL_HALO, w), lambda bi, i: (bi, jnp.minimum((i + 1) * nb, last), 0)),
                  pl.BlockSpec((1, tm, w), lambda bi, i: (bi, i, 0)),
                  pl.BlockSpec((1, attn.shape[1], tm), lambda bi, i: (bi, 0, i)),
                  full(wp_bd), full(pool_scale), full(wf_bd), full(w_out)],
        out_specs=pl.BlockSpec((1, tm, d), lambda bi, i: (bi, i, 0)),
        out_shape=jax.ShapeDtypeStruct((b, l, d), F32),
        compiler_params=_cparams(("parallel", "arbitrary")),
        name="mixer_output",
    )(x, gt, u_pool, u_pool, u_pool, f, attn, wp_bd, pool_scale, wf_bd, w_out)


def _first_index_of_max(vals, iota, n):
    mx = jnp.max(vals, axis=0, keepdims=True)
    ix = jnp.min(jnp.where(vals == mx, iota, n), axis=0, keepdims=True)
    return mx, ix


def _ffn_front_kernel(*refs, n_lat, has_ctx):
    if has_ctx:
        x_ref, xc_ref = refs[:2]
        refs = refs[2:]
    else:
        x_ref, refs = refs[0], refs[1:]
    (sh_ref, sc_ref, gt_ref, g_ref, wr_ref, rb_ref, wsg_ref, wsu_ref, wsd_ref,
     xo_ref, xs_ref, slot_ref, wt_ref, cnt_ref) = refs
    x = x_ref[0]
    if has_ctx:
        x = jnp.where(pl.program_id(0) < n_lat, x, xc_ref[0])
    tm = x.shape[0]
    h = _rms(x, g_ref[...]) * (1.0 + sc_ref[0]) + sh_ref[0]

    logits = lax.dot_general(wr_ref[...], h, (((1,), (1,)), ((), ())),
                             preferred_element_type=F32, precision=HIGHEST)
    aff = jax.nn.sigmoid(logits)
    sel = aff + rb_ref[...]
    e_iota = lax.broadcasted_iota(jnp.int32, (N_EXPERTS, tm), 0).astype(F32)
    neg = jnp.float32(-jnp.inf)
    gscores = []
    for g in range(N_EXPERT_GROUPS):
        blk = sel[g * EXPERTS_PER_GROUP:(g + 1) * EXPERTS_PER_GROUP]
        it = lax.broadcasted_iota(jnp.int32, blk.shape, 0).astype(F32)
        m1, i1 = _first_index_of_max(blk, it, float(EXPERTS_PER_GROUP))
        m2 = jnp.max(jnp.where(it == i1, neg, blk), axis=0, keepdims=True)
        gscores.append(m1 + m2)
    gs = jnp.concatenate(gscores, axis=0)
    g_iota = lax.broadcasted_iota(jnp.int32, gs.shape, 0).astype(F32)
    gself = jnp.zeros(gs.shape, F32)
    for _ in range(TOPK_GROUPS):
        _, ig = _first_index_of_max(gs, g_iota, float(N_EXPERT_GROUPS))
        hit = g_iota == ig
        gself = jnp.where(hit, 1.0, gself)
        gs = jnp.where(hit, neg, gs)
    emask = jnp.concatenate(
        [jnp.broadcast_to(gself[g:g + 1], (EXPERTS_PER_GROUP, tm)) for g in range(N_EXPERT_GROUPS)], axis=0)
    masked = jnp.where(emask > 0.5, sel, neg)
    hits, wts = [], []
    chosen = jnp.zeros((N_EXPERTS, tm), F32)
    for _ in range(TOP_K):
        _, ie = _first_index_of_max(masked, e_iota, float(N_EXPERTS))
        hit = e_iota == ie
        hits.append(hit)
        wts.append(jnp.sum(jnp.where(hit, aff, 0.0), axis=0, keepdims=True))
        chosen = jnp.where(hit, 1.0, chosen)
        masked = jnp.where(hit, neg, masked)
    wsum = wts[0]
    for w in wts[1:]:
        wsum = wsum + w
    scale = ROUTED_SCALE / wsum
    wt_ref[...] = jnp.concatenate([w * scale for w in wts], axis=0)

    r_io = lax.broadcasted_iota(jnp.int32, (tm, tm), 0)
    c_io = lax.broadcasted_iota(jnp.int32, (tm, tm), 1)
    tri = (r_io <= c_io).astype(BF16)
    incl = jnp.dot(chosen.astype(BF16), tri, preferred_element_type=F32)
    count = jnp.sum(chosen, axis=1, keepdims=True)
    n_pieces = jnp.floor((count + (PIECE - 1)) * (1.0 / PIECE))
    cnt_ref[0] = n_pieces
    e_r = lax.broadcasted_iota(jnp.int32, (N_EXPERTS, N_EXPERTS), 0)
    e_c = lax.broadcasted_iota(jnp.int32, (N_EXPERTS, N_EXPERTS), 1)
    before = (e_c < e_r).astype(BF16)
    piece_off = jnp.dot(before, jnp.broadcast_to(n_pieces, (N_EXPERTS, LANES)).astype(BF16),
                        preferred_element_type=F32)[:, :1]
    pos = piece_off * PIECE + incl - chosen
    slots = [jnp.sum(jnp.where(hit, pos, 0.0), axis=0, keepdims=True) for hit in hits]
    slot_ref[...] = jnp.concatenate(slots, axis=0).astype(jnp.int32)

    row_io = lax.broadcasted_iota(jnp.int32, (TILE_ROWS, tm), 0).astype(F32)
    onehot = jnp.zeros((TILE_ROWS, tm), F32)
    keep = (pl.program_id(0) < pl.num_programs(0) - 1).astype(F32)
    for s in slots:
        onehot = jnp.where(row_io == s, keep, onehot)
    onehot = onehot.astype(BF16)
    hb = h.astype(BF16)
    d = hb.shape[1]
    nc = 2 * LANES
    for j in range(d // nc):
        xs_ref[0, :, j * nc:(j + 1) * nc] = jnp.dot(
            onehot, hb[:, j * nc:(j + 1) * nc], preferred_element_type=F32).astype(BF16)

    a = jnp.dot(hb, wsg_ref[...], preferred_element_type=F32)
    bb = jnp.dot(hb, wsu_ref[...], preferred_element_type=F32)
    sh_out = jnp.dot((_silu(a) * bb).astype(BF16), wsd_ref[...], preferred_element_type=F32)
    xo_ref[0] = x + gt_ref[0] * sh_out


def _ffn_front(x, xc, sh, sc, gt, g, w_rt, rb, wsg, wsu, wsd):
    b, l, d = x.shape
    tm = MOE_TILE
    nt = l // tm
    n_lat = b * nt
    has_ctx = xc is not None
    ntc = xc.shape[1] // tm if has_ctx else 0
    n_all = n_lat + b * ntc
    full = lambda a: pl.BlockSpec(a.shape, lambda i: (0,) * a.ndim)

    real = lambda i: jnp.minimum(i, n_all - 1)

    def lat_map(i):
        j = jnp.minimum(i, n_lat - 1)
        return (j // nt, j % nt, 0)

    def ctx_map(i):
        j = jnp.clip(i - n_lat, 0, b * ntc - 1)
        return (j // ntc, j % ntc, 0)

    vec = pl.BlockSpec((1, 1, d), lambda i: (jnp.where(real(i) < n_lat, real(i) // nt, b), 0, 0))
    xspecs = [pl.BlockSpec((1, tm, d), lat_map)] + ([pl.BlockSpec((1, tm, d), ctx_map)] if has_ctx else [])
    xargs = [x] + ([xc] if has_ctx else [])
    tokmap = lambda i: (0, real(i))
    return pl.pallas_call(
        functools.partial(_ffn_front_kernel, n_lat=n_lat, has_ctx=has_ctx),
        grid=(n_all + 1,),
        in_specs=xspecs + [vec, vec, vec, full(g), full(w_rt), full(rb), full(wsg), full(wsu), full(wsd)],
        out_specs=[pl.BlockSpec((1, tm, d), lambda i: (real(i), 0, 0)),
                   pl.BlockSpec((1, TILE_ROWS, d), lambda i: (i, 0, 0)),
                   pl.BlockSpec((TOP_K, tm), tokmap), pl.BlockSpec((TOP_K, tm), tokmap),
                   pl.BlockSpec((1, N_EXPERTS, 1), lambda i: (real(i), 0, 0))],
        out_shape=[jax.ShapeDtypeStruct((n_all, tm, d), F32),
                   jax.ShapeDtypeStruct((n_all + 1, TILE_ROWS, d), BF16),
                   jax.ShapeDtypeStruct((TOP_K, n_all * tm), jnp.int32),
                   jax.ShapeDtypeStruct((TOP_K, n_all * tm), F32),
                   jax.ShapeDtypeStruct((n_all, N_EXPERTS, 1), F32)],
        compiler_params=_cparams(("arbitrary",)),
        name="ffn_front",
    )(*xargs, sh, sc, gt, g, w_rt, rb, wsg, wsu, wsd)


def _expert_kernel(bexp_ref, nused_ref, rlen_ref, rsrc_ref, rnext_ref, xs_ref, wgu_ref, wd_ref, y_ref,
                   state, dst, xbuf, ybuf, sem_in, sem_out, *, runs_per_expert, spare_piece):
    i = pl.program_id(0)
    n_used = nused_ref[0]
    end_run = rlen_ref.shape[0] - 1

    def copy_in(pid, slot, p):
        return pltpu.make_async_copy(xs_ref.at[pid], xbuf.at[slot, pl.ds(p * PIECE, PIECE), :], sem_in.at[slot])

    def copy_out(pid, slot, p):
        return pltpu.make_async_copy(ybuf.at[slot, pl.ds(p * PIECE, PIECE), :], y_ref.at[pid], sem_out.at[slot])

    def issue_gather(blk, slot):
        r_lo = bexp_ref[blk] * runs_per_expert
        r_end = r_lo + runs_per_expert
        fresh = state[0] < r_lo
        r = jnp.where(fresh, r_lo, state[0])
        o = jnp.where(fresh, 0, state[1])
        read_pid = jnp.int32(0)
        for p in range(BLOCK_PIECES):
            exhausted = o >= rlen_ref[jnp.minimum(r, end_run)]
            r = jnp.where(exhausted, rnext_ref[jnp.minimum(r + 1, end_run)], r)
            o = jnp.where(exhausted, 0, o)
            valid = r < r_end
            pid = rsrc_ref[jnp.minimum(r, end_run)] + o
            read_pid = jnp.where(valid, pid, read_pid)
            dst[slot, p] = jnp.where(valid, pid, spare_piece + slot * BLOCK_PIECES + p)
            copy_in(read_pid, slot, p).start()
            o = o + valid.astype(jnp.int32)
        state[0] = r
        state[1] = o

    def wait_gather(slot):
        pltpu.make_async_copy(xbuf.at[slot], xbuf.at[slot], sem_in.at[slot]).wait()

    def send_results(slot):
        for p in range(BLOCK_PIECES):
            copy_out(dst[slot, p], slot, p).start()

    def wait_results(slot):
        pltpu.make_async_copy(ybuf.at[slot], ybuf.at[slot], sem_out.at[slot]).wait()

    @pl.when(i < n_used)
    def _():
        slot = i % 2

        @pl.when(i == 0)
        def _():
            state[0] = 0
            state[1] = 0
            issue_gather(0, 0)

        @pl.when(i + 1 < n_used)
        def _():
            issue_gather(i + 1, (i + 1) % 2)

        wait_gather(slot)

        @pl.when(i >= 2)
        def _():
            wait_results(slot)

        gu = jnp.dot(xbuf[slot], wgu_ref[0], preferred_element_type=F32)
        f = gu.shape[1] // 2
        act = (_silu(gu[:, :f]) * gu[:, f:]).astype(BF16)
        ybuf[slot] = jnp.dot(act, wd_ref[0], preferred_element_type=F32).astype(BF16)
        send_results(slot)

        @pl.when(i == n_used - 1)
        def _():
            wait_results(slot)

            @pl.when(i >= 1)
            def _():
                wait_results(1 - slot)


def _experts(block_expert, n_used, run_len, run_src, run_next, xs_pieces, w_gu, w_down, runs_per_expert):
    n_blocks = block_expert.shape[0]
    _, _, d = xs_pieces.shape
    _, _, f2 = w_gu.shape
    grid_spec = pltpu.PrefetchScalarGridSpec(
        num_scalar_prefetch=5,
        grid=(n_blocks,),
        in_specs=[pl.BlockSpec(memory_space=pl.ANY),
                  pl.BlockSpec((1, d, f2), lambda i, be, *_: (be[i], 0, 0)),
                  pl.BlockSpec((1, f2 // 2, d), lambda i, be, *_: (be[i], 0, 0))],
        out_specs=pl.BlockSpec(memory_space=pl.ANY),
        scratch_shapes=[pltpu.SMEM((2,), jnp.int32),
                        pltpu.SMEM((2, BLOCK_PIECES), jnp.int32),
                        pltpu.VMEM((2, EXPERT_BLOCK, d), BF16),
                        pltpu.VMEM((2, EXPERT_BLOCK, d), BF16),
                        pltpu.SemaphoreType.DMA((2,)), pltpu.SemaphoreType.DMA((2,))],
    )
    return pl.pallas_call(
        functools.partial(_expert_kernel, runs_per_expert=runs_per_expert,
                          spare_piece=runs_per_expert * PIECES_PER_TILE),
        grid_spec=grid_spec,
        out_shape=jax.ShapeDtypeStruct(xs_pieces.shape, xs_pieces.dtype),
        input_output_aliases={5: 0},
        compiler_params=_cparams(("arbitrary",)),
        name="experts",
    )(block_expert, n_used, run_len, run_src, run_next, xs_pieces, w_gu, w_down)


def _combine_kernel(y_ref, slot_ref, w_ref, x_ref, gt_ref, gf_ref, o_ref, *, final):
    tm = x_ref.shape[1]
    slot = slot_ref[...].astype(F32)
    w = w_ref[...]
    col_io = lax.broadcasted_iota(jnp.int32, (tm, TILE_ROWS), 1).astype(F32)
    sel = jnp.zeros((tm, TILE_ROWS), F32)
    for k in range(TOP_K):
        sel = jnp.where(col_io == slot[:, k:k + 1], w[:, k:k + 1], sel)
    routed = jnp.dot(sel.astype(BF16), y_ref[0], preferred_element_type=F32)
    out = x_ref[0] + gt_ref[0] * routed
    if final:
        out = _rms(out, gf_ref[...])
    o_ref[0] = out


def _combine(y_tiles, slot_tok, w_tok, x_tiles, gt, g_final, seq_shape, tile_off, final):
    b, l = seq_shape
    _, tm, d = x_tiles.shape
    nt = l // tm
    tile = lambda bi, i: tile_off + bi * nt + i
    return pl.pallas_call(
        functools.partial(_combine_kernel, final=final),
        grid=(b, nt),
        in_specs=[pl.BlockSpec((1, TILE_ROWS, d), lambda bi, i: (tile(bi, i), 0, 0)),
                  pl.BlockSpec((tm, TOP_K), lambda bi, i: (tile(bi, i), 0)),
                  pl.BlockSpec((tm, TOP_K), lambda bi, i: (tile(bi, i), 0)),
                  pl.BlockSpec((1, tm, d), lambda bi, i: (tile(bi, i), 0, 0)),
                  pl.BlockSpec((1, 1, d), lambda bi, i: (bi, 0, 0)),
                  pl.BlockSpec(g_final.shape, lambda bi, i: (0, 0))],
        out_specs=pl.BlockSpec((1, tm, d), lambda bi, i: (bi, i, 0)),
        out_shape=jax.ShapeDtypeStruct((b, l, d), F32),
        compiler_params=_cparams(("parallel", "parallel")),
        name="combine",
    )(y_tiles, slot_tok, w_tok, x_tiles, gt, g_final)


def _rope_tables(l):
    rows = l // GRID_W
    row = jnp.repeat(jnp.arange(rows, dtype=F32), GRID_W)
    col = jnp.tile(jnp.arange(GRID_W, dtype=F32), rows)
    inv_freq = ROPE_THETA ** (-jnp.arange(ROPE_FREQS, dtype=F32) * 2.0 / (2 * ROPE_FREQS))
    ar, ac = row[:, None] * inv_freq, col[:, None] * inv_freq
    ones = jnp.ones((l, ROPE_LANE0), F32)
    zpad = jnp.zeros((l, HEAD_PAD - ROPE_LANE0 - QK_ROPE), F32)
    cos_t = jnp.concatenate([ones, jnp.cos(ar), jnp.cos(ar), jnp.cos(ac), jnp.cos(ac), zpad], axis=1)
    sin_t = jnp.concatenate([0.0 * ones, -jnp.sin(ar), jnp.sin(ar), -jnp.sin(ac), jnp.sin(ac), zpad], axis=1)
    return cos_t, sin_t


def _identity_tables(l):
    lane = jnp.arange(HEAD_PAD)
    cos_t = jnp.broadcast_to((lane < ROPE_LANE0 + QK_ROPE).astype(F32), (l, HEAD_PAD))
    return cos_t, jnp.zeros((l, HEAD_PAD), F32)


def _block_diag(w):
    g, a, b = w.shape
    out = jnp.zeros((g * a, g * b), w.dtype)
    for i in range(g):
        out = out.at[i * a:(i + 1) * a, i * b:(i + 1) * b].set(w[i])
    return out


def _prep_layer(w_in, w_pool, w_fno, w_uq, w_ukv):
    d = w_in.shape[0]
    kr = jnp.zeros((d, HEAD_PAD), w_in.dtype).at[:, ROPE_LANE0:ROPE_LANE0 + QK_ROPE].set(w_in[:, COL_KR:])
    w_in_p = jnp.concatenate([w_in[:, :COL_KR], kr], axis=1).astype(BF16)
    qk = QK_NOPE + QK_ROPE
    wq = w_uq.reshape(Q_LORA, N_HEADS, qk)
    wq = jnp.pad(wq, ((0, 0), (0, 0), (0, HEAD_PAD - qk))).reshape(Q_LORA, N_HEADS * HEAD_PAD)
    wkv = w_ukv.reshape(KV_LORA, N_HEADS, QK_NOPE + V_DIM)
    wk = jnp.pad(wkv[..., :QK_NOPE], ((0, 0), (0, 0), (0, HEAD_PAD - QK_NOPE))).reshape(KV_LORA, -1)
    wv = jnp.pad(wkv[..., QK_NOPE:], ((0, 0), (0, 0), (0, V_ROWS - V_DIM))).reshape(KV_LORA, -1)
    return (w_in_p, wq.T.astype(BF16), wk.astype(BF16), wv.T.astype(BF16),
            _block_diag(w_pool).astype(BF16), _block_diag(w_fno).astype(BF16))


def _run_tables(n_pieces):
    n_all = n_pieces.shape[0]
    npc = n_pieces.astype(jnp.int32)
    piece_off = jnp.cumsum(npc, axis=1) - npc
    run_src = (jnp.arange(n_all, dtype=jnp.int32)[:, None] * PIECES_PER_TILE + piece_off).T.reshape(-1)
    run_len = npc.T.reshape(-1)
    n_runs = run_len.shape[0]
    run_id = jnp.arange(n_runs, dtype=jnp.int32)
    run_next = lax.cummin(jnp.where(run_len > 0, run_id, n_runs), axis=0, reverse=True)
    pad1 = lambda a, v: jnp.concatenate([a, jnp.full((1,), v, jnp.int32)])
    run_len, run_src, run_next = pad1(run_len, 0), pad1(run_src, 0), pad1(run_next, n_runs)
    blocks = (jnp.sum(npc, axis=0) + BLOCK_PIECES - 1) // BLOCK_PIECES
    block_end = jnp.cumsum(blocks)
    max_blocks = n_all * PIECES_PER_TILE // BLOCK_PIECES + N_EXPERTS
    blk = jnp.arange(max_blocks, dtype=jnp.int32)
    block_expert = jnp.minimum(jnp.sum((block_end[None, :] <= blk[:, None]).astype(jnp.int32), axis=1),
                               N_EXPERTS - 1)
    return block_expert, block_end[-1:], run_len, run_src, run_next


TM_PROJ = 256
TM_MIX = 256
ATTN_TQ = 2048
ATTN_TK = 512


def kernel(x, c, ctx, c_ctx, w_mod, b_mod, g_mix, g_ffn, w_in, w_pool, pool_scale, w_fno, g_q, w_uq,
           g_kv, w_ukv, w_out, w_router, router_bias, w_gate, w_up, w_down, w_sh_gate, w_sh_up,
           w_sh_down, g_final):
    b, l, d = x.shape
    lc = ctx.shape[1]
    depth = w_mod.shape[0]
    cc = jnp.zeros((SUBLANES, d), F32).at[:b].set(c).at[b].set(c_ctx)
    mods = _modulation(cc, w_mod, b_mod)
    with_transposed = lambda cs: (cs[0], cs[1], cs[0].T, cs[1].T)
    tab_lat = with_transposed(_rope_tables(l))
    tab_ctx = with_transposed(_identity_tables(lc))
    xc = ctx
    row = lambda v: v.reshape(1, -1)
    for li in range(depth):
        last = li == depth - 1
        m = mods[li].reshape(SUBLANES, N_MOD, d)
        lat = [m[:b, j][:, None, :] for j in range(N_MOD)]
        cm = [jnp.broadcast_to(m[b, j][None, None, :], (b, 1, d)) for j in range(N_MOD)]
        w_in_p, wq_t, wk_p, wv_t, wp_bd, wf_bd = _prep_layer(w_in[li], w_pool[li], w_fno[li], w_uq[li], w_ukv[li])
        w_out_b = w_out[li].astype(BF16)
        gq, gkv, gm = row(g_q[li]), row(g_kv[li]), row(g_mix[li])

        up, uf, q, k, v = _in_projection(x, lat[0], lat[1], gm, w_in_p, gq, wq_t, gkv, wk_p, wv_t,
                                         tab_lat, TM_PROJ)
        upc, ufc, qc, kc, vc = _in_projection(xc, cm[0], cm[1], gm, w_in_p, gq, wq_t, gkv, wk_p, wv_t,
                                              tab_ctx, TM_PROJ)
        attn = _attention(q, kc, vc, k, v, tq=ATTN_TQ, tk=ATTN_TK)
        f = _fourier(uf)
        ps = row(pool_scale[li])
        x = _mixer_output(x, lat[2], up, f, attn, wp_bd, ps, wf_bd, w_out_b, TM_MIX)
        if not last:
            attn_c = _attention(qc, kc, vc, tq=ATTN_TQ, tk=ATTN_TK)
            fc = _fourier(ufc)
            xc = _mixer_output(xc, cm[2], upc, fc, attn_c, wp_bd, ps, wf_bd, w_out_b, TM_MIX)

        w_rt = w_router[li].T
        rb = router_bias[li].reshape(N_EXPERTS, 1)
        wsg, wsu, wsd = w_sh_gate[li].astype(BF16), w_sh_up[li].astype(BF16), w_sh_down[li].astype(BF16)
        with_ctx = lambda j: jnp.concatenate([lat[j], cm[j][:1]], axis=0)
        x_tiles, xs, slot_t, w_t, n_pieces = _ffn_front(
            x, None if last else xc, with_ctx(3), with_ctx(4), with_ctx(5), row(g_ffn[li]), w_rt, rb, wsg, wsu, wsd)
        n_all = xs.shape[0] - 1
        block_expert, n_used, run_len, run_src, run_next = _run_tables(n_pieces[..., 0])
        w_gu = jnp.concatenate([w_gate[li], w_up[li]], axis=-1).astype(BF16)
        y = _experts(block_expert, n_used, run_len, run_src, run_next,
                     xs.reshape((n_all + 1) * PIECES_PER_TILE, PIECE, d), w_gu, w_down[li].astype(BF16), n_all)
        y = y.reshape(n_all + 1, TILE_ROWS, d)
        slot_tok, w_tok = slot_t.T, w_t.T
        x = _combine(y, slot_tok, w_tok, x_tiles, lat[5], row(g_final), (b, l), 0, last)
        if not last:
            xc = _combine(y, slot_tok, w_tok, x_tiles, cm[5], row(g_final), (b, lc), b * l // MOE_TILE, False)
    return x
```

```python
import functools
import math

import jax
import jax.numpy as jnp
import numpy as np
from jax import lax
from jax.experimental import pallas as pl
from jax.experimental.pallas import tpu as pltpu

F32 = jnp.float32
BF16 = jnp.bfloat16
HIGHEST = lax.Precision.HIGHEST

EPS = 1e-6
N_MOD = 6
GRID_W = 64
POOL_WINDOWS = (2, 4, 8, 16)
GROUP_DIM = 64
N_GROUPS = 4
MIX_GROUP_WIDTH = N_GROUPS * GROUP_DIM
N_HEADS = 8
QK_NOPE = 64
QK_ROPE = 32
V_DIM = 64
Q_LORA = 384
KV_LORA = 256
ROPE_FREQS = QK_ROPE // 4
ROPE_THETA = 10000.0
SOFTMAX_SCALE = (QK_NOPE + QK_ROPE) ** -0.5
N_EXPERTS = 64
TOP_K = 8
N_EXPERT_GROUPS = 8
TOPK_GROUPS = 4
EXPERTS_PER_GROUP = N_EXPERTS // N_EXPERT_GROUPS
ROUTED_SCALE = 2.5

LANES = 128
SUBLANES = 8
HEAD_PAD = LANES
VMEM_LIMIT = 56 * 1024 * 1024

COL_FNO = MIX_GROUP_WIDTH
COL_Q = COL_FNO + MIX_GROUP_WIDTH
COL_KV = COL_Q + Q_LORA
COL_KR = COL_KV + KV_LORA
IN_PAD = COL_KR + HEAD_PAD
ROPE_LANE0 = QK_NOPE

MOE_TILE = 256
PIECE = 16
PIECES_PER_TILE = MOE_TILE * TOP_K // PIECE + N_EXPERTS
TILE_ROWS = PIECES_PER_TILE * PIECE
BLOCK_PIECES = 32
EXPERT_BLOCK = BLOCK_PIECES * PIECE


def _cparams(sem, vmem=VMEM_LIMIT):
    return pltpu.CompilerParams(dimension_semantics=sem, vmem_limit_bytes=vmem)


def _rms(x, g):
    return x * lax.rsqrt(jnp.mean(x * x, axis=-1, keepdims=True) + EPS) * g


def _silu(x):
    return x * jax.nn.sigmoid(x)


def _mod_kernel(c_ref, w_ref, b_ref, o_ref):
    s = _silu(c_ref[...])
    o_ref[0] = jnp.dot(s, w_ref[0], preferred_element_type=F32, precision=HIGHEST) + b_ref[0]


def _modulation(cc, w_mod, b_mod):
    depth, d, nd = w_mod.shape
    n_chunks = nd // d
    return pl.pallas_call(
        _mod_kernel,
        grid=(depth, n_chunks),
        in_specs=[pl.BlockSpec((SUBLANES, d), lambda l, j: (0, 0)),
                  pl.BlockSpec((1, d, d), lambda l, j: (l, 0, j)),
                  pl.BlockSpec((1, 1, d), lambda l, j: (l, 0, j))],
        out_specs=pl.BlockSpec((1, SUBLANES, d), lambda l, j: (l, 0, j)),
        out_shape=jax.ShapeDtypeStruct((depth, SUBLANES, nd), F32),
        compiler_params=_cparams(("parallel", "parallel")),
        name="modulation",
    )(cc, w_mod, b_mod.reshape(depth, 1, nd))


def _rope(t, c, s):
    lane = lax.broadcasted_iota(jnp.int32, t.shape, 1)
    first_half = (lane % (2 * ROPE_FREQS)) < ROPE_FREQS
    partner = jnp.where(first_half, pltpu.roll(t, LANES - ROPE_FREQS, 1), pltpu.roll(t, ROPE_FREQS, 1))
    return t * c + partner * s


def _rope_rows(t, c, s):
    rw = lax.broadcasted_iota(jnp.int32, t.shape, 0)
    first_half = (rw % (2 * ROPE_FREQS)) < ROPE_FREQS
    n = t.shape[0]
    partner = jnp.where(first_half, pltpu.roll(t, n - ROPE_FREQS, 0), pltpu.roll(t, ROPE_FREQS, 0))
    return t * c + partner * s


V_ROWS = V_DIM + 16


def _inproj_kernel(x_ref, sh_ref, sc_ref, g_ref, win_ref, gq_ref, wuqt_ref, gkv_ref, wuk_ref, wuvt_ref,
                   cos_ref, sin_ref, cost_ref, sint_ref, up_ref, uf_ref, q_ref, k_ref, v_ref):
    x = x_ref[0]
    h = _rms(x, g_ref[...]) * (1.0 + sc_ref[0]) + sh_ref[0]
    u = jnp.dot(h.astype(BF16), win_ref[...], preferred_element_type=F32)
    up_ref[0] = u[:, :COL_FNO]
    uf_ref[0] = u[:, COL_FNO:COL_Q]
    nt = (((1,), (1,)), ((), ()))
    cq = _rms(u[:, COL_Q:COL_KV], gq_ref[...]).astype(BF16)
    qt = lax.dot_general(wuqt_ref[...], cq, nt, preferred_element_type=F32)
    ckv = _rms(u[:, COL_KV:COL_KR], gkv_ref[...]).astype(BF16)
    kn = jnp.dot(ckv, wuk_ref[...], preferred_element_type=F32)
    vt = lax.dot_general(wuvt_ref[...], ckv, nt, preferred_element_type=F32)
    kr = _rope(u[:, COL_KR:IN_PAD], cos_ref[...], sin_ref[...])
    cos_t, sin_t = cost_ref[...], sint_ref[...]
    tm = x.shape[0]
    ones_row = (lax.broadcasted_iota(jnp.int32, (V_ROWS, tm), 0) == V_DIM).astype(F32)
    q_scale = SOFTMAX_SCALE * math.log2(math.e)
    for hd in range(N_HEADS):
        lo = hd * HEAD_PAD
        q_ref[0, hd] = (_rope_rows(qt[lo:lo + HEAD_PAD], cos_t, sin_t) * q_scale).astype(BF16)
        k_ref[0, hd] = (kn[:, lo:lo + HEAD_PAD] + kr).astype(BF16)
        v_ref[0, hd] = (vt[hd * V_ROWS:(hd + 1) * V_ROWS] + ones_row).astype(BF16)


def _in_projection(x, sh, sc, g, w_in_p, g_q, w_uq_t, g_kv, w_uk_p, w_uv_t, tables, tm):
    b, l, d = x.shape
    tm = min(tm, l)
    cos_r, sin_r, cos_c, sin_c = tables
    full = lambda a: pl.BlockSpec(a.shape, lambda bi, i: (0,) * a.ndim)
    vec = pl.BlockSpec((1, 1, d), lambda bi, i: (bi, 0, 0))
    rtab = pl.BlockSpec((tm, HEAD_PAD), lambda bi, i: (i, 0))
    ctab = pl.BlockSpec((HEAD_PAD, tm), lambda bi, i: (0, i))
    return pl.pallas_call(
        _inproj_kernel,
        grid=(b, l // tm),
        in_specs=[pl.BlockSpec((1, tm, d), lambda bi, i: (bi, i, 0)), vec, vec,
                  full(g), full(w_in_p), full(g_q), full(w_uq_t), full(g_kv), full(w_uk_p), full(w_uv_t),
                  rtab, rtab, ctab, ctab],
        out_specs=[pl.BlockSpec((1, tm, MIX_GROUP_WIDTH), lambda bi, i: (bi, i, 0)),
                   pl.BlockSpec((1, tm, MIX_GROUP_WIDTH), lambda bi, i: (bi, i, 0)),
                   pl.BlockSpec((1, N_HEADS, HEAD_PAD, tm), lambda bi, i: (bi, 0, 0, i)),
                   pl.BlockSpec((1, N_HEADS, tm, HEAD_PAD), lambda bi, i: (bi, 0, i, 0)),
                   pl.BlockSpec((1, N_HEADS, V_ROWS, tm), lambda bi, i: (bi, 0, 0, i))],
        out_shape=[jax.ShapeDtypeStruct((b, l, MIX_GROUP_WIDTH), F32),
                   jax.ShapeDtypeStruct((b, l, MIX_GROUP_WIDTH), F32),
                   jax.ShapeDtypeStruct((b, N_HEADS, HEAD_PAD, l), BF16),
                   jax.ShapeDtypeStruct((b, N_HEADS, l, HEAD_PAD), BF16),
                   jax.ShapeDtypeStruct((b, N_HEADS, V_ROWS, l), BF16)],
        compiler_params=_cparams(("parallel", "parallel")),
        name="in_projection",
    )(x, sh, sc, g, w_in_p, g_q, w_uq_t, g_kv, w_uk_p, w_uv_t, cos_r, sin_r, cos_c, sin_c)


ATTN_HEADS_PER_STEP = 2


def _attn_kernel(*refs, tk, n_chunks):
    if n_chunks:
        q_ref, kc_ref, vc_ref, k_ref, v_ref, o_ref, s_scr = refs
    else:
        q_ref, kc_ref, vc_ref, o_ref = refs
    tq = q_ref.shape[3]
    heads = range(ATTN_HEADS_PER_STEP)
    qs = [q_ref[0, hh] for hh in heads]

    def scores(hh, kc):
        return jnp.dot(kc, qs[hh], preferred_element_type=F32)

    def absorb(st, vt, m, acc):
        m_new = jnp.maximum(m, jnp.max(st, axis=0, keepdims=True))
        alpha = jnp.exp2(m - m_new)
        pt = jnp.exp2(st - m_new).astype(BF16)
        return m_new, alpha * acc + jnp.dot(vt, pt, preferred_element_type=F32)

    carry = tuple(absorb(scores(hh, kc_ref[0, hh]), vc_ref[0, hh],
                         jnp.full((1, tq), -1e30, F32), jnp.zeros((V_ROWS, tq), F32)) for hh in heads)
    if n_chunks:
        def produce(slot, chunk):
            off = pl.multiple_of(chunk * tk, tk)
            for hh in heads:
                s_scr[slot, hh] = scores(hh, k_ref[0, hh, pl.ds(off, tk), :])

        def consume(slot, chunk, carry):
            off = pl.multiple_of(chunk * tk, tk)
            return tuple(absorb(s_scr[slot, hh], v_ref[0, hh, :, pl.ds(off, tk)], *carry[hh]) for hh in heads)

        produce(0, 0)

        def body(j, carry):
            a = 2 * j
            produce(1, a + 1)
            carry = consume(0, a, carry)
            produce(0, jnp.minimum(a + 2, n_chunks - 1))
            return consume(1, a + 1, carry)

        carry = lax.fori_loop(0, n_chunks // 2, body, carry)
    for hh in heads:
        acc = carry[hh][1]
        out = acc[:V_DIM] / acc[V_DIM:V_DIM + 1]
        o_ref[0, hh * V_DIM:(hh + 1) * V_DIM, :] = out.astype(o_ref.dtype)


def _attention(q_t, k_ctx, v_ctx_t, k=None, v_t=None, *, tq, tk):
    b, h, _, l = q_t.shape
    lc = k_ctx.shape[2]
    tq = min(tq, l)
    hps = ATTN_HEADS_PER_STEP
    qspec = pl.BlockSpec((1, hps, HEAD_PAD, tq), lambda bi, hi, i: (bi, hi, 0, i))
    kcspec = pl.BlockSpec((1, hps, lc, HEAD_PAD), lambda bi, hi, i: (bi, hi, 0, 0))
    vcspec = pl.BlockSpec((1, hps, V_ROWS, lc), lambda bi, hi, i: (bi, hi, 0, 0))
    args, specs, n_chunks, scratch = [q_t, k_ctx, v_ctx_t], [qspec, kcspec, vcspec], 0, []
    if k is not None:
        lk = k.shape[2]
        tk = min(tk, lk // 2)
        n_chunks = lk // tk
        assert n_chunks % 2 == 0 and n_chunks * tk == lk
        scratch = [pltpu.VMEM((2, hps, tk, tq), F32)]
        once = pl.Buffered(1)
        args += [k, v_t]
        specs += [pl.BlockSpec((1, hps, lk, HEAD_PAD), lambda bi, hi, i: (bi, hi, 0, 0), pipeline_mode=once),
                  pl.BlockSpec((1, hps, V_ROWS, lk), lambda bi, hi, i: (bi, hi, 0, 0), pipeline_mode=once)]
    return pl.pallas_call(
        functools.partial(_attn_kernel, tk=tk, n_chunks=n_chunks),
        grid=(b, h // hps, l // tq),
        in_specs=specs,
        out_specs=pl.BlockSpec((1, hps * V_DIM, tq), lambda bi, hi, i: (bi, hi, i)),
        out_shape=jax.ShapeDtypeStruct((b, h * V_DIM, l), BF16),
        scratch_shapes=scratch,
        compiler_params=_cparams(("parallel", "parallel", "arbitrary")),
        name="attention",
    )(*args)


def _dft_step1_kernel(x_ref, c_ref, s_ref, yr_ref, yi_ref):
    xb = x_ref[0].astype(BF16)
    yr_ref[0] = jnp.dot(c_ref[...], xb, preferred_element_type=F32)
    yi_ref[0] = jnp.dot(s_ref[...], xb, preferred_element_type=F32)


def _dft_step2_kernel(yr_ref, yi_ref, tc_ref, ts_ref, c_ref, s_ref, cc_ref, sc_ref, o_ref):
    for j in range(SUBLANES):
        yr, yi = yr_ref[0, j], yi_ref[0, j]
        tc, ts = tc_ref[j], ts_ref[j]
        zr = (yr * tc - yi * ts).astype(BF16)
        zi = (yi * tc + yr * ts).astype(BF16)
        a = (jnp.dot(c_ref[...], zr, preferred_element_type=F32)
             - jnp.dot(s_ref[...], zi, preferred_element_type=F32))
        bm = (jnp.dot(s_ref[...], zr, preferred_element_type=F32)
              + jnp.dot(c_ref[...], zi, preferred_element_type=F32))
        o_ref[0, :, j, :] = (jnp.dot(a.astype(BF16), cc_ref[...], preferred_element_type=F32)
                             - jnp.dot(bm.astype(BF16), sc_ref[...], preferred_element_type=F32))


def _dft_dense_kernel(x_ref, c_ref, s_ref, cc_ref, sc_ref, o_ref):
    xb = x_ref[0].astype(BF16)
    a = jnp.dot(c_ref[...], xb, preferred_element_type=F32)
    bm = jnp.dot(s_ref[...], xb, preferred_element_type=F32)
    o_ref[0] = (jnp.dot(a.astype(BF16), cc_ref[...], preferred_element_type=F32)
                - jnp.dot(bm.astype(BF16), sc_ref[...], preferred_element_type=F32))


def _dft_mats(n):
    ang = 2.0 * np.pi * np.outer(np.arange(n), np.arange(n)) / n
    return np.cos(ang), np.sin(ang)


def _channel_dft(l):
    c, s = _dft_mats(GROUP_DIM)
    eye = np.eye(N_GROUPS)
    norm = 1.0 / math.sqrt(l * GROUP_DIM)
    return (jnp.asarray(np.kron(eye, c) * norm, BF16), jnp.asarray(np.kron(eye, s) * norm, BF16))


def _dft_factors(l):
    n1 = 1 << (int(math.log2(l)) // 2)
    return n1, l // n1


def _fourier(u_fno, lane_block=4096):
    b, l, c = u_fno.shape
    cc, sc = _channel_dft(l)
    full = lambda a: pl.BlockSpec(a.shape, lambda *_: (0,) * a.ndim)
    if l <= 512:
        cm, sm = _dft_mats(l)
        cm, sm = jnp.asarray(cm, BF16), jnp.asarray(sm, BF16)
        blk = pl.BlockSpec((1, l, c), lambda bi: (bi, 0, 0))
        return pl.pallas_call(
            _dft_dense_kernel, grid=(b,),
            in_specs=[blk, full(cm), full(sm), full(cc), full(sc)],
            out_specs=blk, out_shape=jax.ShapeDtypeStruct((b, l, c), F32),
            compiler_params=_cparams(("parallel",)), name="dft_dense",
        )(u_fno, cm, sm, cc, sc)
    n1, n2 = _dft_factors(l)
    c1, s1 = _dft_mats(n1)
    c2, s2 = _dft_mats(n2)
    c1, s1, c2, s2 = (jnp.asarray(m, BF16) for m in (c1, s1, c2, s2))
    ang = 2.0 * np.pi * np.outer(np.arange(n1), np.arange(n2)) / l
    tc = jnp.asarray(np.cos(ang)[:, :, None], F32)
    ts = jnp.asarray(np.sin(ang)[:, :, None], F32)
    w = n2 * c
    tn = min(lane_block, w)
    x2 = u_fno.reshape(b, n1, w)
    yspec = pl.BlockSpec((1, n1, tn), lambda bi, i: (bi, 0, i))
    yr, yi = pl.pallas_call(
        _dft_step1_kernel, grid=(b, w // tn),
        in_specs=[yspec, full(c1), full(s1)],
        out_specs=[yspec, yspec],
        out_shape=[jax.ShapeDtypeStruct((b, n1, w), F32)] * 2,
        compiler_params=_cparams(("parallel", "parallel")), name="dft_step1",
    )(x2, c1, s1)
    yr = yr.reshape(b, n1, n2, c)
    yi = yi.reshape(b, n1, n2, c)
    slab = pl.BlockSpec((1, SUBLANES, n2, c), lambda bi, i: (bi, i, 0, 0))
    tw = pl.BlockSpec((SUBLANES, n2, 1), lambda bi, i: (i, 0, 0))
    out = pl.pallas_call(
        _dft_step2_kernel, grid=(b, n1 // SUBLANES),
        in_specs=[slab, slab, tw, tw, full(c2), full(s2), full(cc), full(sc)],
        out_specs=pl.BlockSpec((1, n2, SUBLANES, c), lambda bi, i: (bi, 0, i, 0)),
        out_shape=jax.ShapeDtypeStruct((b, n2, n1, c), F32),
        compiler_params=_cparams(("parallel", "parallel")), name="dft_step2",
    )(yr, yi, tc, ts, c2, s2, cc, sc)
    return out.reshape(b, l, c)


POOL_HALO = SUBLANES


def _mixout_kernel(x_ref, gt_ref, up_ref, prev_ref, next_ref, f_ref, at_ref,
                   wp_ref, ps_ref, wf_ref, wo_ref, o_ref, *, seq_len):
    i = pl.program_id(1)
    tm = x_ref.shape[1]
    u = up_ref[0]
    prev = jnp.where(i > 0, prev_ref[0], 0.0)
    nxt = jnp.where(i < pl.num_programs(1) - 1, next_ref[0], 0.0)
    p = jnp.concatenate([prev, u, nxt], axis=0)
    n = tm + 2 * POOL_HALO
    s1 = pltpu.roll(p, 1, 0) + p
    s2 = pltpu.roll(s1, 1, 0) + pltpu.roll(s1, n - 1, 0)
    s4 = pltpu.roll(s2, 2, 0) + pltpu.roll(s2, n - 2, 0)
    s8 = pltpu.roll(s4, 4, 0) + pltpu.roll(s4, n - 4, 0)
    lane = lax.broadcasted_iota(jnp.int32, (tm, MIX_GROUP_WIDTH), 1)
    grp = lane // GROUP_DIM
    lo, hi = POOL_HALO, POOL_HALO + tm
    win = jnp.where(grp == 0, s1[lo:hi],
                    jnp.where(grp == 1, s2[lo:hi], jnp.where(grp == 2, s4[lo:hi], s8[lo:hi])))
    half = jnp.left_shift(1, grp)
    t = i * tm + lax.broadcasted_iota(jnp.int32, (tm, MIX_GROUP_WIDTH), 0)
    cnt = (jnp.minimum(t + half, seq_len) - jnp.maximum(t - half, 0)).astype(F32)
    dlt = win / cnt - u
    pool_y = jnp.dot(dlt.astype(BF16), wp_ref[...], preferred_element_type=F32) * ps_ref[...]
    fno_y = jnp.dot(f_ref[0].astype(BF16), wf_ref[...], preferred_element_type=F32)
    w = MIX_GROUP_WIDTH
    y = (jnp.dot(pool_y.astype(BF16), wo_ref[0:w, :], preferred_element_type=F32)
         + jnp.dot(fno_y.astype(BF16), wo_ref[w:2 * w, :], preferred_element_type=F32)
         + lax.dot_general(at_ref[0], wo_ref[2 * w:, :], (((0,), (0,)), ((), ())),
                           preferred_element_type=F32))
    o_ref[0] = x_ref[0] + gt_ref[0] * y


def _mixer_output(x, gt, u_pool, f, attn, wp_bd, pool_scale, wf_bd, w_out, tm):
    b, l, d = x.shape
    tm = min(tm, l)
    nb = tm // POOL_HALO
    last = l // POOL_HALO - 1
    full = lambda a: pl.BlockSpec(a.shape, lambda bi, i: (0,) * a.ndim)
    w = MIX_GROUP_WIDTH
    return pl.pallas_call(
        functools.partial(_mixout_kernel, seq_len=l),
        grid=(b, l // tm),
        in_specs=[pl.BlockSpec((1, tm, d), lambda bi, i: (bi, i, 0)),
                  pl.BlockSpec((1, 1, d), lambda bi, i: (bi, 0, 0)),
                  pl.BlockSpec((1, tm, w), lambda bi, i: (bi, i, 0)),
                  pl.BlockSpec((1, POOL_HALO, w), lambda bi, i: (bi, jnp.maximum(i * nb - 1, 0), 0)),
                  pl.BlockSpec((1, POOL_HALO, w), lambda bi, i: (bi, jnp.minimum((i + 1) * nb, last), 0)),
                  pl.BlockSpec((1, tm, w), lambda bi, i: (bi, i, 0)),
                  pl.BlockSpec((1, attn.shape[1], tm), lambda bi, i: (bi, 0, i)),
                  full(wp_bd), full(pool_scale), full(wf_bd), full(w_out)],
        out_specs=pl.BlockSpec((1, tm, d), lambda bi, i: (bi, i, 0)),
        out_shape=jax.ShapeDtypeStruct((b, l, d), F32),
        compiler_params=_cparams(("parallel", "arbitrary")),
        name="mixer_output",
    )(x, gt, u_pool, u_pool, u_pool, f, attn, wp_bd, pool_scale, wf_bd, w_out)


def _first_index_of_max(vals, iota, n):
    mx = jnp.max(vals, axis=0, keepdims=True)
    ix = jnp.min(jnp.where(vals == mx, iota, n), axis=0, keepdims=True)
    return mx, ix


def _slot_match(rank1, weights, piece_off, n_pieces, tm, one=1.0):
    row = lax.broadcasted_iota(jnp.int32, (N_EXPERTS, TILE_ROWS), 1).astype(F32)
    lo = piece_off * PIECE
    member = jnp.where(row >= lo, jnp.where(row < lo + n_pieces * PIECE, 1.0, 0.0), 0.0)
    lhs = jnp.concatenate([member, member * PIECE], axis=0).astype(BF16)
    rhs = jnp.concatenate([rank1, jnp.broadcast_to(piece_off, (N_EXPERTS, tm))], axis=0)
    if weights is not None:
        rhs = jnp.concatenate([rhs, jnp.concatenate([weights, jnp.zeros_like(weights)], axis=0)], axis=1)
    res = lax.dot_general(lhs, rhs.astype(BF16), (((0,), (0,)), ((), ())), preferred_element_type=F32)
    slot1 = lax.broadcasted_iota(jnp.int32, (TILE_ROWS, tm), 0).astype(F32) + 1.0
    match = jnp.where(res[:, :tm] == slot1, one, 0.0)
    return match, (res[:, tm:] if weights is not None else None)


def _ffn_front_kernel(*refs, n_lat, has_ctx):
    if has_ctx:
        x_ref, xc_ref = refs[:2]
        refs = refs[2:]
    else:
        x_ref, refs = refs[0], refs[1:]
    (sh_ref, sc_ref, gt_ref, g_ref, wr_ref, rb_ref, wsg_ref, wsu_ref, wsd_ref,
     xo_ref, xs_ref, route_ref, cnt_ref) = refs
    x = x_ref[0]
    if has_ctx:
        x = jnp.where(pl.program_id(0) < n_lat, x, xc_ref[0])
    tm = x.shape[0]
    h = _rms(x, g_ref[...]) * (1.0 + sc_ref[0]) + sh_ref[0]

    logits = lax.dot_general(wr_ref[...], h, (((1,), (1,)), ((), ())),
                             preferred_element_type=F32, precision=HIGHEST)
    aff = jax.nn.sigmoid(logits)
    sel = aff + rb_ref[...]
    e_iota = lax.broadcasted_iota(jnp.int32, (N_EXPERTS, tm), 0).astype(F32)
    neg = jnp.float32(-jnp.inf)
    gscores = []
    for g in range(N_EXPERT_GROUPS):
        blk = sel[g * EXPERTS_PER_GROUP:(g + 1) * EXPERTS_PER_GROUP]
        it = lax.broadcasted_iota(jnp.int32, blk.shape, 0).astype(F32)
        m1, i1 = _first_index_of_max(blk, it, float(EXPERTS_PER_GROUP))
        m2 = jnp.max(jnp.where(it == i1, neg, blk), axis=0, keepdims=True)
        gscores.append(m1 + m2)
    gs = jnp.concatenate(gscores, axis=0)
    g_iota = lax.broadcasted_iota(jnp.int32, gs.shape, 0).astype(F32)
    gself = jnp.zeros(gs.shape, F32)
    for _ in range(TOPK_GROUPS):
        _, ig = _first_index_of_max(gs, g_iota, float(N_EXPERT_GROUPS))
        hit = g_iota == ig
        gself = jnp.where(hit, 1.0, gself)
        gs = jnp.where(hit, neg, gs)
    emask = jnp.concatenate(
        [jnp.broadcast_to(gself[g:g + 1], (EXPERTS_PER_GROUP, tm)) for g in range(N_EXPERT_GROUPS)], axis=0)
    masked = jnp.where(emask > 0.5, sel, neg)
    hits, wts = [], []
    chosen = jnp.zeros((N_EXPERTS, tm), F32)
    for _ in range(TOP_K):
        _, ie = _first_index_of_max(masked, e_iota, float(N_EXPERTS))
        hit = e_iota == ie
        hits.append(hit)
        wts.append(jnp.sum(jnp.where(hit, aff, 0.0), axis=0, keepdims=True))
        chosen = jnp.where(hit, 1.0, chosen)
        masked = jnp.where(hit, neg, masked)
    wsum = wts[0]
    for w in wts[1:]:
        wsum = wsum + w
    scale = ROUTED_SCALE / wsum
    w_at_expert = jnp.zeros((N_EXPERTS, tm), F32)
    for hit, w in zip(hits, wts):
        w_at_expert = jnp.where(hit, w * scale, w_at_expert)

    r_io = lax.broadcasted_iota(jnp.int32, (tm, tm), 0)
    c_io = lax.broadcasted_iota(jnp.int32, (tm, tm), 1)
    tri = (r_io <= c_io).astype(BF16)
    incl = jnp.dot(chosen.astype(BF16), tri, preferred_element_type=F32)
    count = jnp.sum(chosen, axis=1, keepdims=True)
    n_pieces = jnp.floor((count + (PIECE - 1)) * (1.0 / PIECE))
    cnt_ref[0] = n_pieces
    e_r = lax.broadcasted_iota(jnp.int32, (N_EXPERTS, N_EXPERTS), 0)
    e_c = lax.broadcasted_iota(jnp.int32, (N_EXPERTS, N_EXPERTS), 1)
    before = (e_c < e_r).astype(BF16)
    piece_off = jnp.dot(before, jnp.broadcast_to(n_pieces, (N_EXPERTS, LANES)).astype(BF16),
                        preferred_element_type=F32)[:, :1]
    rank1 = chosen * incl
    route_ref[0, 0] = rank1
    route_ref[0, 1] = w_at_expert
    route_ref[0, 2] = jnp.broadcast_to(piece_off, (N_EXPERTS, tm))
    route_ref[0, 3] = jnp.broadcast_to(n_pieces, (N_EXPERTS, tm))

    keep = (pl.program_id(0) < pl.num_programs(0) - 1).astype(F32)
    onehot = _slot_match(rank1, None, piece_off, n_pieces, tm, one=keep)[0].astype(BF16)
    hb = h.astype(BF16)
    d = hb.shape[1]
    nc = 2 * LANES
    for j in range(d // nc):
        xs_ref[0, :, j * nc:(j + 1) * nc] = jnp.dot(
            onehot, hb[:, j * nc:(j + 1) * nc], preferred_element_type=F32).astype(BF16)

    a = jnp.dot(hb, wsg_ref[...], preferred_element_type=F32)
    bb = jnp.dot(hb, wsu_ref[...], preferred_element_type=F32)
    sh_out = jnp.dot((_silu(a) * bb).astype(BF16), wsd_ref[...], preferred_element_type=F32)
    xo_ref[0] = x + gt_ref[0] * sh_out


def _ffn_front(x, xc, sh, sc, gt, g, w_rt, rb, wsg, wsu, wsd):
    b, l, d = x.shape
    tm = MOE_TILE
    nt = l // tm
    n_lat = b * nt
    has_ctx = xc is not None
    ntc = xc.shape[1] // tm if has_ctx else 0
    n_all = n_lat + b * ntc
    full = lambda a: pl.BlockSpec(a.shape, lambda i: (0,) * a.ndim)

    real = lambda i: jnp.minimum(i, n_all - 1)

    def lat_map(i):
        j = jnp.minimum(i, n_lat - 1)
        return (j // nt, j % nt, 0)

    def ctx_map(i):
        j = jnp.clip(i - n_lat, 0, b * ntc - 1)
        return (j // ntc, j % ntc, 0)

    vec = pl.BlockSpec((1, 1, d), lambda i: (jnp.where(real(i) < n_lat, real(i) // nt, b), 0, 0))
    xspecs = [pl.BlockSpec((1, tm, d), lat_map)] + ([pl.BlockSpec((1, tm, d), ctx_map)] if has_ctx else [])
    xargs = [x] + ([xc] if has_ctx else [])
    return pl.pallas_call(
        functools.partial(_ffn_front_kernel, n_lat=n_lat, has_ctx=has_ctx),
        grid=(n_all + 1,),
        in_specs=xspecs + [vec, vec, vec, full(g), full(w_rt), full(rb), full(wsg), full(wsu), full(wsd)],
        out_specs=[pl.BlockSpec((1, tm, d), lambda i: (real(i), 0, 0)),
                   pl.BlockSpec((1, TILE_ROWS, d), lambda i: (i, 0, 0)),
                   pl.BlockSpec((1, 4, N_EXPERTS, tm), lambda i: (real(i), 0, 0, 0)),
                   pl.BlockSpec((1, N_EXPERTS, 1), lambda i: (real(i), 0, 0))],
        out_shape=[jax.ShapeDtypeStruct((n_all, tm, d), F32),
                   jax.ShapeDtypeStruct((n_all + 1, TILE_ROWS, d), BF16),
                   jax.ShapeDtypeStruct((n_all, 4, N_EXPERTS, tm), F32),
                   jax.ShapeDtypeStruct((n_all, N_EXPERTS, 1), F32)],
        compiler_params=_cparams(("arbitrary",)),
        name="ffn_front",
    )(*xargs, sh, sc, gt, g, w_rt, rb, wsg, wsu, wsd)


def _expert_kernel(bexp_ref, nused_ref, rlen_ref, rsrc_ref, rnext_ref, xs_ref, wgu_ref, wd_ref, y_ref,
                   state, dst, xbuf, ybuf, sem_in, sem_out, *, runs_per_expert, spare_piece):
    i = pl.program_id(0)
    n_used = nused_ref[0]
    end_run = rlen_ref.shape[0] - 1

    def copy_in(pid, slot, p):
        return pltpu.make_async_copy(xs_ref.at[pid], xbuf.at[slot, pl.ds(p * PIECE, PIECE), :], sem_in.at[slot])

    def copy_out(pid, slot, p):
        return pltpu.make_async_copy(ybuf.at[slot, pl.ds(p * PIECE, PIECE), :], y_ref.at[pid], sem_out.at[slot])

    def issue_gather(blk, slot):
        r_lo = bexp_ref[blk] * runs_per_expert
        r_end = r_lo + runs_per_expert
        fresh = state[0] < r_lo
        r = jnp.where(fresh, r_lo, state[0])
        o = jnp.where(fresh, 0, state[1])
        read_pid = jnp.int32(0)
        for p in range(BLOCK_PIECES):
            exhausted = o >= rlen_ref[jnp.minimum(r, end_run)]
            r = jnp.where(exhausted, rnext_ref[jnp.minimum(r + 1, end_run)], r)
            o = jnp.where(exhausted, 0, o)
            valid = r < r_end
            pid = rsrc_ref[jnp.minimum(r, end_run)] + o
            read_pid = jnp.where(valid, pid, read_pid)
            dst[slot, p] = jnp.where(valid, pid, spare_piece + slot * BLOCK_PIECES + p)
            copy_in(read_pid, slot, p).start()
            o = o + valid.astype(jnp.int32)
        state[0] = r
        state[1] = o

    def wait_gather(slot):
        pltpu.make_async_copy(xbuf.at[slot], xbuf.at[slot], sem_in.at[slot]).wait()

    def send_results(slot):
        for p in range(BLOCK_PIECES):
            copy_out(dst[slot, p], slot, p).start()

    def wait_results(slot):
        pltpu.make_async_copy(ybuf.at[slot], ybuf.at[slot], sem_out.at[slot]).wait()

    @pl.when(i < n_used)
    def _():
        slot = i % 2

        @pl.when(i == 0)
        def _():
            state[0] = 0
            state[1] = 0
            issue_gather(0, 0)

        @pl.when(i + 1 < n_used)
        def _():
            issue_gather(i + 1, (i + 1) % 2)

        wait_gather(slot)

        @pl.when(i >= 2)
        def _():
            wait_results(slot)

        gu = jnp.dot(xbuf[slot], wgu_ref[0], preferred_element_type=F32)
        f = gu.shape[1] // 2
        act = (_silu(gu[:, :f]) * gu[:, f:]).astype(BF16)
        ybuf[slot] = jnp.dot(act, wd_ref[0], preferred_element_type=F32).astype(BF16)
        send_results(slot)

        @pl.when(i == n_used - 1)
        def _():
            wait_results(slot)

            @pl.when(i >= 1)
            def _():
                wait_results(1 - slot)


def _experts(block_expert, n_used, run_len, run_src, run_next, xs_pieces, w_gu, w_down, runs_per_expert):
    n_blocks = block_expert.shape[0]
    _, _, d = xs_pieces.shape
    _, _, f2 = w_gu.shape
    grid_spec = pltpu.PrefetchScalarGridSpec(
        num_scalar_prefetch=5,
        grid=(n_blocks,),
        in_specs=[pl.BlockSpec(memory_space=pl.ANY),
                  pl.BlockSpec((1, d, f2), lambda i, be, *_: (be[i], 0, 0)),
                  pl.BlockSpec((1, f2 // 2, d), lambda i, be, *_: (be[i], 0, 0))],
        out_specs=pl.BlockSpec(memory_space=pl.ANY),
        scratch_shapes=[pltpu.SMEM((2,), jnp.int32),
                        pltpu.SMEM((2, BLOCK_PIECES), jnp.int32),
                        pltpu.VMEM((2, EXPERT_BLOCK, d), BF16),
                        pltpu.VMEM((2, EXPERT_BLOCK, d), BF16),
                        pltpu.SemaphoreType.DMA((2,)), pltpu.SemaphoreType.DMA((2,))],
    )
    return pl.pallas_call(
        functools.partial(_expert_kernel, runs_per_expert=runs_per_expert,
                          spare_piece=runs_per_expert * PIECES_PER_TILE),
        grid_spec=grid_spec,
        out_shape=jax.ShapeDtypeStruct(xs_pieces.shape, xs_pieces.dtype),
        input_output_aliases={5: 0},
        compiler_params=_cparams(("arbitrary",)),
        name="experts",
    )(block_expert, n_used, run_len, run_src, run_next, xs_pieces, w_gu, w_down)


def _combine_kernel(y_ref, route_ref, x_ref, gt_ref, gf_ref, o_ref, *, final):
    tm = x_ref.shape[1]
    match, w_rows = _slot_match(route_ref[0, 0], route_ref[0, 1], route_ref[0, 2][:, :1],
                                route_ref[0, 3][:, :1], tm)
    sel_t = (match * w_rows).astype(BF16)
    routed = lax.dot_general(sel_t, y_ref[0], (((0,), (0,)), ((), ())), preferred_element_type=F32)
    out = x_ref[0] + gt_ref[0] * routed
    if final:
        out = _rms(out, gf_ref[...])
    o_ref[0] = out


def _combine(y_tiles, route, x_tiles, gt, g_final, seq_shape, tile_off, final):
    b, l = seq_shape
    _, tm, d = x_tiles.shape
    nt = l // tm
    tile = lambda bi, i: tile_off + bi * nt + i
    return pl.pallas_call(
        functools.partial(_combine_kernel, final=final),
        grid=(b, nt),
        in_specs=[pl.BlockSpec((1, TILE_ROWS, d), lambda bi, i: (tile(bi, i), 0, 0)),
                  pl.BlockSpec((1, 4, N_EXPERTS, tm), lambda bi, i: (tile(bi, i), 0, 0, 0)),
                  pl.BlockSpec((1, tm, d), lambda bi, i: (tile(bi, i), 0, 0)),
                  pl.BlockSpec((1, 1, d), lambda bi, i: (bi, 0, 0)),
                  pl.BlockSpec(g_final.shape, lambda bi, i: (0, 0))],
        out_specs=pl.BlockSpec((1, tm, d), lambda bi, i: (bi, i, 0)),
        out_shape=jax.ShapeDtypeStruct((b, l, d), F32),
        compiler_params=_cparams(("parallel", "parallel")),
        name="combine",
    )(y_tiles, route, x_tiles, gt, g_final)


def _rope_tables(l):
    rows = l // GRID_W
    row = jnp.repeat(jnp.arange(rows, dtype=F32), GRID_W)
    col = jnp.tile(jnp.arange(GRID_W, dtype=F32), rows)
    inv_freq = ROPE_THETA ** (-jnp.arange(ROPE_FREQS, dtype=F32) * 2.0 / (2 * ROPE_FREQS))
    ar, ac = row[:, None] * inv_freq, col[:, None] * inv_freq
    ones = jnp.ones((l, ROPE_LANE0), F32)
    zpad = jnp.zeros((l, HEAD_PAD - ROPE_LANE0 - QK_ROPE), F32)
    cos_t = jnp.concatenate([ones, jnp.cos(ar), jnp.cos(ar), jnp.cos(ac), jnp.cos(ac), zpad], axis=1)
    sin_t = jnp.concatenate([0.0 * ones, -jnp.sin(ar), jnp.sin(ar), -jnp.sin(ac), jnp.sin(ac), zpad], axis=1)
    return cos_t, sin_t


def _identity_tables(l):
    lane = jnp.arange(HEAD_PAD)
    cos_t = jnp.broadcast_to((lane < ROPE_LANE0 + QK_ROPE).astype(F32), (l, HEAD_PAD))
    return cos_t, jnp.zeros((l, HEAD_PAD), F32)


def _block_diag(w):
    g, a, b = w.shape
    out = jnp.zeros((g * a, g * b), w.dtype)
    for i in range(g):
        out = out.at[i * a:(i + 1) * a, i * b:(i + 1) * b].set(w[i])
    return out


def _prep_layer(w_in, w_pool, w_fno, w_uq, w_ukv):
    d = w_in.shape[0]
    kr = jnp.zeros((d, HEAD_PAD), w_in.dtype).at[:, ROPE_LANE0:ROPE_LANE0 + QK_ROPE].set(w_in[:, COL_KR:])
    w_in_p = jnp.concatenate([w_in[:, :COL_KR], kr], axis=1).astype(BF16)
    qk = QK_NOPE + QK_ROPE
    wq = w_uq.reshape(Q_LORA, N_HEADS, qk)
    wq = jnp.pad(wq, ((0, 0), (0, 0), (0, HEAD_PAD - qk))).reshape(Q_LORA, N_HEADS * HEAD_PAD)
    wkv = w_ukv.reshape(KV_LORA, N_HEADS, QK_NOPE + V_DIM)
    wk = jnp.pad(wkv[..., :QK_NOPE], ((0, 0), (0, 0), (0, HEAD_PAD - QK_NOPE))).reshape(KV_LORA, -1)
    wv = jnp.pad(wkv[..., QK_NOPE:], ((0, 0), (0, 0), (0, V_ROWS - V_DIM))).reshape(KV_LORA, -1)
    return (w_in_p, wq.T.astype(BF16), wk.astype(BF16), wv.T.astype(BF16),
            _block_diag(w_pool).astype(BF16), _block_diag(w_fno).astype(BF16))


def _run_tables(n_pieces):
    n_all = n_pieces.shape[0]
    npc = n_pieces.astype(jnp.int32)
    piece_off = jnp.cumsum(npc, axis=1) - npc
    run_src = (jnp.arange(n_all, dtype=jnp.int32)[:, None] * PIECES_PER_TILE + piece_off).T.reshape(-1)
    run_len = npc.T.reshape(-1)
    n_runs = run_len.shape[0]
    run_id = jnp.arange(n_runs, dtype=jnp.int32)
    run_next = lax.cummin(jnp.where(run_len > 0, run_id, n_runs), axis=0, reverse=True)
    pad1 = lambda a, v: jnp.concatenate([a, jnp.full((1,), v, jnp.int32)])
    run_len, run_src, run_next = pad1(run_len, 0), pad1(run_src, 0), pad1(run_next, n_runs)
    blocks = (jnp.sum(npc, axis=0) + BLOCK_PIECES - 1) // BLOCK_PIECES
    block_end = jnp.cumsum(blocks)
    max_blocks = n_all * PIECES_PER_TILE // BLOCK_PIECES + N_EXPERTS
    blk = jnp.arange(max_blocks, dtype=jnp.int32)
    block_expert = jnp.minimum(jnp.sum((block_end[None, :] <= blk[:, None]).astype(jnp.int32), axis=1),
                               N_EXPERTS - 1)
    return block_expert, block_end[-1:], run_len, run_src, run_next


TM_PROJ = 256
TM_MIX = 256
ATTN_TQ = 2048
ATTN_TK = 512


def kernel(x, c, ctx, c_ctx, w_mod, b_mod, g_mix, g_ffn, w_in, w_pool, pool_scale, w_fno, g_q, w_uq,
           g_kv, w_ukv, w_out, w_router, router_bias, w_gate, w_up, w_down, w_sh_gate, w_sh_up,
           w_sh_down, g_final):
    b, l, d = x.shape
    lc = ctx.shape[1]
    depth = w_mod.shape[0]
    cc = jnp.zeros((SUBLANES, d), F32).at[:b].set(c).at[b].set(c_ctx)
    mods = _modulation(cc, w_mod, b_mod)
    with_transposed = lambda cs: (cs[0], cs[1], cs[0].T, cs[1].T)
    tab_lat = with_transposed(_rope_tables(l))
    tab_ctx = with_transposed(_identity_tables(lc))
    xc = ctx
    row = lambda v: v.reshape(1, -1)
    for li in range(depth):
        last = li == depth - 1
        m = mods[li].reshape(SUBLANES, N_MOD, d)
        lat = [m[:b, j][:, None, :] for j in range(N_MOD)]
        cm = [jnp.broadcast_to(m[b, j][None, None, :], (b, 1, d)) for j in range(N_MOD)]
        w_in_p, wq_t, wk_p, wv_t, wp_bd, wf_bd = _prep_layer(w_in[li], w_pool[li], w_fno[li], w_uq[li], w_ukv[li])
        w_out_b = w_out[li].astype(BF16)
        gq, gkv, gm = row(g_q[li]), row(g_kv[li]), row(g_mix[li])

        up, uf, q, k, v = _in_projection(x, lat[0], lat[1], gm, w_in_p, gq, wq_t, gkv, wk_p, wv_t,
                                         tab_lat, TM_PROJ)
        upc, ufc, qc, kc, vc = _in_projection(xc, cm[0], cm[1], gm, w_in_p, gq, wq_t, gkv, wk_p, wv_t,
                                              tab_ctx, TM_PROJ)
        attn = _attention(q, kc, vc, k, v, tq=ATTN_TQ, tk=ATTN_TK)
        f = _fourier(uf)
        ps = row(pool_scale[li])
        x = _mixer_output(x, lat[2], up, f, attn, wp_bd, ps, wf_bd, w_out_b, TM_MIX)
        if not last:
            attn_c = _attention(qc, kc, vc, tq=ATTN_TQ, tk=ATTN_TK)
            fc = _fourier(ufc)
            xc = _mixer_output(xc, cm[2], upc, fc, attn_c, wp_bd, ps, wf_bd, w_out_b, TM_MIX)

        w_rt = w_router[li].T
        rb = router_bias[li].reshape(N_EXPERTS, 1)
        wsg, wsu, wsd = w_sh_gate[li].astype(BF16), w_sh_up[li].astype(BF16), w_sh_down[li].astype(BF16)
        with_ctx = lambda j: jnp.concatenate([lat[j], cm[j][:1]], axis=0)
        x_tiles, xs, route, n_pieces = _ffn_front(
            x, None if last else xc, with_ctx(3), with_ctx(4), with_ctx(5), row(g_ffn[li]), w_rt, rb, wsg, wsu, wsd)
        n_all = xs.shape[0] - 1
        block_expert, n_used, run_len, run_src, run_next = _run_tables(n_pieces[..., 0])
        w_gu = jnp.concatenate([w_gate[li], w_up[li]], axis=-1).astype(BF16)
        y = _experts(block_expert, n_used, run_len, run_src, run_next,
                     xs.reshape((n_all + 1) * PIECES_PER_TILE, PIECE, d), w_gu, w_down[li].astype(BF16), n_all)
        y = y.reshape(n_all + 1, TILE_ROWS, d)
        x = _combine(y, route, x_tiles, lat[5], row(g_final), (b, l), 0, last)
        if not last:
            xc = _combine(y, route, x_tiles, cm[5], row(g_final), (b, lc), b * l // MOE_TILE, False)
    return x
```

```python
import functools
import math

import jax
import jax.numpy as jnp
import numpy as np
from jax import lax
from jax.experimental import pallas as pl
from jax.experimental.pallas import tpu as pltpu

F32 = jnp.float32
BF16 = jnp.bfloat16
HIGHEST = lax.Precision.HIGHEST

EPS = 1e-6
N_MOD = 6
GRID_W = 64
POOL_WINDOWS = (2, 4, 8, 16)
GROUP_DIM = 64
N_GROUPS = 4
MIX_GROUP_WIDTH = N_GROUPS * GROUP_DIM
N_HEADS = 8
QK_NOPE = 64
QK_ROPE = 32
V_DIM = 64
Q_LORA = 384
KV_LORA = 256
ROPE_FREQS = QK_ROPE // 4
ROPE_THETA = 10000.0
SOFTMAX_SCALE = (QK_NOPE + QK_ROPE) ** -0.5
N_EXPERTS = 64
TOP_K = 8
N_EXPERT_GROUPS = 8
TOPK_GROUPS = 4
EXPERTS_PER_GROUP = N_EXPERTS // N_EXPERT_GROUPS
ROUTED_SCALE = 2.5

LANES = 128
SUBLANES = 8
HEAD_PAD = LANES
VMEM_LIMIT = 56 * 1024 * 1024

COL_FNO = MIX_GROUP_WIDTH
COL_Q = COL_FNO + MIX_GROUP_WIDTH
COL_KV = COL_Q + Q_LORA
COL_KR = COL_KV + KV_LORA
IN_PAD = COL_KR + HEAD_PAD
ROPE_LANE0 = QK_NOPE

MOE_TILE = 256
PIECE = 16
PIECES_PER_TILE = MOE_TILE * TOP_K // PIECE + N_EXPERTS
TILE_ROWS = PIECES_PER_TILE * PIECE
BLOCK_PIECES = 32
EXPERT_BLOCK = BLOCK_PIECES * PIECE


def _cparams(sem, vmem=VMEM_LIMIT):
    return pltpu.CompilerParams(dimension_semantics=sem, vmem_limit_bytes=vmem)


def _rms(x, g):
    return x * lax.rsqrt(jnp.mean(x * x, axis=-1, keepdims=True) + EPS) * g


def _silu(x):
    return x * jax.nn.sigmoid(x)


def _mod_kernel(c_ref, w_ref, b_ref, o_ref):
    s = _silu(c_ref[...])
    o_ref[0] = jnp.dot(s, w_ref[0], preferred_element_type=F32, precision=HIGHEST) + b_ref[0]


def _modulation(cc, w_mod, b_mod):
    depth, d, nd = w_mod.shape
    n_chunks = nd // d
    return pl.pallas_call(
        _mod_kernel,
        grid=(depth, n_chunks),
        in_specs=[pl.BlockSpec((SUBLANES, d), lambda l, j: (0, 0)),
                  pl.BlockSpec((1, d, d), lambda l, j: (l, 0, j)),
                  pl.BlockSpec((1, 1, d), lambda l, j: (l, 0, j))],
        out_specs=pl.BlockSpec((1, SUBLANES, d), lambda l, j: (l, 0, j)),
        out_shape=jax.ShapeDtypeStruct((depth, SUBLANES, nd), F32),
        compiler_params=_cparams(("parallel", "parallel")),
        name="modulation",
    )(cc, w_mod, b_mod.reshape(depth, 1, nd))


def _rope(t, c, s):
    lane = lax.broadcasted_iota(jnp.int32, t.shape, 1)
    first_half = (lane % (2 * ROPE_FREQS)) < ROPE_FREQS
    partner = jnp.where(first_half, pltpu.roll(t, LANES - ROPE_FREQS, 1), pltpu.roll(t, ROPE_FREQS, 1))
    return t * c + partner * s


def _rope_rows(t, c, s):
    rw = lax.broadcasted_iota(jnp.int32, t.shape, 0)
    first_half = (rw % (2 * ROPE_FREQS)) < ROPE_FREQS
    n = t.shape[0]
    partner = jnp.where(first_half, pltpu.roll(t, n - ROPE_FREQS, 0), pltpu.roll(t, ROPE_FREQS, 0))
    return t * c + partner * s


V_ROWS = V_DIM + 16


def _inproj_kernel(x_ref, sh_ref, sc_ref, g_ref, win_ref, gq_ref, wuqt_ref, gkv_ref, wuk_ref, wuvt_ref,
                   cos_ref, sin_ref, cost_ref, sint_ref, up_ref, uf_ref, q_ref, k_ref, v_ref):
    x = x_ref[0]
    h = _rms(x, g_ref[...]) * (1.0 + sc_ref[0]) + sh_ref[0]
    u = jnp.dot(h.astype(BF16), win_ref[...], preferred_element_type=F32)
    up_ref[0] = u[:, :COL_FNO]
    uf_ref[0] = u[:, COL_FNO:COL_Q]
    nt = (((1,), (1,)), ((), ()))
    cq = _rms(u[:, COL_Q:COL_KV], gq_ref[...]).astype(BF16)
    qt = lax.dot_general(wuqt_ref[...], cq, nt, preferred_element_type=F32)
    ckv = _rms(u[:, COL_KV:COL_KR], gkv_ref[...]).astype(BF16)
    kn = jnp.dot(ckv, wuk_ref[...], preferred_element_type=F32)
    vt = lax.dot_general(wuvt_ref[...], ckv, nt, preferred_element_type=F32)
    kr = _rope(u[:, COL_KR:IN_PAD], cos_ref[...], sin_ref[...])
    cos_t, sin_t = cost_ref[...], sint_ref[...]
    tm = x.shape[0]
    ones_row = (lax.broadcasted_iota(jnp.int32, (V_ROWS, tm), 0) == V_DIM).astype(F32)
    q_scale = SOFTMAX_SCALE * math.log2(math.e)
    for hd in range(N_HEADS):
        lo = hd * HEAD_PAD
        q_ref[0, hd] = (_rope_rows(qt[lo:lo + HEAD_PAD], cos_t, sin_t) * q_scale).astype(BF16)
        k_ref[0, hd] = (kn[:, lo:lo + HEAD_PAD] + kr).astype(BF16)
        v_ref[0, hd] = (vt[hd * V_ROWS:(hd + 1) * V_ROWS] + ones_row).astype(BF16)


def _in_projection(x, sh, sc, g, w_in_p, g_q, w_uq_t, g_kv, w_uk_p, w_uv_t, tables, tm):
    b, l, d = x.shape
    tm = min(tm, l)
    cos_r, sin_r, cos_c, sin_c = tables
    full = lambda a: pl.BlockSpec(a.shape, lambda bi, i: (0,) * a.ndim)
    vec = pl.BlockSpec((1, 1, d), lambda bi, i: (bi, 0, 0))
    rtab = pl.BlockSpec((tm, HEAD_PAD), lambda bi, i: (i, 0))
    ctab = pl.BlockSpec((HEAD_PAD, tm), lambda bi, i: (0, i))
    return pl.pallas_call(
        _inproj_kernel,
        grid=(b, l // tm),
        in_specs=[pl.BlockSpec((1, tm, d), lambda bi, i: (bi, i, 0)), vec, vec,
                  full(g), full(w_in_p), full(g_q), full(w_uq_t), full(g_kv), full(w_uk_p), full(w_uv_t),
                  rtab, rtab, ctab, ctab],
        out_specs=[pl.BlockSpec((1, tm, MIX_GROUP_WIDTH), lambda bi, i: (bi, i, 0)),
                   pl.BlockSpec((1, tm, MIX_GROUP_WIDTH), lambda bi, i: (bi, i, 0)),
                   pl.BlockSpec((1, N_HEADS, HEAD_PAD, tm), lambda bi, i: (bi, 0, 0, i)),
                   pl.BlockSpec((1, N_HEADS, tm, HEAD_PAD), lambda bi, i: (bi, 0, i, 0)),
                   pl.BlockSpec((1, N_HEADS, V_ROWS, tm), lambda bi, i: (bi, 0, 0, i))],
        out_shape=[jax.ShapeDtypeStruct((b, l, MIX_GROUP_WIDTH), F32),
                   jax.ShapeDtypeStruct((b, l, MIX_GROUP_WIDTH), F32),
                   jax.ShapeDtypeStruct((b, N_HEADS, HEAD_PAD, l), BF16),
                   jax.ShapeDtypeStruct((b, N_HEADS, l, HEAD_PAD), BF16),
                   jax.ShapeDtypeStruct((b, N_HEADS, V_ROWS, l), BF16)],
        compiler_params=_cparams(("parallel", "parallel")),
        name="in_projection",
    )(x, sh, sc, g, w_in_p, g_q, w_uq_t, g_kv, w_uk_p, w_uv_t, cos_r, sin_r, cos_c, sin_c)


ATTN_HEADS_PER_STEP = 2


def _attn_kernel(*refs, tk, n_chunks):
    if n_chunks:
        q_ref, kc_ref, vc_ref, k_ref, v_ref, o_ref, s_scr = refs
    else:
        q_ref, kc_ref, vc_ref, o_ref = refs
    tq = q_ref.shape[3]
    heads = range(ATTN_HEADS_PER_STEP)
    qs = [q_ref[0, hh] for hh in heads]

    def scores(hh, kc):
        return jnp.dot(kc, qs[hh], preferred_element_type=F32)

    def absorb(st, vt, m, acc):
        m_new = jnp.maximum(m, jnp.max(st, axis=0, keepdims=True))
        alpha = jnp.exp2(m - m_new)
        pt = jnp.exp2(st - m_new).astype(BF16)
        return m_new, alpha * acc + jnp.dot(vt, pt, preferred_element_type=F32)

    carry = tuple(absorb(scores(hh, kc_ref[0, hh]), vc_ref[0, hh],
                         jnp.full((1, tq), -1e30, F32), jnp.zeros((V_ROWS, tq), F32)) for hh in heads)
    if n_chunks:
        def produce(slot, chunk):
            off = pl.multiple_of(chunk * tk, tk)
            for hh in heads:
                s_scr[slot, hh] = scores(hh, k_ref[0, hh, pl.ds(off, tk), :])

        def consume(slot, chunk, carry):
            off = pl.multiple_of(chunk * tk, tk)
            return tuple(absorb(s_scr[slot, hh], v_ref[0, hh, :, pl.ds(off, tk)], *carry[hh]) for hh in heads)

        produce(0, 0)

        def body(j, carry):
            a = 2 * j
            produce(1, a + 1)
            carry = consume(0, a, carry)
            produce(0, jnp.minimum(a + 2, n_chunks - 1))
            return consume(1, a + 1, carry)

        carry = lax.fori_loop(0, n_chunks // 2, body, carry)
    for hh in heads:
        acc = carry[hh][1]
        out = acc[:V_DIM] / acc[V_DIM:V_DIM + 1]
        o_ref[0, hh * V_DIM:(hh + 1) * V_DIM, :] = out.astype(o_ref.dtype)


def _attention(q_t, k_ctx, v_ctx_t, k=None, v_t=None, *, tq, tk):
    b, h, _, l = q_t.shape
    lc = k_ctx.shape[2]
    tq = min(tq, l)
    hps = ATTN_HEADS_PER_STEP
    qspec = pl.BlockSpec((1, hps, HEAD_PAD, tq), lambda bi, hi, i: (bi, hi, 0, i))
    kcspec = pl.BlockSpec((1, hps, lc, HEAD_PAD), lambda bi, hi, i: (bi, hi, 0, 0))
    vcspec = pl.BlockSpec((1, hps, V_ROWS, lc), lambda bi, hi, i: (bi, hi, 0, 0))
    args, specs, n_chunks, scratch = [q_t, k_ctx, v_ctx_t], [qspec, kcspec, vcspec], 0, []
    if k is not None:
        lk = k.shape[2]
        tk = min(tk, lk // 2)
        n_chunks = lk // tk
        assert n_chunks % 2 == 0 and n_chunks * tk == lk
        scratch = [pltpu.VMEM((2, hps, tk, tq), F32)]
        once = pl.Buffered(1)
        args += [k, v_t]
        specs += [pl.BlockSpec((1, hps, lk, HEAD_PAD), lambda bi, hi, i: (bi, hi, 0, 0), pipeline_mode=once),
                  pl.BlockSpec((1, hps, V_ROWS, lk), lambda bi, hi, i: (bi, hi, 0, 0), pipeline_mode=once)]
    return pl.pallas_call(
        functools.partial(_attn_kernel, tk=tk, n_chunks=n_chunks),
        grid=(b, h // hps, l // tq),
        in_specs=specs,
        out_specs=pl.BlockSpec((1, hps * V_DIM, tq), lambda bi, hi, i: (bi, hi, i)),
        out_shape=jax.ShapeDtypeStruct((b, h * V_DIM, l), BF16),
        scratch_shapes=scratch,
        compiler_params=_cparams(("parallel", "parallel", "arbitrary")),
        name="attention",
    )(*args)


def _dft_step1_kernel(x_ref, c_ref, s_ref, yr_ref, yi_ref):
    xb = x_ref[0].astype(BF16)
    yr_ref[0] = jnp.dot(c_ref[...], xb, preferred_element_type=F32)
    yi_ref[0] = jnp.dot(s_ref[...], xb, preferred_element_type=F32)


def _dft_step2_kernel(yr_ref, yi_ref, tc_ref, ts_ref, c_ref, s_ref, cc_ref, sc_ref, o_ref):
    for j in range(SUBLANES):
        yr, yi = yr_ref[0, j], yi_ref[0, j]
        tc, ts = tc_ref[j], ts_ref[j]
        zr = (yr * tc - yi * ts).astype(BF16)
        zi = (yi * tc + yr * ts).astype(BF16)
        a = (jnp.dot(c_ref[...], zr, preferred_element_type=F32)
             - jnp.dot(s_ref[...], zi, preferred_element_type=F32))
        bm = (jnp.dot(s_ref[...], zr, preferred_element_type=F32)
              + jnp.dot(c_ref[...], zi, preferred_element_type=F32))
        o_ref[0, :, j, :] = (jnp.dot(a.astype(BF16), cc_ref[...], preferred_element_type=F32)
                             - jnp.dot(bm.astype(BF16), sc_ref[...], preferred_element_type=F32))


def _dft_dense_kernel(x_ref, c_ref, s_ref, cc_ref, sc_ref, o_ref):
    xb = x_ref[0].astype(BF16)
    a = jnp.dot(c_ref[...], xb, preferred_element_type=F32)
    bm = jnp.dot(s_ref[...], xb, preferred_element_type=F32)
    o_ref[0] = (jnp.dot(a.astype(BF16), cc_ref[...], preferred_element_type=F32)
                - jnp.dot(bm.astype(BF16), sc_ref[...], preferred_element_type=F32))


def _dft_mats(n):
    ang = 2.0 * np.pi * np.outer(np.arange(n), np.arange(n)) / n
    return np.cos(ang), np.sin(ang)


def _channel_dft(l):
    c, s = _dft_mats(GROUP_DIM)
    eye = np.eye(N_GROUPS)
    norm = 1.0 / math.sqrt(l * GROUP_DIM)
    return (jnp.asarray(np.kron(eye, c) * norm, BF16), jnp.asarray(np.kron(eye, s) * norm, BF16))


def _dft_factors(l):
    n1 = 1 << (int(math.log2(l)) // 2)
    return n1, l // n1


def _fourier(u_fno, lane_block=4096):
    b, l, c = u_fno.shape
    cc, sc = _channel_dft(l)
    full = lambda a: pl.BlockSpec(a.shape, lambda *_: (0,) * a.ndim)
    if l <= 512:
        cm, sm = _dft_mats(l)
        cm, sm = jnp.asarray(cm, BF16), jnp.asarray(sm, BF16)
        blk = pl.BlockSpec((1, l, c), lambda bi: (bi, 0, 0))
        return pl.pallas_call(
            _dft_dense_kernel, grid=(b,),
            in_specs=[blk, full(cm), full(sm), full(cc), full(sc)],
            out_specs=blk, out_shape=jax.ShapeDtypeStruct((b, l, c), F32),
            compiler_params=_cparams(("parallel",)), name="dft_dense",
        )(u_fno, cm, sm, cc, sc)
    n1, n2 = _dft_factors(l)
    c1, s1 = _dft_mats(n1)
    c2, s2 = _dft_mats(n2)
    c1, s1, c2, s2 = (jnp.asarray(m, BF16) for m in (c1, s1, c2, s2))
    ang = 2.0 * np.pi * np.outer(np.arange(n1), np.arange(n2)) / l
    tc = jnp.asarray(np.cos(ang)[:, :, None], F32)
    ts = jnp.asarray(np.sin(ang)[:, :, None], F32)
    w = n2 * c
    tn = min(lane_block, w)
    x2 = u_fno.reshape(b, n1, w)
    yspec = pl.BlockSpec((1, n1, tn), lambda bi, i: (bi, 0, i))
    yr, yi = pl.pallas_call(
        _dft_step1_kernel, grid=(b, w // tn),
        in_specs=[yspec, full(c1), full(s1)],
        out_specs=[yspec, yspec],
        out_shape=[jax.ShapeDtypeStruct((b, n1, w), F32)] * 2,
        compiler_params=_cparams(("parallel", "parallel")), name="dft_step1",
    )(x2, c1, s1)
    yr = yr.reshape(b, n1, n2, c)
    yi = yi.reshape(b, n1, n2, c)
    slab = pl.BlockSpec((1, SUBLANES, n2, c), lambda bi, i: (bi, i, 0, 0))
    tw = pl.BlockSpec((SUBLANES, n2, 1), lambda bi, i: (i, 0, 0))
    out = pl.pallas_call(
        _dft_step2_kernel, grid=(b, n1 // SUBLANES),
        in_specs=[slab, slab, tw, tw, full(c2), full(s2), full(cc), full(sc)],
        out_specs=pl.BlockSpec((1, n2, SUBLANES, c), lambda bi, i: (bi, 0, i, 0)),
        out_shape=jax.ShapeDtypeStruct((b, n2, n1, c), F32),
        compiler_params=_cparams(("parallel", "parallel")), name="dft_step2",
    )(yr, yi, tc, ts, c2, s2, cc, sc)
    return out.reshape(b, l, c)


POOL_HALO = SUBLANES


def _mixout_kernel(x_ref, gt_ref, up_ref, prev_ref, next_ref, f_ref, at_ref,
                   wp_ref, ps_ref, wf_ref, wo_ref, o_ref, *, seq_len):
    i = pl.program_id(1)
    tm = x_ref.shape[1]
    u = up_ref[0]
    prev = jnp.where(i > 0, prev_ref[0], 0.0)
    nxt = jnp.where(i < pl.num_programs(1) - 1, next_ref[0], 0.0)
    p = jnp.concatenate([prev, u, nxt], axis=0)
    n = tm + 2 * POOL_HALO
    s1 = pltpu.roll(p, 1, 0) + p
    s2 = pltpu.roll(s1, 1, 0) + pltpu.roll(s1, n - 1, 0)
    s4 = pltpu.roll(s2, 2, 0) + pltpu.roll(s2, n - 2, 0)
    s8 = pltpu.roll(s4, 4, 0) + pltpu.roll(s4, n - 4, 0)
    lane = lax.broadcasted_iota(jnp.int32, (tm, MIX_GROUP_WIDTH), 1)
    grp = lane // GROUP_DIM
    lo, hi = POOL_HALO, POOL_HALO + tm
    win = jnp.where(grp == 0, s1[lo:hi],
                    jnp.where(grp == 1, s2[lo:hi], jnp.where(grp == 2, s4[lo:hi], s8[lo:hi])))
    half = jnp.left_shift(1, grp)
    t = i * tm + lax.broadcasted_iota(jnp.int32, (tm, MIX_GROUP_WIDTH), 0)
    cnt = (jnp.minimum(t + half, seq_len) - jnp.maximum(t - half, 0)).astype(F32)
    dlt = win / cnt - u
    pool_y = jnp.dot(dlt.astype(BF16), wp_ref[...], preferred_element_type=F32) * ps_ref[...]
    fno_y = jnp.dot(f_ref[0].astype(BF16), wf_ref[...], preferred_element_type=F32)
    w = MIX_GROUP_WIDTH
    y = (jnp.dot(pool_y.astype(BF16), wo_ref[0:w, :], preferred_element_type=F32)
         + jnp.dot(fno_y.astype(BF16), wo_ref[w:2 * w, :], preferred_element_type=F32)
         + lax.dot_general(at_ref[0], wo_ref[2 * w:, :], (((0,), (0,)), ((), ())),
                           preferred_element_type=F32))
    o_ref[0] = x_ref[0] + gt_ref[0] * y


def _mixer_output(x, gt, u_pool, f, attn, wp_bd, pool_scale, wf_bd, w_out, tm):
    b, l, d = x.shape
    tm = min(tm, l)
    nb = tm // POOL_HALO
    last = l // POOL_HALO - 1
    full = lambda a: pl.BlockSpec(a.shape, lambda bi, i: (0,) * a.ndim)
    w = MIX_GROUP_WIDTH
    return pl.pallas_call(
        functools.partial(_mixout_kernel, seq_len=l),
        grid=(b, l // tm),
        in_specs=[pl.BlockSpec((1, tm, d), lambda bi, i: (bi, i, 0)),
                  pl.BlockSpec((1, 1, d), lambda bi, i: (bi, 0, 0)),
                  pl.BlockSpec((1, tm, w), lambda bi, i: (bi, i, 0)),
                  pl.BlockSpec((1, POOL_HALO, w), lambda bi, i: (bi, jnp.maximum(i * nb - 1, 0), 0)),
                  pl.BlockSpec((1, POOL_HALO, w), lambda bi, i: (bi, jnp.minimum((i + 1) * nb, last), 0)),
                  pl.BlockSpec((1, tm, w), lambda bi, i: (bi, i, 0)),
                  pl.BlockSpec((1, attn.shape[1], tm), lambda bi, i: (bi, 0, i)),
                  full(wp_bd), full(pool_scale), full(wf_bd), full(w_out)],
        out_specs=pl.BlockSpec((1, tm, d), lambda bi, i: (bi, i, 0)),
        out_shape=jax.ShapeDtypeStruct((b, l, d), F32),
        compiler_params=_cparams(("parallel", "arbitrary")),
        name="mixer_output",
    )(x, gt, u_pool, u_pool, u_pool, f, attn, wp_bd, pool_scale, wf_bd, w_out)


def _first_index_of_max(vals, iota, n):
    mx = jnp.max(vals, axis=0, keepdims=True)
    ix = jnp.min(jnp.where(vals == mx, iota, n), axis=0, keepdims=True)
    return mx, ix


def _slot_match(rank1, weights, piece_off, n_pieces, tm, one=1.0):
    row = lax.broadcasted_iota(jnp.int32, (N_EXPERTS, TILE_ROWS), 1).astype(F32)
    lo = piece_off * PIECE
    member = jnp.where(row >= lo, jnp.where(row < lo + n_pieces * PIECE, 1.0, 0.0), 0.0)
    lhs = jnp.concatenate([member, member * PIECE], axis=0).astype(BF16)
    rhs = jnp.concatenate([rank1, jnp.broadcast_to(piece_off, (N_EXPERTS, tm))], axis=0)
    if weights is not None:
        rhs = jnp.concatenate([rhs, jnp.concatenate([weights, jnp.zeros_like(weights)], axis=0)], axis=1)
    res = lax.dot_general(lhs, rhs.astype(BF16), (((0,), (0,)), ((), ())), preferred_element_type=F32)
    slot1 = lax.broadcasted_iota(jnp.int32, (TILE_ROWS, tm), 0).astype(F32) + 1.0
    match = jnp.where(res[:, :tm] == slot1, one, 0.0)
    return match, (res[:, tm:] if weights is not None else None)


def _ffn_front_kernel(*refs, n_lat, has_ctx):
    if has_ctx:
        x_ref, xc_ref = refs[:2]
        refs = refs[2:]
    else:
        x_ref, refs = refs[0], refs[1:]
    (sh_ref, sc_ref, gt_ref, g_ref, wr_ref, rb_ref, wsg_ref, wsu_ref, wsd_ref,
     xo_ref, xs_ref, route_ref, cnt_ref) = refs
    x = x_ref[0]
    if has_ctx:
        x = jnp.where(pl.program_id(0) < n_lat, x, xc_ref[0])
    tm = x.shape[0]
    h = _rms(x, g_ref[...]) * (1.0 + sc_ref[0]) + sh_ref[0]

    logits = lax.dot_general(wr_ref[...], h, (((1,), (1,)), ((), ())),
                             preferred_element_type=F32, precision=HIGHEST)
    aff = jax.nn.sigmoid(logits)
    sel = aff + rb_ref[...]
    e_iota = lax.broadcasted_iota(jnp.int32, (N_EXPERTS, tm), 0).astype(F32)
    neg = jnp.float32(-jnp.inf)
    gscores = []
    for g in range(N_EXPERT_GROUPS):
        blk = sel[g * EXPERTS_PER_GROUP:(g + 1) * EXPERTS_PER_GROUP]
        it = lax.broadcasted_iota(jnp.int32, blk.shape, 0).astype(F32)
        m1, i1 = _first_index_of_max(blk, it, float(EXPERTS_PER_GROUP))
        m2 = jnp.max(jnp.where(it == i1, neg, blk), axis=0, keepdims=True)
        gscores.append(m1 + m2)
    gs = jnp.concatenate(gscores, axis=0)
    g_iota = lax.broadcasted_iota(jnp.int32, gs.shape, 0).astype(F32)
    gself = jnp.zeros(gs.shape, F32)
    for _ in range(TOPK_GROUPS):
        _, ig = _first_index_of_max(gs, g_iota, float(N_EXPERT_GROUPS))
        hit = g_iota == ig
        gself = jnp.where(hit, 1.0, gself)
        gs = jnp.where(hit, neg, gs)
    emask = jnp.concatenate(
        [jnp.broadcast_to(gself[g:g + 1], (EXPERTS_PER_GROUP, tm)) for g in range(N_EXPERT_GROUPS)], axis=0)
    masked = jnp.where(emask > 0.5, sel, neg)
    hits, wts = [], []
    chosen = jnp.zeros((N_EXPERTS, tm), F32)
    for _ in range(TOP_K):
        _, ie = _first_index_of_max(masked, e_iota, float(N_EXPERTS))
        hit = e_iota == ie
        hits.append(hit)
        wts.append(jnp.sum(jnp.where(hit, aff, 0.0), axis=0, keepdims=True))
        chosen = jnp.where(hit, 1.0, chosen)
        masked = jnp.where(hit, neg, masked)
    wsum = wts[0]
    for w in wts[1:]:
        wsum = wsum + w
    scale = ROUTED_SCALE / wsum
    w_at_expert = jnp.zeros((N_EXPERTS, tm), F32)
    for hit, w in zip(hits, wts):
        w_at_expert = jnp.where(hit, w * scale, w_at_expert)

    r_io = lax.broadcasted_iota(jnp.int32, (tm, tm), 0)
    c_io = lax.broadcasted_iota(jnp.int32, (tm, tm), 1)
    tri = (r_io <= c_io).astype(BF16)
    incl = jnp.dot(chosen.astype(BF16), tri, preferred_element_type=F32)
    count = jnp.sum(chosen, axis=1, keepdims=True)
    n_pieces = jnp.floor((count + (PIECE - 1)) * (1.0 / PIECE))
    cnt_ref[0] = n_pieces
    e_r = lax.broadcasted_iota(jnp.int32, (N_EXPERTS, N_EXPERTS), 0)
    e_c = lax.broadcasted_iota(jnp.int32, (N_EXPERTS, N_EXPERTS), 1)
    before = (e_c < e_r).astype(BF16)
    piece_off = jnp.dot(before, jnp.broadcast_to(n_pieces, (N_EXPERTS, LANES)).astype(BF16),
                        preferred_element_type=F32)[:, :1]
    rank1 = chosen * incl
    route_ref[0, 0] = rank1
    route_ref[0, 1] = w_at_expert
    route_ref[0, 2] = jnp.broadcast_to(piece_off, (N_EXPERTS, tm))
    route_ref[0, 3] = jnp.broadcast_to(n_pieces, (N_EXPERTS, tm))

    keep = (pl.program_id(0) < pl.num_programs(0) - 1).astype(F32)
    onehot = _slot_match(rank1, None, piece_off, n_pieces, tm, one=keep)[0].astype(BF16)
    hb = h.astype(BF16)
    d = hb.shape[1]
    nc = 2 * LANES
    for j in range(d // nc):
        xs_ref[0, :, j * nc:(j + 1) * nc] = jnp.dot(
            onehot, hb[:, j * nc:(j + 1) * nc], preferred_element_type=F32).astype(BF16)

    a = jnp.dot(hb, wsg_ref[...], preferred_element_type=F32)
    bb = jnp.dot(hb, wsu_ref[...], preferred_element_type=F32)
    sh_out = jnp.dot((_silu(a) * bb).astype(BF16), wsd_ref[...], preferred_element_type=F32)
    xo_ref[0] = x + gt_ref[0] * sh_out


def _ffn_front(x, xc, sh, sc, gt, g, w_rt, rb, wsg, wsu, wsd):
    b, l, d = x.shape
    tm = MOE_TILE
    nt = l // tm
    n_lat = b * nt
    has_ctx = xc is not None
    ntc = xc.shape[1] // tm if has_ctx else 0
    n_all = n_lat + b * ntc
    full = lambda a: pl.BlockSpec(a.shape, lambda i: (0,) * a.ndim)

    real = lambda i: jnp.minimum(i, n_all - 1)

    def lat_map(i):
        j = jnp.minimum(i, n_lat - 1)
        return (j // nt, j % nt, 0)

    def ctx_map(i):
        j = jnp.clip(i - n_lat, 0, b * ntc - 1)
        return (j // ntc, j % ntc, 0)

    vec = pl.BlockSpec((1, 1, d), lambda i: (jnp.where(real(i) < n_lat, real(i) // nt, b), 0, 0))
    xspecs = [pl.BlockSpec((1, tm, d), lat_map)] + ([pl.BlockSpec((1, tm, d), ctx_map)] if has_ctx else [])
    xargs = [x] + ([xc] if has_ctx else [])
    return pl.pallas_call(
        functools.partial(_ffn_front_kernel, n_lat=n_lat, has_ctx=has_ctx),
        grid=(n_all + 1,),
        in_specs=xspecs + [vec, vec, vec, full(g), full(w_rt), full(rb), full(wsg), full(wsu), full(wsd)],
        out_specs=[pl.BlockSpec((1, tm, d), lambda i: (real(i), 0, 0)),
                   pl.BlockSpec((1, TILE_ROWS, d), lambda i: (i, 0, 0)),
                   pl.BlockSpec((1, 4, N_EXPERTS, tm), lambda i: (real(i), 0, 0, 0)),
                   pl.BlockSpec((1, N_EXPERTS, 1), lambda i: (real(i), 0, 0))],
        out_shape=[jax.ShapeDtypeStruct((n_all, tm, d), F32),
                   jax.ShapeDtypeStruct((n_all + 1, TILE_ROWS, d), BF16),
                   jax.ShapeDtypeStruct((n_all, 4, N_EXPERTS, tm), F32),
                   jax.ShapeDtypeStruct((n_all, N_EXPERTS, 1), F32)],
        compiler_params=_cparams(("arbitrary",)),
        name="ffn_front",
    )(*xargs, sh, sc, gt, g, w_rt, rb, wsg, wsu, wsd)


def _expert_kernel(bexp_ref, nused_ref, rlen_ref, rsrc_ref, rnext_ref, xs_ref, wg_ref, wu_ref, wd_ref, y_ref,
                   state, dst, xbuf, ybuf, wgu_s, wd_s, sem_in, sem_out, *, runs_per_expert, spare_piece):
    i = pl.program_id(0)
    n_used = nused_ref[0]
    end_run = rlen_ref.shape[0] - 1

    def copy_in(pid, slot, p):
        return pltpu.make_async_copy(xs_ref.at[pid], xbuf.at[slot, pl.ds(p * PIECE, PIECE), :], sem_in.at[slot])

    def copy_out(pid, slot, p):
        return pltpu.make_async_copy(ybuf.at[slot, pl.ds(p * PIECE, PIECE), :], y_ref.at[pid], sem_out.at[slot])

    def issue_gather(blk, slot):
        r_lo = bexp_ref[blk] * runs_per_expert
        r_end = r_lo + runs_per_expert
        fresh = state[0] < r_lo
        r = jnp.where(fresh, r_lo, state[0])
        o = jnp.where(fresh, 0, state[1])
        read_pid = jnp.int32(0)
        for p in range(BLOCK_PIECES):
            exhausted = o >= rlen_ref[jnp.minimum(r, end_run)]
            r = jnp.where(exhausted, rnext_ref[jnp.minimum(r + 1, end_run)], r)
            o = jnp.where(exhausted, 0, o)
            valid = r < r_end
            pid = rsrc_ref[jnp.minimum(r, end_run)] + o
            read_pid = jnp.where(valid, pid, read_pid)
            dst[slot, p] = jnp.where(valid, pid, spare_piece + slot * BLOCK_PIECES + p)
            copy_in(read_pid, slot, p).start()
            o = o + valid.astype(jnp.int32)
        state[0] = r
        state[1] = o

    def wait_gather(slot):
        pltpu.make_async_copy(xbuf.at[slot], xbuf.at[slot], sem_in.at[slot]).wait()

    def send_results(slot):
        for p in range(BLOCK_PIECES):
            copy_out(dst[slot, p], slot, p).start()

    def wait_results(slot):
        pltpu.make_async_copy(ybuf.at[slot], ybuf.at[slot], sem_out.at[slot]).wait()

    @pl.when(i < n_used)
    def _():
        slot = i % 2

        @pl.when(i == 0)
        def _():
            state[0] = 0
            state[1] = 0
            issue_gather(0, 0)

        @pl.when(i + 1 < n_used)
        def _():
            issue_gather(i + 1, (i + 1) % 2)

        wait_gather(slot)

        @pl.when(i >= 2)
        def _():
            wait_results(slot)

        f = wg_ref.shape[3]

        @pl.when(jnp.logical_or(i == 0, bexp_ref[i] != bexp_ref[jnp.maximum(i - 1, 0)]))
        def _():
            wgu_s[:, :f] = wg_ref[0, 0].astype(BF16)
            wgu_s[:, f:] = wu_ref[0, 0].astype(BF16)
            wd_s[...] = wd_ref[0, 0].astype(BF16)

        gu = jnp.dot(xbuf[slot], wgu_s[...], preferred_element_type=F32)
        act = (_silu(gu[:, :f]) * gu[:, f:]).astype(BF16)
        ybuf[slot] = jnp.dot(act, wd_s[...], preferred_element_type=F32).astype(BF16)
        send_results(slot)

        @pl.when(i == n_used - 1)
        def _():
            wait_results(slot)

            @pl.when(i >= 1)
            def _():
                wait_results(1 - slot)


def _experts(block_expert, n_used, run_len, run_src, run_next, xs_pieces, w_gate, w_up, w_down, layer,
             runs_per_expert):
    n_blocks = block_expert.shape[0]
    _, _, d = xs_pieces.shape
    f = w_gate.shape[3]
    wspec = lambda shape: pl.BlockSpec((1, 1) + shape, lambda i, be, *_: (layer, be[i], 0, 0))
    grid_spec = pltpu.PrefetchScalarGridSpec(
        num_scalar_prefetch=5,
        grid=(n_blocks,),
        in_specs=[pl.BlockSpec(memory_space=pl.ANY), wspec((d, f)), wspec((d, f)), wspec((f, d))],
        out_specs=pl.BlockSpec(memory_space=pl.ANY),
        scratch_shapes=[pltpu.SMEM((2,), jnp.int32),
                        pltpu.SMEM((2, BLOCK_PIECES), jnp.int32),
                        pltpu.VMEM((2, EXPERT_BLOCK, d), BF16),
                        pltpu.VMEM((2, EXPERT_BLOCK, d), BF16),
                        pltpu.VMEM((d, 2 * f), BF16),
                        pltpu.VMEM((f, d), BF16),
                        pltpu.SemaphoreType.DMA((2,)), pltpu.SemaphoreType.DMA((2,))],
    )
    return pl.pallas_call(
        functools.partial(_expert_kernel, runs_per_expert=runs_per_expert,
                          spare_piece=runs_per_expert * PIECES_PER_TILE),
        grid_spec=grid_spec,
        out_shape=jax.ShapeDtypeStruct(xs_pieces.shape, xs_pieces.dtype),
        input_output_aliases={5: 0},
        compiler_params=_cparams(("arbitrary",)),
        name="experts",
    )(block_expert, n_used, run_len, run_src, run_next, xs_pieces, w_gate, w_up, w_down)


def _combine_kernel(y_ref, route_ref, x_ref, gt_ref, gf_ref, o_ref, *, final):
    tm = x_ref.shape[1]
    match, w_rows = _slot_match(route_ref[0, 0], route_ref[0, 1], route_ref[0, 2][:, :1],
                                route_ref[0, 3][:, :1], tm)
    sel_t = (match * w_rows).astype(BF16)
    routed = lax.dot_general(sel_t, y_ref[0], (((0,), (0,)), ((), ())), preferred_element_type=F32)
    out = x_ref[0] + gt_ref[0] * routed
    if final:
        out = _rms(out, gf_ref[...])
    o_ref[0] = out


def _combine(y_tiles, route, x_tiles, gt, g_final, seq_shape, tile_off, final):
    b, l = seq_shape
    _, tm, d = x_tiles.shape
    nt = l // tm
    tile = lambda bi, i: tile_off + bi * nt + i
    return pl.pallas_call(
        functools.partial(_combine_kernel, final=final),
        grid=(b, nt),
        in_specs=[pl.BlockSpec((1, TILE_ROWS, d), lambda bi, i: (tile(bi, i), 0, 0)),
                  pl.BlockSpec((1, 4, N_EXPERTS, tm), lambda bi, i: (tile(bi, i), 0, 0, 0)),
                  pl.BlockSpec((1, tm, d), lambda bi, i: (tile(bi, i), 0, 0)),
                  pl.BlockSpec((1, 1, d), lambda bi, i: (bi, 0, 0)),
                  pl.BlockSpec(g_final.shape, lambda bi, i: (0, 0))],
        out_specs=pl.BlockSpec((1, tm, d), lambda bi, i: (bi, i, 0)),
        out_shape=jax.ShapeDtypeStruct((b, l, d), F32),
        compiler_params=_cparams(("parallel", "parallel")),
        name="combine",
    )(y_tiles, route, x_tiles, gt, g_final)


def _rope_tables(l):
    rows = l // GRID_W
    row = jnp.repeat(jnp.arange(rows, dtype=F32), GRID_W)
    col = jnp.tile(jnp.arange(GRID_W, dtype=F32), rows)
    inv_freq = ROPE_THETA ** (-jnp.arange(ROPE_FREQS, dtype=F32) * 2.0 / (2 * ROPE_FREQS))
    ar, ac = row[:, None] * inv_freq, col[:, None] * inv_freq
    ones = jnp.ones((l, ROPE_LANE0), F32)
    zpad = jnp.zeros((l, HEAD_PAD - ROPE_LANE0 - QK_ROPE), F32)
    cos_t = jnp.concatenate([ones, jnp.cos(ar), jnp.cos(ar), jnp.cos(ac), jnp.cos(ac), zpad], axis=1)
    sin_t = jnp.concatenate([0.0 * ones, -jnp.sin(ar), jnp.sin(ar), -jnp.sin(ac), jnp.sin(ac), zpad], axis=1)
    return cos_t, sin_t


def _identity_tables(l):
    lane = jnp.arange(HEAD_PAD)
    cos_t = jnp.broadcast_to((lane < ROPE_LANE0 + QK_ROPE).astype(F32), (l, HEAD_PAD))
    return cos_t, jnp.zeros((l, HEAD_PAD), F32)


def _block_diag(w):
    g, a, b = w.shape
    out = jnp.zeros((g * a, g * b), w.dtype)
    for i in range(g):
        out = out.at[i * a:(i + 1) * a, i * b:(i + 1) * b].set(w[i])
    return out


def _prep_layer(w_in, w_pool, w_fno, w_uq, w_ukv):
    d = w_in.shape[0]
    kr = jnp.zeros((d, HEAD_PAD), w_in.dtype).at[:, ROPE_LANE0:ROPE_LANE0 + QK_ROPE].set(w_in[:, COL_KR:])
    w_in_p = jnp.concatenate([w_in[:, :COL_KR], kr], axis=1).astype(BF16)
    qk = QK_NOPE + QK_ROPE
    wq = w_uq.reshape(Q_LORA, N_HEADS, qk)
    wq = jnp.pad(wq, ((0, 0), (0, 0), (0, HEAD_PAD - qk))).reshape(Q_LORA, N_HEADS * HEAD_PAD)
    wkv = w_ukv.reshape(KV_LORA, N_HEADS, QK_NOPE + V_DIM)
    wk = jnp.pad(wkv[..., :QK_NOPE], ((0, 0), (0, 0), (0, HEAD_PAD - QK_NOPE))).reshape(KV_LORA, -1)
    wv = jnp.pad(wkv[..., QK_NOPE:], ((0, 0), (0, 0), (0, V_ROWS - V_DIM))).reshape(KV_LORA, -1)
    return (w_in_p, wq.T.astype(BF16), wk.astype(BF16), wv.T.astype(BF16),
            _block_diag(w_pool).astype(BF16), _block_diag(w_fno).astype(BF16))


def _run_tables(n_pieces):
    n_all = n_pieces.shape[0]
    npc = n_pieces.astype(jnp.int32)
    piece_off = jnp.cumsum(npc, axis=1) - npc
    run_src = (jnp.arange(n_all, dtype=jnp.int32)[:, None] * PIECES_PER_TILE + piece_off).T.reshape(-1)
    run_len = npc.T.reshape(-1)
    n_runs = run_len.shape[0]
    run_id = jnp.arange(n_runs, dtype=jnp.int32)
    run_next = lax.cummin(jnp.where(run_len > 0, run_id, n_runs), axis=0, reverse=True)
    pad1 = lambda a, v: jnp.concatenate([a, jnp.full((1,), v, jnp.int32)])
    run_len, run_src, run_next = pad1(run_len, 0), pad1(run_src, 0), pad1(run_next, n_runs)
    blocks = (jnp.sum(npc, axis=0) + BLOCK_PIECES - 1) // BLOCK_PIECES
    block_end = jnp.cumsum(blocks)
    max_blocks = n_all * PIECES_PER_TILE // BLOCK_PIECES + N_EXPERTS
    blk = jnp.arange(max_blocks, dtype=jnp.int32)
    block_expert = jnp.minimum(jnp.sum((block_end[None, :] <= blk[:, None]).astype(jnp.int32), axis=1),
                               N_EXPERTS - 1)
    return block_expert, block_end[-1:], run_len, run_src, run_next


TM_PROJ = 256
TM_MIX = 256
ATTN_TQ = 2048
ATTN_TK = 512


def kernel(x, c, ctx, c_ctx, w_mod, b_mod, g_mix, g_ffn, w_in, w_pool, pool_scale, w_fno, g_q, w_uq,
           g_kv, w_ukv, w_out, w_router, router_bias, w_gate, w_up, w_down, w_sh_gate, w_sh_up,
           w_sh_down, g_final):
    b, l, d = x.shape
    lc = ctx.shape[1]
    depth = w_mod.shape[0]
    cc = jnp.zeros((SUBLANES, d), F32).at[:b].set(c).at[b].set(c_ctx)
    mods = _modulation(cc, w_mod, b_mod)
    with_transposed = lambda cs: (cs[0], cs[1], cs[0].T, cs[1].T)
    tab_lat = with_transposed(_rope_tables(l))
    tab_ctx = with_transposed(_identity_tables(lc))
    xc = ctx
    row = lambda v: v.reshape(1, -1)
    for li in range(depth):
        last = li == depth - 1
        m = mods[li].reshape(SUBLANES, N_MOD, d)
        lat = [m[:b, j][:, None, :] for j in range(N_MOD)]
        cm = [jnp.broadcast_to(m[b, j][None, None, :], (b, 1, d)) for j in range(N_MOD)]
        w_in_p, wq_t, wk_p, wv_t, wp_bd, wf_bd = _prep_layer(w_in[li], w_pool[li], w_fno[li], w_uq[li], w_ukv[li])
        w_out_b = w_out[li].astype(BF16)
        gq, gkv, gm = row(g_q[li]), row(g_kv[li]), row(g_mix[li])

        up, uf, q, k, v = _in_projection(x, lat[0], lat[1], gm, w_in_p, gq, wq_t, gkv, wk_p, wv_t,
                                         tab_lat, TM_PROJ)
        upc, ufc, qc, kc, vc = _in_projection(xc, cm[0], cm[1], gm, w_in_p, gq, wq_t, gkv, wk_p, wv_t,
                                              tab_ctx, TM_PROJ)
        attn = _attention(q, kc, vc, k, v, tq=ATTN_TQ, tk=ATTN_TK)
        f = _fourier(uf)
        ps = row(pool_scale[li])
        x = _mixer_output(x, lat[2], up, f, attn, wp_bd, ps, wf_bd, w_out_b, TM_MIX)
        if not last:
            attn_c = _attention(qc, kc, vc, tq=ATTN_TQ, tk=ATTN_TK)
            fc = _fourier(ufc)
            xc = _mixer_output(xc, cm[2], upc, fc, attn_c, wp_bd, ps, wf_bd, w_out_b, TM_MIX)

        w_rt = w_router[li].T
        rb = router_bias[li].reshape(N_EXPERTS, 1)
        wsg, wsu, wsd = w_sh_gate[li].astype(BF16), w_sh_up[li].astype(BF16), w_sh_down[li].astype(BF16)
        with_ctx = lambda j: jnp.concatenate([lat[j], cm[j][:1]], axis=0)
        x_tiles, xs, route, n_pieces = _ffn_front(
            x, None if last else xc, with_ctx(3), with_ctx(4), with_ctx(5), row(g_ffn[li]), w_rt, rb, wsg, wsu, wsd)
        n_all = xs.shape[0] - 1
        block_expert, n_used, run_len, run_src, run_next = _run_tables(n_pieces[..., 0])
        y = _experts(block_expert, n_used, run_len, run_src, run_next,
                     xs.reshape((n_all + 1) * PIECES_PER_TILE, PIECE, d), w_gate, w_up, w_down, li, n_all)
        y = y.reshape(n_all + 1, TILE_ROWS, d)
        x = _combine(y, route, x_tiles, lat[5], row(g_final), (b, l), 0, last)
        if not last:
            xc = _combine(y, route, x_tiles, cm[5], row(g_final), (b, lc), b * l // MOE_TILE, False)
    return x
```

```python
import functools
import math

import jax
import jax.numpy as jnp
import numpy as np
from jax import lax
from jax.experimental import pallas as pl
from jax.experimental.pallas import tpu as pltpu

F32 = jnp.float32
BF16 = jnp.bfloat16
HIGHEST = lax.Precision.HIGHEST

EPS = 1e-6
N_MOD = 6
GRID_W = 64
POOL_WINDOWS = (2, 4, 8, 16)
GROUP_DIM = 64
N_GROUPS = 4
MIX_GROUP_WIDTH = N_GROUPS * GROUP_DIM
N_HEADS = 8
QK_NOPE = 64
QK_ROPE = 32
V_DIM = 64
Q_LORA = 384
KV_LORA = 256
ROPE_FREQS = QK_ROPE // 4
ROPE_THETA = 10000.0
SOFTMAX_SCALE = (QK_NOPE + QK_ROPE) ** -0.5
N_EXPERTS = 64
TOP_K = 8
N_EXPERT_GROUPS = 8
TOPK_GROUPS = 4
EXPERTS_PER_GROUP = N_EXPERTS // N_EXPERT_GROUPS
ROUTED_SCALE = 2.5

LANES = 128
SUBLANES = 8
HEAD_PAD = LANES
VMEM_LIMIT = 56 * 1024 * 1024

COL_FNO = MIX_GROUP_WIDTH
COL_Q = COL_FNO + MIX_GROUP_WIDTH
COL_KV = COL_Q + Q_LORA
COL_KR = COL_KV + KV_LORA
IN_PAD = COL_KR + HEAD_PAD
ROPE_LANE0 = QK_NOPE

MOE_TILE = 256
PIECE = 16
PIECES_PER_TILE = MOE_TILE * TOP_K // PIECE + N_EXPERTS
TILE_ROWS = PIECES_PER_TILE * PIECE
BLOCK_PIECES = 32
EXPERT_BLOCK = BLOCK_PIECES * PIECE


def _cparams(sem, vmem=VMEM_LIMIT):
    return pltpu.CompilerParams(dimension_semantics=sem, vmem_limit_bytes=vmem)


def _rms(x, g):
    return x * lax.rsqrt(jnp.mean(x * x, axis=-1, keepdims=True) + EPS) * g


def _silu(x):
    return x * jax.nn.sigmoid(x)


def _mod_kernel(c_ref, w_ref, b_ref, o_ref):
    s = _silu(c_ref[...])
    o_ref[0] = jnp.dot(s, w_ref[0], preferred_element_type=F32, precision=HIGHEST) + b_ref[0]


def _modulation(cc, w_mod, b_mod):
    depth, d, nd = w_mod.shape
    n_chunks = nd // d
    return pl.pallas_call(
        _mod_kernel,
        grid=(depth, n_chunks),
        in_specs=[pl.BlockSpec((SUBLANES, d), lambda l, j: (0, 0)),
                  pl.BlockSpec((1, d, d), lambda l, j: (l, 0, j)),
                  pl.BlockSpec((1, 1, d), lambda l, j: (l, 0, j))],
        out_specs=pl.BlockSpec((1, SUBLANES, d), lambda l, j: (l, 0, j)),
        out_shape=jax.ShapeDtypeStruct((depth, SUBLANES, nd), F32),
        compiler_params=_cparams(("parallel", "parallel")),
        name="modulation",
    )(cc, w_mod, b_mod.reshape(depth, 1, nd))


def _rope(t, c, s):
    lane = lax.broadcasted_iota(jnp.int32, t.shape, 1)
    first_half = (lane % (2 * ROPE_FREQS)) < ROPE_FREQS
    partner = jnp.where(first_half, pltpu.roll(t, LANES - ROPE_FREQS, 1), pltpu.roll(t, ROPE_FREQS, 1))
    return t * c + partner * s


def _rope_rows(t, c, s):
    rw = lax.broadcasted_iota(jnp.int32, t.shape, 0)
    first_half = (rw % (2 * ROPE_FREQS)) < ROPE_FREQS
    n = t.shape[0]
    partner = jnp.where(first_half, pltpu.roll(t, n - ROPE_FREQS, 0), pltpu.roll(t, ROPE_FREQS, 0))
    return t * c + partner * s


V_ROWS = V_DIM + 16


def _inproj_kernel(x_ref, sh_ref, sc_ref, g_ref, win_ref, gq_ref, wuqt_ref, gkv_ref, wuk_ref, wuvt_ref,
                   cos_ref, sin_ref, cost_ref, sint_ref, up_ref, uf_ref, q_ref, k_ref, v_ref):
    x = x_ref[0]
    h = _rms(x, g_ref[...]) * (1.0 + sc_ref[0]) + sh_ref[0]
    u = jnp.dot(h.astype(BF16), win_ref[...], preferred_element_type=F32)
    up_ref[0] = u[:, :COL_FNO]
    uf_ref[0] = u[:, COL_FNO:COL_Q]
    nt = (((1,), (1,)), ((), ()))
    cq = _rms(u[:, COL_Q:COL_KV], gq_ref[...]).astype(BF16)
    qt = lax.dot_general(wuqt_ref[...], cq, nt, preferred_element_type=F32)
    ckv = _rms(u[:, COL_KV:COL_KR], gkv_ref[...]).astype(BF16)
    kn = jnp.dot(ckv, wuk_ref[...], preferred_element_type=F32)
    vt = lax.dot_general(wuvt_ref[...], ckv, nt, preferred_element_type=F32)
    kr = _rope(u[:, COL_KR:IN_PAD], cos_ref[...], sin_ref[...])
    cos_t, sin_t = cost_ref[...], sint_ref[...]
    tm = x.shape[0]
    ones_row = (lax.broadcasted_iota(jnp.int32, (V_ROWS, tm), 0) == V_DIM).astype(F32)
    q_scale = SOFTMAX_SCALE * math.log2(math.e)
    for hd in range(N_HEADS):
        lo = hd * HEAD_PAD
        q_ref[0, hd] = (_rope_rows(qt[lo:lo + HEAD_PAD], cos_t, sin_t) * q_scale).astype(BF16)
        k_ref[0, hd] = (kn[:, lo:lo + HEAD_PAD] + kr).astype(BF16)
        v_ref[0, hd] = (vt[hd * V_ROWS:(hd + 1) * V_ROWS] + ones_row).astype(BF16)


def _in_projection(x, sh, sc, g, w_in_p, g_q, w_uq_t, g_kv, w_uk_p, w_uv_t, tables, tm):
    b, l, d = x.shape
    tm = min(tm, l)
    cos_r, sin_r, cos_c, sin_c = tables
    full = lambda a: pl.BlockSpec(a.shape, lambda bi, i: (0,) * a.ndim)
    vec = pl.BlockSpec((1, 1, d), lambda bi, i: (bi, 0, 0))
    rtab = pl.BlockSpec((tm, HEAD_PAD), lambda bi, i: (i, 0))
    ctab = pl.BlockSpec((HEAD_PAD, tm), lambda bi, i: (0, i))
    return pl.pallas_call(
        _inproj_kernel,
        grid=(b, l // tm),
        in_specs=[pl.BlockSpec((1, tm, d), lambda bi, i: (bi, i, 0)), vec, vec,
                  full(g), full(w_in_p), full(g_q), full(w_uq_t), full(g_kv), full(w_uk_p), full(w_uv_t),
                  rtab, rtab, ctab, ctab],
        out_specs=[pl.BlockSpec((1, tm, MIX_GROUP_WIDTH), lambda bi, i: (bi, i, 0)),
                   pl.BlockSpec((1, tm, MIX_GROUP_WIDTH), lambda bi, i: (bi, i, 0)),
                   pl.BlockSpec((1, N_HEADS, HEAD_PAD, tm), lambda bi, i: (bi, 0, 0, i)),
                   pl.BlockSpec((1, N_HEADS, tm, HEAD_PAD), lambda bi, i: (bi, 0, i, 0)),
                   pl.BlockSpec((1, N_HEADS, V_ROWS, tm), lambda bi, i: (bi, 0, 0, i))],
        out_shape=[jax.ShapeDtypeStruct((b, l, MIX_GROUP_WIDTH), F32),
                   jax.ShapeDtypeStruct((b, l, MIX_GROUP_WIDTH), F32),
                   jax.ShapeDtypeStruct((b, N_HEADS, HEAD_PAD, l), BF16),
                   jax.ShapeDtypeStruct((b, N_HEADS, l, HEAD_PAD), BF16),
                   jax.ShapeDtypeStruct((b, N_HEADS, V_ROWS, l), BF16)],
        compiler_params=_cparams(("parallel", "parallel")),
        name="in_projection",
    )(x, sh, sc, g, w_in_p, g_q, w_uq_t, g_kv, w_uk_p, w_uv_t, cos_r, sin_r, cos_c, sin_c)


ATTN_HEADS_PER_STEP = 2


def _attn_kernel(*refs, tk, n_chunks):
    if n_chunks:
        q_ref, kc_ref, vc_ref, k_ref, v_ref, o_ref, s_scr = refs
    else:
        q_ref, kc_ref, vc_ref, o_ref = refs
    tq = q_ref.shape[3]
    heads = range(ATTN_HEADS_PER_STEP)
    qs = [q_ref[0, hh] for hh in heads]

    def scores(hh, kc):
        return jnp.dot(kc, qs[hh], preferred_element_type=F32)

    def absorb(st, vt, m, acc):
        m_new = jnp.maximum(m, jnp.max(st, axis=0, keepdims=True))
        alpha = jnp.exp2(m - m_new)
        pt = jnp.exp2(st - m_new).astype(BF16)
        return m_new, alpha * acc + jnp.dot(vt, pt, preferred_element_type=F32)

    carry = tuple(absorb(scores(hh, kc_ref[0, hh]), vc_ref[0, hh],
                         jnp.full((1, tq), -1e30, F32), jnp.zeros((V_ROWS, tq), F32)) for hh in heads)
    if n_chunks:
        def produce(slot, chunk):
            off = pl.multiple_of(chunk * tk, tk)
            for hh in heads:
                s_scr[slot, hh] = scores(hh, k_ref[0, hh, pl.ds(off, tk), :])

        def consume(slot, chunk, carry):
            off = pl.multiple_of(chunk * tk, tk)
            return tuple(absorb(s_scr[slot, hh], v_ref[0, hh, :, pl.ds(off, tk)], *carry[hh]) for hh in heads)

        produce(0, 0)

        def body(j, carry):
            a = 2 * j
            produce(1, a + 1)
            carry = consume(0, a, carry)
            produce(0, jnp.minimum(a + 2, n_chunks - 1))
            return consume(1, a + 1, carry)

        carry = lax.fori_loop(0, n_chunks // 2, body, carry)
    for hh in heads:
        acc = carry[hh][1]
        out = acc[:V_DIM] / acc[V_DIM:V_DIM + 1]
        o_ref[0, hh * V_DIM:(hh + 1) * V_DIM, :] = out.astype(o_ref.dtype)


def _attention(q_t, k_ctx, v_ctx_t, k=None, v_t=None, *, tq, tk):
    b, h, _, l = q_t.shape
    lc = k_ctx.shape[2]
    tq = min(tq, l)
    hps = ATTN_HEADS_PER_STEP
    qspec = pl.BlockSpec((1, hps, HEAD_PAD, tq), lambda bi, hi, i: (bi, hi, 0, i))
    kcspec = pl.BlockSpec((1, hps, lc, HEAD_PAD), lambda bi, hi, i: (bi, hi, 0, 0))
    vcspec = pl.BlockSpec((1, hps, V_ROWS, lc), lambda bi, hi, i: (bi, hi, 0, 0))
    args, specs, n_chunks, scratch = [q_t, k_ctx, v_ctx_t], [qspec, kcspec, vcspec], 0, []
    if k is not None:
        lk = k.shape[2]
        tk = min(tk, lk // 2)
        n_chunks = lk // tk
        assert n_chunks % 2 == 0 and n_chunks * tk == lk
        scratch = [pltpu.VMEM((2, hps, tk, tq), F32)]
        once = pl.Buffered(1)
        args += [k, v_t]
        specs += [pl.BlockSpec((1, hps, lk, HEAD_PAD), lambda bi, hi, i: (bi, hi, 0, 0), pipeline_mode=once),
                  pl.BlockSpec((1, hps, V_ROWS, lk), lambda bi, hi, i: (bi, hi, 0, 0), pipeline_mode=once)]
    return pl.pallas_call(
        functools.partial(_attn_kernel, tk=tk, n_chunks=n_chunks),
        grid=(b, h // hps, l // tq),
        in_specs=specs,
        out_specs=pl.BlockSpec((1, hps * V_DIM, tq), lambda bi, hi, i: (bi, hi, i)),
        out_shape=jax.ShapeDtypeStruct((b, h * V_DIM, l), BF16),
        scratch_shapes=scratch,
        compiler_params=_cparams(("parallel", "parallel", "arbitrary")),
        name="attention",
    )(*args)


def _dft_step1_kernel(x_ref, c_ref, s_ref, yr_ref, yi_ref):
    xb = x_ref[0].astype(BF16)
    yr_ref[0] = jnp.dot(c_ref[...], xb, preferred_element_type=F32)
    yi_ref[0] = jnp.dot(s_ref[...], xb, preferred_element_type=F32)


def _dft_step2_kernel(yr_ref, yi_ref, tc_ref, ts_ref, c_ref, s_ref, cc_ref, sc_ref, o_ref):
    for j in range(SUBLANES):
        yr, yi = yr_ref[0, j], yi_ref[0, j]
        tc, ts = tc_ref[j], ts_ref[j]
        zr = (yr * tc - yi * ts).astype(BF16)
        zi = (yi * tc + yr * ts).astype(BF16)
        a = (jnp.dot(c_ref[...], zr, preferred_element_type=F32)
             - jnp.dot(s_ref[...], zi, preferred_element_type=F32))
        bm = (jnp.dot(s_ref[...], zr, preferred_element_type=F32)
              + jnp.dot(c_ref[...], zi, preferred_element_type=F32))
        o_ref[0, :, j, :] = (jnp.dot(a.astype(BF16), cc_ref[...], preferred_element_type=F32)
                             - jnp.dot(bm.astype(BF16), sc_ref[...], preferred_element_type=F32))


def _dft_dense_kernel(x_ref, c_ref, s_ref, cc_ref, sc_ref, o_ref):
    xb = x_ref[0].astype(BF16)
    a = jnp.dot(c_ref[...], xb, preferred_element_type=F32)
    bm = jnp.dot(s_ref[...], xb, preferred_element_type=F32)
    o_ref[0] = (jnp.dot(a.astype(BF16), cc_ref[...], preferred_element_type=F32)
                - jnp.dot(bm.astype(BF16), sc_ref[...], preferred_element_type=F32))


def _dft_mats(n):
    ang = 2.0 * np.pi * np.outer(np.arange(n), np.arange(n)) / n
    return np.cos(ang), np.sin(ang)


def _channel_dft(l):
    c, s = _dft_mats(GROUP_DIM)
    eye = np.eye(N_GROUPS)
    norm = 1.0 / math.sqrt(l * GROUP_DIM)
    return (jnp.asarray(np.kron(eye, c) * norm, BF16), jnp.asarray(np.kron(eye, s) * norm, BF16))


def _dft_factors(l):
    n1 = 1 << (int(math.log2(l)) // 2)
    return n1, l // n1


def _fourier(u_fno, lane_block=4096):
    b, l, c = u_fno.shape
    cc, sc = _channel_dft(l)
    full = lambda a: pl.BlockSpec(a.shape, lambda *_: (0,) * a.ndim)
    if l <= 512:
        cm, sm = _dft_mats(l)
        cm, sm = jnp.asarray(cm, BF16), jnp.asarray(sm, BF16)
        blk = pl.BlockSpec((1, l, c), lambda bi: (bi, 0, 0))
        return pl.pallas_call(
            _dft_dense_kernel, grid=(b,),
            in_specs=[blk, full(cm), full(sm), full(cc), full(sc)],
            out_specs=blk, out_shape=jax.ShapeDtypeStruct((b, l, c), F32),
            compiler_params=_cparams(("parallel",)), name="dft_dense",
        )(u_fno, cm, sm, cc, sc)
    n1, n2 = _dft_factors(l)
    c1, s1 = _dft_mats(n1)
    c2, s2 = _dft_mats(n2)
    c1, s1, c2, s2 = (jnp.asarray(m, BF16) for m in (c1, s1, c2, s2))
    ang = 2.0 * np.pi * np.outer(np.arange(n1), np.arange(n2)) / l
    tc = jnp.asarray(np.cos(ang)[:, :, None], F32)
    ts = jnp.asarray(np.sin(ang)[:, :, None], F32)
    w = n2 * c
    tn = min(lane_block, w)
    x2 = u_fno.reshape(b, n1, w)
    yspec = pl.BlockSpec((1, n1, tn), lambda bi, i: (bi, 0, i))
    yr, yi = pl.pallas_call(
        _dft_step1_kernel, grid=(b, w // tn),
        in_specs=[yspec, full(c1), full(s1)],
        out_specs=[yspec, yspec],
        out_shape=[jax.ShapeDtypeStruct((b, n1, w), F32)] * 2,
        compiler_params=_cparams(("parallel", "parallel")), name="dft_step1",
    )(x2, c1, s1)
    yr = yr.reshape(b, n1, n2, c)
    yi = yi.reshape(b, n1, n2, c)
    slab = pl.BlockSpec((1, SUBLANES, n2, c), lambda bi, i: (bi, i, 0, 0))
    tw = pl.BlockSpec((SUBLANES, n2, 1), lambda bi, i: (i, 0, 0))
    out = pl.pallas_call(
        _dft_step2_kernel, grid=(b, n1 // SUBLANES),
        in_specs=[slab, slab, tw, tw, full(c2), full(s2), full(cc), full(sc)],
        out_specs=pl.BlockSpec((1, n2, SUBLANES, c), lambda bi, i: (bi, 0, i, 0)),
        out_shape=jax.ShapeDtypeStruct((b, n2, n1, c), F32),
        compiler_params=_cparams(("parallel", "parallel")), name="dft_step2",
    )(yr, yi, tc, ts, c2, s2, cc, sc)
    return out.reshape(b, l, c)


POOL_HALO = SUBLANES


def _mixout_kernel(x_ref, gt_ref, up_ref, prev_ref, next_ref, f_ref, at_ref,
                   wp_ref, ps_ref, wf_ref, wo_ref, o_ref, *, seq_len):
    i = pl.program_id(1)
    tm = x_ref.shape[1]
    u = up_ref[0]
    prev = jnp.where(i > 0, prev_ref[0], 0.0)
    nxt = jnp.where(i < pl.num_programs(1) - 1, next_ref[0], 0.0)
    p = jnp.concatenate([prev, u, nxt], axis=0)
    n = tm + 2 * POOL_HALO
    s1 = pltpu.roll(p, 1, 0) + p
    s2 = pltpu.roll(s1, 1, 0) + pltpu.roll(s1, n - 1, 0)
    s4 = pltpu.roll(s2, 2, 0) + pltpu.roll(s2, n - 2, 0)
    s8 = pltpu.roll(s4, 4, 0) + pltpu.roll(s4, n - 4, 0)
    lane = lax.broadcasted_iota(jnp.int32, (tm, MIX_GROUP_WIDTH), 1)
    grp = lane // GROUP_DIM
    lo, hi = POOL_HALO, POOL_HALO + tm
    win = jnp.where(grp == 0, s1[lo:hi],
                    jnp.where(grp == 1, s2[lo:hi], jnp.where(grp == 2, s4[lo:hi], s8[lo:hi])))
    half = jnp.left_shift(1, grp)
    t = i * tm + lax.broadcasted_iota(jnp.int32, (tm, MIX_GROUP_WIDTH), 0)
    cnt = (jnp.minimum(t + half, seq_len) - jnp.maximum(t - half, 0)).astype(F32)
    dlt = win / cnt - u
    pool_y = jnp.dot(dlt.astype(BF16), wp_ref[...], preferred_element_type=F32) * ps_ref[...]
    fno_y = jnp.dot(f_ref[0].astype(BF16), wf_ref[...], preferred_element_type=F32)
    w = MIX_GROUP_WIDTH
    y = (jnp.dot(pool_y.astype(BF16), wo_ref[0:w, :], preferred_element_type=F32)
         + jnp.dot(fno_y.astype(BF16), wo_ref[w:2 * w, :], preferred_element_type=F32)
         + lax.dot_general(at_ref[0], wo_ref[2 * w:, :], (((0,), (0,)), ((), ())),
                           preferred_element_type=F32))
    o_ref[0] = x_ref[0] + gt_ref[0] * y


def _mixer_output(x, gt, u_pool, f, attn, wp_bd, pool_scale, wf_bd, w_out, tm):
    b, l, d = x.shape
    tm = min(tm, l)
    nb = tm // POOL_HALO
    last = l // POOL_HALO - 1
    full = lambda a: pl.BlockSpec(a.shape, lambda bi, i: (0,) * a.ndim)
    w = MIX_GROUP_WIDTH
    return pl.pallas_call(
        functools.partial(_mixout_kernel, seq_len=l),
        grid=(b, l // tm),
        in_specs=[pl.BlockSpec((1, tm, d), lambda bi, i: (bi, i, 0)),
                  pl.BlockSpec((1, 1, d), lambda bi, i: (bi, 0, 0)),
                  pl.BlockSpec((1, tm, w), lambda bi, i: (bi, i, 0)),
                  pl.BlockSpec((1, POOL_HALO, w), lambda bi, i: (bi, jnp.maximum(i * nb - 1, 0), 0)),
                  pl.BlockSpec((1, POOL_HALO, w), lambda bi, i: (bi, jnp.minimum((i + 1) * nb, last), 0)),
                  pl.BlockSpec((1, tm, w), lambda bi, i: (bi, i, 0)),
                  pl.BlockSpec((1, attn.shape[1], tm), lambda bi, i: (bi, 0, i)),
                  full(wp_bd), full(pool_scale), full(wf_bd), full(w_out)],
        out_specs=pl.BlockSpec((1, tm, d), lambda bi, i: (bi, i, 0)),
        out_shape=jax.ShapeDtypeStruct((b, l, d), F32),
        compiler_params=_cparams(("parallel", "arbitrary")),
        name="mixer_output",
    )(x, gt, u_pool, u_pool, u_pool, f, attn, wp_bd, pool_scale, wf_bd, w_out)


def _first_index_of_max(vals, iota, n):
    mx = jnp.max(vals, axis=0, keepdims=True)
    ix = jnp.min(jnp.where(vals == mx, iota, n), axis=0, keepdims=True)
    return mx, ix


def _slot_match(rank1, weights, piece_off, n_pieces, tm, one=1.0):
    row = lax.broadcasted_iota(jnp.int32, (N_EXPERTS, TILE_ROWS), 1).astype(F32)
    lo = piece_off * PIECE
    member = jnp.where(row >= lo, jnp.where(row < lo + n_pieces * PIECE, 1.0, 0.0), 0.0)
    lhs = jnp.concatenate([member, member * PIECE], axis=0).astype(BF16)
    rhs = jnp.concatenate([rank1, jnp.broadcast_to(piece_off, (N_EXPERTS, tm))], axis=0)
    if weights is not None:
        rhs = jnp.concatenate([rhs, jnp.concatenate([weights, jnp.zeros_like(weights)], axis=0)], axis=1)
    res = lax.dot_general(lhs, rhs.astype(BF16), (((0,), (0,)), ((), ())), preferred_element_type=F32)
    slot1 = lax.broadcasted_iota(jnp.int32, (TILE_ROWS, tm), 0).astype(F32) + 1.0
    match = jnp.where(res[:, :tm] == slot1, one, 0.0)
    return match, (res[:, tm:] if weights is not None else None)


def _ffn_front_kernel(*refs, n_lat, has_ctx):
    if has_ctx:
        x_ref, xc_ref = refs[:2]
        refs = refs[2:]
    else:
        x_ref, refs = refs[0], refs[1:]
    (sh_ref, sc_ref, gt_ref, g_ref, wr_ref, rb_ref, wsg_ref, wsu_ref, wsd_ref,
     xo_ref, xs_ref, route_ref, cnt_ref) = refs
    x = x_ref[0]
    if has_ctx:
        x = jnp.where(pl.program_id(0) < n_lat, x, xc_ref[0])
    tm = x.shape[0]
    h = _rms(x, g_ref[...]) * (1.0 + sc_ref[0]) + sh_ref[0]

    logits = lax.dot_general(wr_ref[...], h, (((1,), (1,)), ((), ())),
                             preferred_element_type=F32, precision=HIGHEST)
    aff = jax.nn.sigmoid(logits)
    sel = aff + rb_ref[...]
    e_iota = lax.broadcasted_iota(jnp.int32, (N_EXPERTS, tm), 0).astype(F32)
    neg = jnp.float32(-jnp.inf)
    gscores = []
    for g in range(N_EXPERT_GROUPS):
        blk = sel[g * EXPERTS_PER_GROUP:(g + 1) * EXPERTS_PER_GROUP]
        it = lax.broadcasted_iota(jnp.int32, blk.shape, 0).astype(F32)
        m1, i1 = _first_index_of_max(blk, it, float(EXPERTS_PER_GROUP))
        m2 = jnp.max(jnp.where(it == i1, neg, blk), axis=0, keepdims=True)
        gscores.append(m1 + m2)
    gs = jnp.concatenate(gscores, axis=0)
    g_iota = lax.broadcasted_iota(jnp.int32, gs.shape, 0).astype(F32)
    gself = jnp.zeros(gs.shape, F32)
    for _ in range(TOPK_GROUPS):
        _, ig = _first_index_of_max(gs, g_iota, float(N_EXPERT_GROUPS))
        hit = g_iota == ig
        gself = jnp.where(hit, 1.0, gself)
        gs = jnp.where(hit, neg, gs)
    emask = jnp.concatenate(
        [jnp.broadcast_to(gself[g:g + 1], (EXPERTS_PER_GROUP, tm)) for g in range(N_EXPERT_GROUPS)], axis=0)
    masked = jnp.where(emask > 0.5, sel, neg)
    hits, wts = [], []
    chosen = jnp.zeros((N_EXPERTS, tm), F32)
    for _ in range(TOP_K):
        _, ie = _first_index_of_max(masked, e_iota, float(N_EXPERTS))
        hit = e_iota == ie
        hits.append(hit)
        wts.append(jnp.sum(jnp.where(hit, aff, 0.0), axis=0, keepdims=True))
        chosen = jnp.where(hit, 1.0, chosen)
        masked = jnp.where(hit, neg, masked)
    wsum = wts[0]
    for w in wts[1:]:
        wsum = wsum + w
    scale = ROUTED_SCALE / wsum
    w_at_expert = jnp.zeros((N_EXPERTS, tm), F32)
    for hit, w in zip(hits, wts):
        w_at_expert = jnp.where(hit, w * scale, w_at_expert)

    r_io = lax.broadcasted_iota(jnp.int32, (tm, tm), 0)
    c_io = lax.broadcasted_iota(jnp.int32, (tm, tm), 1)
    tri = (r_io <= c_io).astype(BF16)
    incl = jnp.dot(chosen.astype(BF16), tri, preferred_element_type=F32)
    count = jnp.sum(chosen, axis=1, keepdims=True)
    n_pieces = jnp.floor((count + (PIECE - 1)) * (1.0 / PIECE))
    cnt_ref[0] = n_pieces
    e_r = lax.broadcasted_iota(jnp.int32, (N_EXPERTS, N_EXPERTS), 0)
    e_c = lax.broadcasted_iota(jnp.int32, (N_EXPERTS, N_EXPERTS), 1)
    before = (e_c < e_r).astype(BF16)
    piece_off = jnp.dot(before, jnp.broadcast_to(n_pieces, (N_EXPERTS, LANES)).astype(BF16),
                        preferred_element_type=F32)[:, :1]
    rank1 = chosen * incl
    route_ref[0, 0] = rank1
    route_ref[0, 1] = w_at_expert
    route_ref[0, 2] = jnp.broadcast_to(piece_off, (N_EXPERTS, tm))
    route_ref[0, 3] = jnp.broadcast_to(n_pieces, (N_EXPERTS, tm))

    keep = (pl.program_id(0) < pl.num_programs(0) - 1).astype(F32)
    onehot = _slot_match(rank1, None, piece_off, n_pieces, tm, one=keep)[0].astype(BF16)
    hb = h.astype(BF16)
    d = hb.shape[1]
    nc = 2 * LANES
    for j in range(d // nc):
        xs_ref[0, :, j * nc:(j + 1) * nc] = jnp.dot(
            onehot, hb[:, j * nc:(j + 1) * nc], preferred_element_type=F32).astype(BF16)

    a = jnp.dot(hb, wsg_ref[...], preferred_element_type=F32)
    bb = jnp.dot(hb, wsu_ref[...], preferred_element_type=F32)
    sh_out = jnp.dot((_silu(a) * bb).astype(BF16), wsd_ref[...], preferred_element_type=F32)
    xo_ref[0] = x + gt_ref[0] * sh_out


def _ffn_front(x, xc, sh, sc, gt, g, w_rt, rb, wsg, wsu, wsd):
    b, l, d = x.shape
    tm = MOE_TILE
    nt = l // tm
    n_lat = b * nt
    has_ctx = xc is not None
    ntc = xc.shape[1] // tm if has_ctx else 0
    n_all = n_lat + b * ntc
    full = lambda a: pl.BlockSpec(a.shape, lambda i: (0,) * a.ndim)

    real = lambda i: jnp.minimum(i, n_all - 1)

    def lat_map(i):
        j = jnp.minimum(i, n_lat - 1)
        return (j // nt, j % nt, 0)

    def ctx_map(i):
        j = jnp.clip(i - n_lat, 0, b * ntc - 1)
        return (j // ntc, j % ntc, 0)

    vec = pl.BlockSpec((1, 1, d), lambda i: (jnp.where(real(i) < n_lat, real(i) // nt, b), 0, 0))
    xspecs = [pl.BlockSpec((1, tm, d), lat_map)] + ([pl.BlockSpec((1, tm, d), ctx_map)] if has_ctx else [])
    xargs = [x] + ([xc] if has_ctx else [])
    return pl.pallas_call(
        functools.partial(_ffn_front_kernel, n_lat=n_lat, has_ctx=has_ctx),
        grid=(n_all + 1,),
        in_specs=xspecs + [vec, vec, vec, full(g), full(w_rt), full(rb), full(wsg), full(wsu), full(wsd)],
        out_specs=[pl.BlockSpec((1, tm, d), lambda i: (real(i), 0, 0)),
                   pl.BlockSpec((1, TILE_ROWS, d), lambda i: (i, 0, 0)),
                   pl.BlockSpec((1, 4, N_EXPERTS, tm), lambda i: (real(i), 0, 0, 0)),
                   pl.BlockSpec((1, N_EXPERTS, 1), lambda i: (real(i), 0, 0))],
        out_shape=[jax.ShapeDtypeStruct((n_all, tm, d), F32),
                   jax.ShapeDtypeStruct((n_all + 1, TILE_ROWS, d), BF16),
                   jax.ShapeDtypeStruct((n_all, 4, N_EXPERTS, tm), F32),
                   jax.ShapeDtypeStruct((n_all, N_EXPERTS, 1), F32)],
        compiler_params=_cparams(("arbitrary",)),
        name="ffn_front",
    )(*xargs, sh, sc, gt, g, w_rt, rb, wsg, wsu, wsd)


def _expert_kernel(bexp_ref, nused_ref, rlen_ref, rsrc_ref, rnext_ref, xs_ref, wg_ref, wu_ref, wd_ref, y_ref,
                   state, dst, xbuf, ybuf, wgu_s, wd_s, sem_in, sem_out, *, runs_per_expert, spare_piece):
    i = pl.program_id(0)
    n_used = nused_ref[0]
    end_run = rlen_ref.shape[0] - 1

    def copy_in(pid, slot, p):
        return pltpu.make_async_copy(xs_ref.at[pid], xbuf.at[slot, pl.ds(p * PIECE, PIECE), :], sem_in.at[slot])

    def copy_out(pid, slot, p):
        return pltpu.make_async_copy(ybuf.at[slot, pl.ds(p * PIECE, PIECE), :], y_ref.at[pid], sem_out.at[slot])

    def issue_gather(blk, slot):
        r_lo = bexp_ref[blk] * runs_per_expert
        r_end = r_lo + runs_per_expert
        fresh = state[0] < r_lo
        r = jnp.where(fresh, r_lo, state[0])
        o = jnp.where(fresh, 0, state[1])
        read_pid = jnp.int32(0)
        for p in range(BLOCK_PIECES):
            exhausted = o >= rlen_ref[jnp.minimum(r, end_run)]
            r = jnp.where(exhausted, rnext_ref[jnp.minimum(r + 1, end_run)], r)
            o = jnp.where(exhausted, 0, o)
            valid = r < r_end
            pid = rsrc_ref[jnp.minimum(r, end_run)] + o
            read_pid = jnp.where(valid, pid, read_pid)
            dst[slot, p] = jnp.where(valid, pid, spare_piece + slot * BLOCK_PIECES + p)
            copy_in(read_pid, slot, p).start()
            o = o + valid.astype(jnp.int32)
        state[0] = r
        state[1] = o

    def wait_gather(slot):
        pltpu.make_async_copy(xbuf.at[slot], xbuf.at[slot], sem_in.at[slot]).wait()

    def send_results(slot):
        for p in range(BLOCK_PIECES):
            copy_out(dst[slot, p], slot, p).start()

    def wait_results(slot):
        pltpu.make_async_copy(ybuf.at[slot], ybuf.at[slot], sem_out.at[slot]).wait()

    @pl.when(i < n_used)
    def _():
        slot = i % 2

        @pl.when(i == 0)
        def _():
            state[0] = 0
            state[1] = 0
            issue_gather(0, 0)

        @pl.when(i + 1 < n_used)
        def _():
            issue_gather(i + 1, (i + 1) % 2)

        wait_gather(slot)

        @pl.when(i >= 2)
        def _():
            wait_results(slot)

        f = wg_ref.shape[3]

        @pl.when(jnp.logical_or(i == 0, bexp_ref[i] != bexp_ref[jnp.maximum(i - 1, 0)]))
        def _():
            wgu_s[:, :f] = wg_ref[0, 0].astype(BF16)
            wgu_s[:, f:] = wu_ref[0, 0].astype(BF16)
            wd_s[...] = wd_ref[0, 0].astype(BF16)

        gu = jnp.dot(xbuf[slot], wgu_s[...], preferred_element_type=F32)
        act = (_silu(gu[:, :f]) * gu[:, f:]).astype(BF16)
        ybuf[slot] = jnp.dot(act, wd_s[...], preferred_element_type=F32).astype(BF16)
        send_results(slot)

        @pl.when(i == n_used - 1)
        def _():
            wait_results(slot)

            @pl.when(i >= 1)
            def _():
                wait_results(1 - slot)


def _experts(block_expert, n_used, run_len, run_src, run_next, xs_pieces, w_gate, w_up, w_down, layer,
             runs_per_expert):
    n_blocks = block_expert.shape[0]
    _, _, d = xs_pieces.shape
    f = w_gate.shape[3]
    wspec = lambda shape: pl.BlockSpec((1, 1) + shape, lambda i, be, *_: (layer, be[i], 0, 0))
    grid_spec = pltpu.PrefetchScalarGridSpec(
        num_scalar_prefetch=5,
        grid=(n_blocks,),
        in_specs=[pl.BlockSpec(memory_space=pl.ANY), wspec((d, f)), wspec((d, f)), wspec((f, d))],
        out_specs=pl.BlockSpec(memory_space=pl.ANY),
        scratch_shapes=[pltpu.SMEM((2,), jnp.int32),
                        pltpu.SMEM((2, BLOCK_PIECES), jnp.int32),
                        pltpu.VMEM((2, EXPERT_BLOCK, d), BF16),
                        pltpu.VMEM((2, EXPERT_BLOCK, d), BF16),
                        pltpu.VMEM((d, 2 * f), BF16),
                        pltpu.VMEM((f, d), BF16),
                        pltpu.SemaphoreType.DMA((2,)), pltpu.SemaphoreType.DMA((2,))],
    )
    return pl.pallas_call(
        functools.partial(_expert_kernel, runs_per_expert=runs_per_expert,
                          spare_piece=runs_per_expert * PIECES_PER_TILE),
        grid_spec=grid_spec,
        out_shape=jax.ShapeDtypeStruct(xs_pieces.shape, xs_pieces.dtype),
        input_output_aliases={5: 0},
        compiler_params=_cparams(("arbitrary",)),
        name="experts",
    )(block_expert, n_used, run_len, run_src, run_next, xs_pieces, w_gate, w_up, w_down)


def _combine_kernel(y_ref, route_ref, x_ref, gt_ref, gf_ref, o_ref, *, final):
    tm = x_ref.shape[1]
    match, w_rows = _slot_match(route_ref[0, 0], route_ref[0, 1], route_ref[0, 2][:, :1],
                                route_ref[0, 3][:, :1], tm)
    sel_t = (match * w_rows).astype(BF16)
    routed = lax.dot_general(sel_t, y_ref[0], (((0,), (0,)), ((), ())), preferred_element_type=F32)
    out = x_ref[0] + gt_ref[0] * routed
    if final:
        out = _rms(out, gf_ref[...])
    o_ref[0] = out


def _combine(y_tiles, route, x_tiles, gt, g_final, seq_shape, tile_off, final):
    b, l = seq_shape
    _, tm, d = x_tiles.shape
    nt = l // tm
    tile = lambda bi, i: tile_off + bi * nt + i
    return pl.pallas_call(
        functools.partial(_combine_kernel, final=final),
        grid=(b, nt),
        in_specs=[pl.BlockSpec((1, TILE_ROWS, d), lambda bi, i: (tile(bi, i), 0, 0)),
                  pl.BlockSpec((1, 4, N_EXPERTS, tm), lambda bi, i: (tile(bi, i), 0, 0, 0)),
                  pl.BlockSpec((1, tm, d), lambda bi, i: (tile(bi, i), 0, 0)),
                  pl.BlockSpec((1, 1, d), lambda bi, i: (bi, 0, 0)),
                  pl.BlockSpec(g_final.shape, lambda bi, i: (0, 0))],
        out_specs=pl.BlockSpec((1, tm, d), lambda bi, i: (bi, i, 0)),
        out_shape=jax.ShapeDtypeStruct((b, l, d), F32),
        compiler_params=_cparams(("parallel", "parallel")),
        name="combine",
    )(y_tiles, route, x_tiles, gt, g_final)


def _rope_tables(l):
    rows = l // GRID_W
    row = jnp.repeat(jnp.arange(rows, dtype=F32), GRID_W)
    col = jnp.tile(jnp.arange(GRID_W, dtype=F32), rows)
    inv_freq = ROPE_THETA ** (-jnp.arange(ROPE_FREQS, dtype=F32) * 2.0 / (2 * ROPE_FREQS))
    ar, ac = row[:, None] * inv_freq, col[:, None] * inv_freq
    ones = jnp.ones((l, ROPE_LANE0), F32)
    zpad = jnp.zeros((l, HEAD_PAD - ROPE_LANE0 - QK_ROPE), F32)
    cos_t = jnp.concatenate([ones, jnp.cos(ar), jnp.cos(ar), jnp.cos(ac), jnp.cos(ac), zpad], axis=1)
    sin_t = jnp.concatenate([0.0 * ones, -jnp.sin(ar), jnp.sin(ar), -jnp.sin(ac), jnp.sin(ac), zpad], axis=1)
    return cos_t, sin_t


def _identity_tables(l):
    lane = jnp.arange(HEAD_PAD)
    cos_t = jnp.broadcast_to((lane < ROPE_LANE0 + QK_ROPE).astype(F32), (l, HEAD_PAD))
    return cos_t, jnp.zeros((l, HEAD_PAD), F32)


def _block_diag(w):
    g, a, b = w.shape
    out = jnp.zeros((g * a, g * b), w.dtype)
    for i in range(g):
        out = out.at[i * a:(i + 1) * a, i * b:(i + 1) * b].set(w[i])
    return out


def _prep_layer(w_in, w_pool, w_fno, w_uq, w_ukv):
    d = w_in.shape[0]
    kr = jnp.zeros((d, HEAD_PAD), w_in.dtype).at[:, ROPE_LANE0:ROPE_LANE0 + QK_ROPE].set(w_in[:, COL_KR:])
    w_in_p = jnp.concatenate([w_in[:, :COL_KR], kr], axis=1).astype(BF16)
    qk = QK_NOPE + QK_ROPE
    wq = w_uq.reshape(Q_LORA, N_HEADS, qk)
    wq = jnp.pad(wq, ((0, 0), (0, 0), (0, HEAD_PAD - qk))).reshape(Q_LORA, N_HEADS * HEAD_PAD)
    wkv = w_ukv.reshape(KV_LORA, N_HEADS, QK_NOPE + V_DIM)
    wk = jnp.pad(wkv[..., :QK_NOPE], ((0, 0), (0, 0), (0, HEAD_PAD - QK_NOPE))).reshape(KV_LORA, -1)
    wv = jnp.pad(wkv[..., QK_NOPE:], ((0, 0), (0, 0), (0, V_ROWS - V_DIM))).reshape(KV_LORA, -1)
    return (w_in_p, wq.T.astype(BF16), wk.astype(BF16), wv.T.astype(BF16),
            _block_diag(w_pool).astype(BF16), _block_diag(w_fno).astype(BF16))


def _run_tables(n_pieces):
    n_all = n_pieces.shape[0]
    npc = n_pieces.astype(jnp.int32)
    piece_off = jnp.cumsum(npc, axis=1) - npc
    run_src = (jnp.arange(n_all, dtype=jnp.int32)[:, None] * PIECES_PER_TILE + piece_off).T.reshape(-1)
    run_len = npc.T.reshape(-1)
    n_runs = run_len.shape[0]
    run_id = jnp.arange(n_runs, dtype=jnp.int32)
    run_next = lax.cummin(jnp.where(run_len > 0, run_id, n_runs), axis=0, reverse=True)
    pad1 = lambda a, v: jnp.concatenate([a, jnp.full((1,), v, jnp.int32)])
    run_len, run_src, run_next = pad1(run_len, 0), pad1(run_src, 0), pad1(run_next, n_runs)
    blocks = (jnp.sum(npc, axis=0) + BLOCK_PIECES - 1) // BLOCK_PIECES
    block_end = jnp.cumsum(blocks)
    max_blocks = n_all * PIECES_PER_TILE // BLOCK_PIECES + N_EXPERTS
    blk = jnp.arange(max_blocks, dtype=jnp.int32)
    block_expert = jnp.minimum(jnp.sum((block_end[None, :] <= blk[:, None]).astype(jnp.int32), axis=1),
                               N_EXPERTS - 1)
    return block_expert, block_end[-1:], run_len, run_src, run_next


TM_PROJ = 512
TM_MIX = 512
ATTN_TQ = 2048
ATTN_TK = 512


def kernel(x, c, ctx, c_ctx, w_mod, b_mod, g_mix, g_ffn, w_in, w_pool, pool_scale, w_fno, g_q, w_uq,
           g_kv, w_ukv, w_out, w_router, router_bias, w_gate, w_up, w_down, w_sh_gate, w_sh_up,
           w_sh_down, g_final):
    b, l, d = x.shape
    lc = ctx.shape[1]
    depth = w_mod.shape[0]
    cc = jnp.zeros((SUBLANES, d), F32).at[:b].set(c).at[b].set(c_ctx)
    mods = _modulation(cc, w_mod, b_mod)
    with_transposed = lambda cs: (cs[0], cs[1], cs[0].T, cs[1].T)
    tab_lat = with_transposed(_rope_tables(l))
    tab_ctx = with_transposed(_identity_tables(lc))
    xc = ctx
    row = lambda v: v.reshape(1, -1)
    for li in range(depth):
        last = li == depth - 1
        m = mods[li].reshape(SUBLANES, N_MOD, d)
        lat = [m[:b, j][:, None, :] for j in range(N_MOD)]
        cm = [jnp.broadcast_to(m[b, j][None, None, :], (b, 1, d)) for j in range(N_MOD)]
        w_in_p, wq_t, wk_p, wv_t, wp_bd, wf_bd = _prep_layer(w_in[li], w_pool[li], w_fno[li], w_uq[li], w_ukv[li])
        w_out_b = w_out[li].astype(BF16)
        gq, gkv, gm = row(g_q[li]), row(g_kv[li]), row(g_mix[li])

        up, uf, q, k, v = _in_projection(x, lat[0], lat[1], gm, w_in_p, gq, wq_t, gkv, wk_p, wv_t,
                                         tab_lat, TM_PROJ)
        upc, ufc, qc, kc, vc = _in_projection(xc, cm[0], cm[1], gm, w_in_p, gq, wq_t, gkv, wk_p, wv_t,
                                              tab_ctx, TM_PROJ)
        attn = _attention(q, kc, vc, k, v, tq=ATTN_TQ, tk=ATTN_TK)
        f = _fourier(uf)
        ps = row(pool_scale[li])
        x = _mixer_output(x, lat[2], up, f, attn, wp_bd, ps, wf_bd, w_out_b, TM_MIX)
        if not last:
            attn_c = _attention(qc, kc, vc, tq=ATTN_TQ, tk=ATTN_TK)
            fc = _fourier(ufc)
            xc = _mixer_output(xc, cm[2], upc, fc, attn_c, wp_bd, ps, wf_bd, w_out_b, TM_MIX)

        w_rt = w_router[li].T
        rb = router_bias[li].reshape(N_EXPERTS, 1)
        wsg, wsu, wsd = w_sh_gate[li].astype(BF16), w_sh_up[li].astype(BF16), w_sh_down[li].astype(BF16)
        with_ctx = lambda j: jnp.concatenate([lat[j], cm[j][:1]], axis=0)
        x_tiles, xs, route, n_pieces = _ffn_front(
            x, None if last else xc, with_ctx(3), with_ctx(4), with_ctx(5), row(g_ffn[li]), w_rt, rb, wsg, wsu, wsd)
        n_all = xs.shape[0] - 1
        block_expert, n_used, run_len, run_src, run_next = _run_tables(n_pieces[..., 0])
        y = _experts(block_expert, n_used, run_len, run_src, run_next,
                     xs.reshape((n_all + 1) * PIECES_PER_TILE, PIECE, d), w_gate, w_up, w_down, li, n_all)
        y = y.reshape(n_all + 1, TILE_ROWS, d)
        x = _combine(y, route, x_tiles, lat[5], row(g_final), (b, l), 0, last)
        if not last:
            xc = _combine(y, route, x_tiles, cm[5], row(g_final), (b, lc), b * l // MOE_TILE, False)
    return x
```

```python
import functools
import math

import jax
import jax.numpy as jnp
import numpy as np
from jax import lax
from jax.experimental import pallas as pl
from jax.experimental.pallas import tpu as pltpu

F32 = jnp.float32
BF16 = jnp.bfloat16
HIGHEST = lax.Precision.HIGHEST

EPS = 1e-6
N_MOD = 6
GRID_W = 64
POOL_WINDOWS = (2, 4, 8, 16)
GROUP_DIM = 64
N_GROUPS = 4
MIX_GROUP_WIDTH = N_GROUPS * GROUP_DIM
N_HEADS = 8
QK_NOPE = 64
QK_ROPE = 32
V_DIM = 64
Q_LORA = 384
KV_LORA = 256
ROPE_FREQS = QK_ROPE // 4
ROPE_THETA = 10000.0
SOFTMAX_SCALE = (QK_NOPE + QK_ROPE) ** -0.5
N_EXPERTS = 64
TOP_K = 8
N_EXPERT_GROUPS = 8
TOPK_GROUPS = 4
EXPERTS_PER_GROUP = N_EXPERTS // N_EXPERT_GROUPS
ROUTED_SCALE = 2.5

LANES = 128
SUBLANES = 8
HEAD_PAD = LANES
VMEM_LIMIT = 56 * 1024 * 1024

COL_FNO = MIX_GROUP_WIDTH
COL_Q = COL_FNO + MIX_GROUP_WIDTH
COL_KV = COL_Q + Q_LORA
COL_KR = COL_KV + KV_LORA
IN_PAD = COL_KR + HEAD_PAD
ROPE_LANE0 = QK_NOPE

MOE_TILE = 256
PIECE = 16
PIECES_PER_TILE = MOE_TILE * TOP_K // PIECE + N_EXPERTS
TILE_ROWS = PIECES_PER_TILE * PIECE
BLOCK_PIECES = 32
EXPERT_BLOCK = BLOCK_PIECES * PIECE


def _cparams(sem, vmem=VMEM_LIMIT):
    return pltpu.CompilerParams(dimension_semantics=sem, vmem_limit_bytes=vmem)


def _rms(x, g):
    return x * lax.rsqrt(jnp.mean(x * x, axis=-1, keepdims=True) + EPS) * g


def _silu(x):
    return x * jax.nn.sigmoid(x)


def _mod_kernel(c_ref, w_ref, b_ref, o_ref):
    s = _silu(c_ref[...])
    o_ref[0] = jnp.dot(s, w_ref[0], preferred_element_type=F32, precision=HIGHEST) + b_ref[0]


def _modulation(cc, w_mod, b_mod):
    depth, d, nd = w_mod.shape
    n_chunks = nd // d
    return pl.pallas_call(
        _mod_kernel,
        grid=(depth, n_chunks),
        in_specs=[pl.BlockSpec((SUBLANES, d), lambda l, j: (0, 0)),
                  pl.BlockSpec((1, d, d), lambda l, j: (l, 0, j)),
                  pl.BlockSpec((1, 1, d), lambda l, j: (l, 0, j))],
        out_specs=pl.BlockSpec((1, SUBLANES, d), lambda l, j: (l, 0, j)),
        out_shape=jax.ShapeDtypeStruct((depth, SUBLANES, nd), F32),
        compiler_params=_cparams(("parallel", "parallel")),
        name="modulation",
    )(cc, w_mod, b_mod.reshape(depth, 1, nd))


def _rope(t, c, s):
    lane = lax.broadcasted_iota(jnp.int32, t.shape, 1)
    first_half = (lane % (2 * ROPE_FREQS)) < ROPE_FREQS
    partner = jnp.where(first_half, pltpu.roll(t, LANES - ROPE_FREQS, 1), pltpu.roll(t, ROPE_FREQS, 1))
    return t * c + partner * s


def _rope_rows(t, c, s):
    rw = lax.broadcasted_iota(jnp.int32, t.shape, 0)
    first_half = (rw % (2 * ROPE_FREQS)) < ROPE_FREQS
    n = t.shape[0]
    partner = jnp.where(first_half, pltpu.roll(t, n - ROPE_FREQS, 0), pltpu.roll(t, ROPE_FREQS, 0))
    return t * c + partner * s


V_ROWS = V_DIM + 16


def _inproj_kernel(x_ref, sh_ref, sc_ref, g_ref, win_ref, gq_ref, wuqt_ref, gkv_ref, wuk_ref, wuvt_ref,
                   cos_ref, sin_ref, cost_ref, sint_ref, up_ref, uf_ref, q_ref, k_ref, v_ref):
    x = x_ref[0]
    h = _rms(x, g_ref[...]) * (1.0 + sc_ref[0]) + sh_ref[0]
    u = jnp.dot(h.astype(BF16), win_ref[...], preferred_element_type=F32)
    up_ref[0] = u[:, :COL_FNO]
    uf_ref[0] = u[:, COL_FNO:COL_Q]
    nt = (((1,), (1,)), ((), ()))
    cq = _rms(u[:, COL_Q:COL_KV], gq_ref[...]).astype(BF16)
    qt = lax.dot_general(wuqt_ref[...], cq, nt, preferred_element_type=F32)
    ckv = _rms(u[:, COL_KV:COL_KR], gkv_ref[...]).astype(BF16)
    kn = jnp.dot(ckv, wuk_ref[...], preferred_element_type=F32)
    vt = lax.dot_general(wuvt_ref[...], ckv, nt, preferred_element_type=F32)
    kr = _rope(u[:, COL_KR:IN_PAD], cos_ref[...], sin_ref[...])
    cos_t, sin_t = cost_ref[...], sint_ref[...]
    tm = x.shape[0]
    ones_row = (lax.broadcasted_iota(jnp.int32, (V_ROWS, tm), 0) == V_DIM).astype(F32)
    q_scale = SOFTMAX_SCALE * math.log2(math.e)
    for hd in range(N_HEADS):
        lo = hd * HEAD_PAD
        q_ref[0, hd] = (_rope_rows(qt[lo:lo + HEAD_PAD], cos_t, sin_t) * q_scale).astype(BF16)
        k_ref[0, hd] = (kn[:, lo:lo + HEAD_PAD] + kr).astype(BF16)
        v_ref[0, hd] = (vt[hd * V_ROWS:(hd + 1) * V_ROWS] + ones_row).astype(BF16)


def _in_projection(x, sh, sc, g, w_in_p, g_q, w_uq_t, g_kv, w_uk_p, w_uv_t, tables, tm):
    b, l, d = x.shape
    tm = min(tm, l)
    cos_r, sin_r, cos_c, sin_c = tables
    full = lambda a: pl.BlockSpec(a.shape, lambda bi, i: (0,) * a.ndim)
    vec = pl.BlockSpec((1, 1, d), lambda bi, i: (bi, 0, 0))
    rtab = pl.BlockSpec((tm, HEAD_PAD), lambda bi, i: (i, 0))
    ctab = pl.BlockSpec((HEAD_PAD, tm), lambda bi, i: (0, i))
    return pl.pallas_call(
        _inproj_kernel,
        grid=(b, l // tm),
        in_specs=[pl.BlockSpec((1, tm, d), lambda bi, i: (bi, i, 0)), vec, vec,
                  full(g), full(w_in_p), full(g_q), full(w_uq_t), full(g_kv), full(w_uk_p), full(w_uv_t),
                  rtab, rtab, ctab, ctab],
        out_specs=[pl.BlockSpec((1, tm, MIX_GROUP_WIDTH), lambda bi, i: (bi, i, 0)),
                   pl.BlockSpec((1, tm, MIX_GROUP_WIDTH), lambda bi, i: (bi, i, 0)),
                   pl.BlockSpec((1, N_HEADS, HEAD_PAD, tm), lambda bi, i: (bi, 0, 0, i)),
                   pl.BlockSpec((1, N_HEADS, tm, HEAD_PAD), lambda bi, i: (bi, 0, i, 0)),
                   pl.BlockSpec((1, N_HEADS, V_ROWS, tm), lambda bi, i: (bi, 0, 0, i))],
        out_shape=[jax.ShapeDtypeStruct((b, l, MIX_GROUP_WIDTH), F32),
                   jax.ShapeDtypeStruct((b, l, MIX_GROUP_WIDTH), F32),
                   jax.ShapeDtypeStruct((b, N_HEADS, HEAD_PAD, l), BF16),
                   jax.ShapeDtypeStruct((b, N_HEADS, l, HEAD_PAD), BF16),
                   jax.ShapeDtypeStruct((b, N_HEADS, V_ROWS, l), BF16)],
        compiler_params=_cparams(("parallel", "parallel")),
        name="in_projection",
    )(x, sh, sc, g, w_in_p, g_q, w_uq_t, g_kv, w_uk_p, w_uv_t, cos_r, sin_r, cos_c, sin_c)


ATTN_HEADS_PER_STEP = 2


def _attn_kernel(*refs, tk, n_chunks):
    if n_chunks:
        q_ref, kc_ref, vc_ref, k_ref, v_ref, o_ref, s_scr = refs
    else:
        q_ref, kc_ref, vc_ref, o_ref = refs
    tq = q_ref.shape[3]
    heads = range(ATTN_HEADS_PER_STEP)
    qs = [q_ref[0, hh] for hh in heads]

    def scores(hh, kc):
        return jnp.dot(kc, qs[hh], preferred_element_type=F32)

    def absorb(st, vt, m, acc):
        m_new = jnp.maximum(m, jnp.max(st, axis=0, keepdims=True))
        alpha = jnp.exp2(m - m_new)
        pt = jnp.exp2(st - m_new).astype(BF16)
        return m_new, alpha * acc + jnp.dot(vt, pt, preferred_element_type=F32)

    carry = tuple(absorb(scores(hh, kc_ref[0, hh]), vc_ref[0, hh],
                         jnp.full((1, tq), -1e30, F32), jnp.zeros((V_ROWS, tq), F32)) for hh in heads)
    if n_chunks:
        def produce(slot, chunk):
            off = pl.multiple_of(chunk * tk, tk)
            for hh in heads:
                s_scr[slot, hh] = scores(hh, k_ref[0, hh, pl.ds(off, tk), :])

        def consume(slot, chunk, carry):
            off = pl.multiple_of(chunk * tk, tk)
            return tuple(absorb(s_scr[slot, hh], v_ref[0, hh, :, pl.ds(off, tk)], *carry[hh]) for hh in heads)

        produce(0, 0)

        def body(j, carry):
            a = 2 * j
            produce(1, a + 1)
            carry = consume(0, a, carry)
            produce(0, jnp.minimum(a + 2, n_chunks - 1))
            return consume(1, a + 1, carry)

        carry = lax.fori_loop(0, n_chunks // 2, body, carry)
    for hh in heads:
        acc = carry[hh][1]
        out = acc[:V_DIM] / acc[V_DIM:V_DIM + 1]
        o_ref[0, hh * V_DIM:(hh + 1) * V_DIM, :] = out.astype(o_ref.dtype)


def _attention(q_t, k_ctx, v_ctx_t, k=None, v_t=None, *, tq, tk):
    b, h, _, l = q_t.shape
    lc = k_ctx.shape[2]
    tq = min(tq, l)
    hps = ATTN_HEADS_PER_STEP
    qspec = pl.BlockSpec((1, hps, HEAD_PAD, tq), lambda bi, hi, i: (bi, hi, 0, i))
    kcspec = pl.BlockSpec((1, hps, lc, HEAD_PAD), lambda bi, hi, i: (bi, hi, 0, 0))
    vcspec = pl.BlockSpec((1, hps, V_ROWS, lc), lambda bi, hi, i: (bi, hi, 0, 0))
    args, specs, n_chunks, scratch = [q_t, k_ctx, v_ctx_t], [qspec, kcspec, vcspec], 0, []
    if k is not None:
        lk = k.shape[2]
        tk = min(tk, lk // 2)
        n_chunks = lk // tk
        assert n_chunks % 2 == 0 and n_chunks * tk == lk
        scratch = [pltpu.VMEM((2, hps, tk, tq), F32)]
        once = pl.Buffered(1)
        args += [k, v_t]
        specs += [pl.BlockSpec((1, hps, lk, HEAD_PAD), lambda bi, hi, i: (bi, hi, 0, 0), pipeline_mode=once),
                  pl.BlockSpec((1, hps, V_ROWS, lk), lambda bi, hi, i: (bi, hi, 0, 0), pipeline_mode=once)]
    return pl.pallas_call(
        functools.partial(_attn_kernel, tk=tk, n_chunks=n_chunks),
        grid=(b, h // hps, l // tq),
        in_specs=specs,
        out_specs=pl.BlockSpec((1, hps * V_DIM, tq), lambda bi, hi, i: (bi, hi, i)),
        out_shape=jax.ShapeDtypeStruct((b, h * V_DIM, l), BF16),
        scratch_shapes=scratch,
        compiler_params=_cparams(("parallel", "parallel", "arbitrary")),
        name="attention",
    )(*args)


def _dft_step1_kernel(x_ref, c_ref, s_ref, yr_ref, yi_ref):
    xb = x_ref[0].astype(BF16)
    yr = jnp.dot(c_ref[...], xb, preferred_element_type=F32)
    yi = jnp.dot(s_ref[...], xb, preferred_element_type=F32)
    c = yr_ref.shape[3]
    for j in range(yr_ref.shape[2]):
        yr_ref[0, :, j, :] = yr[:, j * c:(j + 1) * c]
        yi_ref[0, :, j, :] = yi[:, j * c:(j + 1) * c]


def _dft_step2_kernel(yr_ref, yi_ref, tc_ref, ts_ref, c_ref, s_ref, cc_ref, sc_ref, o_ref):
    for j in range(SUBLANES):
        yr, yi = yr_ref[0, j], yi_ref[0, j]
        tc, ts = tc_ref[j], ts_ref[j]
        zr = (yr * tc - yi * ts).astype(BF16)
        zi = (yi * tc + yr * ts).astype(BF16)
        a = (jnp.dot(c_ref[...], zr, preferred_element_type=F32)
             - jnp.dot(s_ref[...], zi, preferred_element_type=F32))
        bm = (jnp.dot(s_ref[...], zr, preferred_element_type=F32)
              + jnp.dot(c_ref[...], zi, preferred_element_type=F32))
        o_ref[0, :, j, :] = (jnp.dot(a.astype(BF16), cc_ref[...], preferred_element_type=F32)
                             - jnp.dot(bm.astype(BF16), sc_ref[...], preferred_element_type=F32))


def _dft_dense_kernel(x_ref, c_ref, s_ref, cc_ref, sc_ref, o_ref):
    xb = x_ref[0].astype(BF16)
    a = jnp.dot(c_ref[...], xb, preferred_element_type=F32)
    bm = jnp.dot(s_ref[...], xb, preferred_element_type=F32)
    o_ref[0] = (jnp.dot(a.astype(BF16), cc_ref[...], preferred_element_type=F32)
                - jnp.dot(bm.astype(BF16), sc_ref[...], preferred_element_type=F32))


def _dft_mats(n):
    ang = 2.0 * np.pi * np.outer(np.arange(n), np.arange(n)) / n
    return np.cos(ang), np.sin(ang)


def _channel_dft(l):
    c, s = _dft_mats(GROUP_DIM)
    eye = np.eye(N_GROUPS)
    norm = 1.0 / math.sqrt(l * GROUP_DIM)
    return (jnp.asarray(np.kron(eye, c) * norm, BF16), jnp.asarray(np.kron(eye, s) * norm, BF16))


def _dft_factors(l):
    n1 = 1 << (int(math.log2(l)) // 2)
    return n1, l // n1


def _fourier(u_fno, lane_block=4096):
    b, l, c = u_fno.shape
    cc, sc = _channel_dft(l)
    full = lambda a: pl.BlockSpec(a.shape, lambda *_: (0,) * a.ndim)
    if l <= 512:
        cm, sm = _dft_mats(l)
        cm, sm = jnp.asarray(cm, BF16), jnp.asarray(sm, BF16)
        blk = pl.BlockSpec((1, l, c), lambda bi: (bi, 0, 0))
        return pl.pallas_call(
            _dft_dense_kernel, grid=(b,),
            in_specs=[blk, full(cm), full(sm), full(cc), full(sc)],
            out_specs=blk, out_shape=jax.ShapeDtypeStruct((b, l, c), F32),
            compiler_params=_cparams(("parallel",)), name="dft_dense",
        )(u_fno, cm, sm, cc, sc)
    n1, n2 = _dft_factors(l)
    c1, s1 = _dft_mats(n1)
    c2, s2 = _dft_mats(n2)
    c1, s1, c2, s2 = (jnp.asarray(m, BF16) for m in (c1, s1, c2, s2))
    ang = 2.0 * np.pi * np.outer(np.arange(n1), np.arange(n2)) / l
    tc = jnp.asarray(np.cos(ang)[:, :, None], F32)
    ts = jnp.asarray(np.sin(ang)[:, :, None], F32)
    w = n2 * c
    tn = min(lane_block, w)
    x2 = u_fno.reshape(b, n1, w)
    yspec = pl.BlockSpec((1, n1, tn), lambda bi, i: (bi, 0, i))
    yr, yi = pl.pallas_call(
        _dft_step1_kernel, grid=(b, w // tn),
        in_specs=[yspec, full(c1), full(s1)],
        out_specs=[pl.BlockSpec((1, n1, tn // c, c), lambda bi, i: (bi, 0, i, 0))] * 2,
        out_shape=[jax.ShapeDtypeStruct((b, n1, n2, c), F32)] * 2,
        compiler_params=_cparams(("parallel", "parallel")), name="dft_step1",
    )(x2, c1, s1)
    slab = pl.BlockSpec((1, SUBLANES, n2, c), lambda bi, i: (bi, i, 0, 0))
    tw = pl.BlockSpec((SUBLANES, n2, 1), lambda bi, i: (i, 0, 0))
    out = pl.pallas_call(
        _dft_step2_kernel, grid=(b, n1 // SUBLANES),
        in_specs=[slab, slab, tw, tw, full(c2), full(s2), full(cc), full(sc)],
        out_specs=pl.BlockSpec((1, n2, SUBLANES, c), lambda bi, i: (bi, 0, i, 0)),
        out_shape=jax.ShapeDtypeStruct((b, n2, n1, c), F32),
        compiler_params=_cparams(("parallel", "parallel")), name="dft_step2",
    )(yr, yi, tc, ts, c2, s2, cc, sc)
    return out.reshape(b, l, c)


POOL_HALO = SUBLANES


def _mixout_kernel(x_ref, gt_ref, up_ref, prev_ref, next_ref, f_ref, at_ref,
                   wp_ref, ps_ref, wf_ref, wo_ref, o_ref, *, seq_len):
    i = pl.program_id(1)
    tm = x_ref.shape[1]
    u = up_ref[0]
    prev = jnp.where(i > 0, prev_ref[0], 0.0)
    nxt = jnp.where(i < pl.num_programs(1) - 1, next_ref[0], 0.0)
    p = jnp.concatenate([prev, u, nxt], axis=0)
    n = tm + 2 * POOL_HALO
    s1 = pltpu.roll(p, 1, 0) + p
    s2 = pltpu.roll(s1, 1, 0) + pltpu.roll(s1, n - 1, 0)
    s4 = pltpu.roll(s2, 2, 0) + pltpu.roll(s2, n - 2, 0)
    s8 = pltpu.roll(s4, 4, 0) + pltpu.roll(s4, n - 4, 0)
    lane = lax.broadcasted_iota(jnp.int32, (tm, MIX_GROUP_WIDTH), 1)
    grp = lane // GROUP_DIM
    lo, hi = POOL_HALO, POOL_HALO + tm
    win = jnp.where(grp == 0, s1[lo:hi],
                    jnp.where(grp == 1, s2[lo:hi], jnp.where(grp == 2, s4[lo:hi], s8[lo:hi])))
    half = jnp.left_shift(1, grp)
    t = i * tm + lax.broadcasted_iota(jnp.int32, (tm, MIX_GROUP_WIDTH), 0)
    cnt = (jnp.minimum(t + half, seq_len) - jnp.maximum(t - half, 0)).astype(F32)
    dlt = win / cnt - u
    pool_y = jnp.dot(dlt.astype(BF16), wp_ref[...], preferred_element_type=F32) * ps_ref[...]
    fno_y = jnp.dot(f_ref[0].astype(BF16), wf_ref[...], preferred_element_type=F32)
    w = MIX_GROUP_WIDTH
    y = (jnp.dot(pool_y.astype(BF16), wo_ref[0:w, :], preferred_element_type=F32)
         + jnp.dot(fno_y.astype(BF16), wo_ref[w:2 * w, :], preferred_element_type=F32)
         + lax.dot_general(at_ref[0], wo_ref[2 * w:, :], (((0,), (0,)), ((), ())),
                           preferred_element_type=F32))
    o_ref[0] = x_ref[0] + gt_ref[0] * y


def _mixer_output(x, gt, u_pool, f, attn, wp_bd, pool_scale, wf_bd, w_out, tm):
    b, l, d = x.shape
    tm = min(tm, l)
    nb = tm // POOL_HALO
    last = l // POOL_HALO - 1
    full = lambda a: pl.BlockSpec(a.shape, lambda bi, i: (0,) * a.ndim)
    w = MIX_GROUP_WIDTH
    return pl.pallas_call(
        functools.partial(_mixout_kernel, seq_len=l),
        grid=(b, l // tm),
        in_specs=[pl.BlockSpec((1, tm, d), lambda bi, i: (bi, i, 0)),
                  pl.BlockSpec((1, 1, d), lambda bi, i: (bi, 0, 0)),
                  pl.BlockSpec((1, tm, w), lambda bi, i: (bi, i, 0)),
                  pl.BlockSpec((1, POOL_HALO, w), lambda bi, i: (bi, jnp.maximum(i * nb - 1, 0), 0)),
                  pl.BlockSpec((1, POOL_HALO, w), lambda bi, i: (bi, jnp.minimum((i + 1) * nb, last), 0)),
                  pl.BlockSpec((1, tm, w), lambda bi, i: (bi, i, 0)),
                  pl.BlockSpec((1, attn.shape[1], tm), lambda bi, i: (bi, 0, i)),
                  full(wp_bd), full(pool_scale), full(wf_bd), full(w_out)],
        out_specs=pl.BlockSpec((1, tm, d), lambda bi, i: (bi, i, 0)),
        out_shape=jax.ShapeDtypeStruct((b, l, d), F32),
        compiler_params=_cparams(("parallel", "arbitrary")),
        name="mixer_output",
    )(x, gt, u_pool, u_pool, u_pool, f, attn, wp_bd, pool_scale, wf_bd, w_out)


def _first_index_of_max(vals, iota, n):
    mx = jnp.max(vals, axis=0, keepdims=True)
    ix = jnp.min(jnp.where(vals == mx, iota, n), axis=0, keepdims=True)
    return mx, ix


def _slot_match(rank1, weights, piece_off, n_pieces, tm, one=1.0):
    row = lax.broadcasted_iota(jnp.int32, (N_EXPERTS, TILE_ROWS), 1).astype(F32)
    lo = piece_off * PIECE
    member = jnp.where(row >= lo, jnp.where(row < lo + n_pieces * PIECE, 1.0, 0.0), 0.0)
    lhs = jnp.concatenate([member, member * PIECE], axis=0).astype(BF16)
    rhs = jnp.concatenate([rank1, jnp.broadcast_to(piece_off, (N_EXPERTS, tm))], axis=0)
    if weights is not None:
        rhs = jnp.concatenate([rhs, jnp.concatenate([weights, jnp.zeros_like(weights)], axis=0)], axis=1)
    res = lax.dot_general(lhs, rhs.astype(BF16), (((0,), (0,)), ((), ())), preferred_element_type=F32)
    slot1 = lax.broadcasted_iota(jnp.int32, (TILE_ROWS, tm), 0).astype(F32) + 1.0
    match = jnp.where(res[:, :tm] == slot1, one, 0.0)
    return match, (res[:, tm:] if weights is not None else None)


def _ffn_front_kernel(*refs, n_lat, has_ctx):
    if has_ctx:
        x_ref, xc_ref = refs[:2]
        refs = refs[2:]
    else:
        x_ref, refs = refs[0], refs[1:]
    (sh_ref, sc_ref, gt_ref, g_ref, wr_ref, rb_ref, wsg_ref, wsu_ref, wsd_ref,
     xo_ref, xs_ref, route_ref, cnt_ref) = refs
    x = x_ref[0]
    if has_ctx:
        x = jnp.where(pl.program_id(0) < n_lat, x, xc_ref[0])
    tm = x.shape[0]
    h = _rms(x, g_ref[...]) * (1.0 + sc_ref[0]) + sh_ref[0]

    logits = lax.dot_general(wr_ref[...], h, (((1,), (1,)), ((), ())),
                             preferred_element_type=F32, precision=HIGHEST)
    aff = jax.nn.sigmoid(logits)
    sel = aff + rb_ref[...]
    e_iota = lax.broadcasted_iota(jnp.int32, (N_EXPERTS, tm), 0).astype(F32)
    neg = jnp.float32(-jnp.inf)
    gscores = []
    for g in range(N_EXPERT_GROUPS):
        blk = sel[g * EXPERTS_PER_GROUP:(g + 1) * EXPERTS_PER_GROUP]
        it = lax.broadcasted_iota(jnp.int32, blk.shape, 0).astype(F32)
        m1, i1 = _first_index_of_max(blk, it, float(EXPERTS_PER_GROUP))
        m2 = jnp.max(jnp.where(it == i1, neg, blk), axis=0, keepdims=True)
        gscores.append(m1 + m2)
    gs = jnp.concatenate(gscores, axis=0)
    g_iota = lax.broadcasted_iota(jnp.int32, gs.shape, 0).astype(F32)
    gself = jnp.zeros(gs.shape, F32)
    for _ in range(TOPK_GROUPS):
        _, ig = _first_index_of_max(gs, g_iota, float(N_EXPERT_GROUPS))
        hit = g_iota == ig
        gself = jnp.where(hit, 1.0, gself)
        gs = jnp.where(hit, neg, gs)
    emask = jnp.concatenate(
        [jnp.broadcast_to(gself[g:g + 1], (EXPERTS_PER_GROUP, tm)) for g in range(N_EXPERT_GROUPS)], axis=0)
    masked = jnp.where(emask > 0.5, sel, neg)
    hits, wts = [], []
    chosen = jnp.zeros((N_EXPERTS, tm), F32)
    for _ in range(TOP_K):
        _, ie = _first_index_of_max(masked, e_iota, float(N_EXPERTS))
        hit = e_iota == ie
        hits.append(hit)
        wts.append(jnp.sum(jnp.where(hit, aff, 0.0), axis=0, keepdims=True))
        chosen = jnp.where(hit, 1.0, chosen)
        masked = jnp.where(hit, neg, masked)
    wsum = wts[0]
    for w in wts[1:]:
        wsum = wsum + w
    scale = ROUTED_SCALE / wsum
    w_at_expert = jnp.zeros((N_EXPERTS, tm), F32)
    for hit, w in zip(hits, wts):
        w_at_expert = jnp.where(hit, w * scale, w_at_expert)

    r_io = lax.broadcasted_iota(jnp.int32, (tm, tm), 0)
    c_io = lax.broadcasted_iota(jnp.int32, (tm, tm), 1)
    tri = (r_io <= c_io).astype(BF16)
    incl = jnp.dot(chosen.astype(BF16), tri, preferred_element_type=F32)
    count = jnp.sum(chosen, axis=1, keepdims=True)
    n_pieces = jnp.floor((count + (PIECE - 1)) * (1.0 / PIECE))
    cnt_ref[0] = n_pieces
    e_r = lax.broadcasted_iota(jnp.int32, (N_EXPERTS, N_EXPERTS), 0)
    e_c = lax.broadcasted_iota(jnp.int32, (N_EXPERTS, N_EXPERTS), 1)
    before = (e_c < e_r).astype(BF16)
    piece_off = jnp.dot(before, jnp.broadcast_to(n_pieces, (N_EXPERTS, LANES)).astype(BF16),
                        preferred_element_type=F32)[:, :1]
    rank1 = chosen * incl
    route_ref[0, 0] = rank1
    route_ref[0, 1] = w_at_expert
    route_ref[0, 2] = jnp.broadcast_to(piece_off, (N_EXPERTS, tm))
    route_ref[0, 3] = jnp.broadcast_to(n_pieces, (N_EXPERTS, tm))

    keep = (pl.program_id(0) < pl.num_programs(0) - 1).astype(F32)
    onehot = _slot_match(rank1, None, piece_off, n_pieces, tm, one=keep)[0].astype(BF16)
    hb = h.astype(BF16)
    d = hb.shape[1]
    nc = 2 * LANES
    for j in range(d // nc):
        xs_ref[0, :, j * nc:(j + 1) * nc] = jnp.dot(
            onehot, hb[:, j * nc:(j + 1) * nc], preferred_element_type=F32).astype(BF16)

    a = jnp.dot(hb, wsg_ref[...], preferred_element_type=F32)
    bb = jnp.dot(hb, wsu_ref[...], preferred_element_type=F32)
    sh_out = jnp.dot((_silu(a) * bb).astype(BF16), wsd_ref[...], preferred_element_type=F32)
    xo_ref[0] = x + gt_ref[0] * sh_out


def _ffn_front(x, xc, sh, sc, gt, g, w_rt, rb, wsg, wsu, wsd):
    b, l, d = x.shape
    tm = MOE_TILE
    nt = l // tm
    n_lat = b * nt
    has_ctx = xc is not None
    ntc = xc.shape[1] // tm if has_ctx else 0
    n_all = n_lat + b * ntc
    full = lambda a: pl.BlockSpec(a.shape, lambda i: (0,) * a.ndim)

    real = lambda i: jnp.minimum(i, n_all - 1)

    def lat_map(i):
        j = jnp.minimum(i, n_lat - 1)
        return (j // nt, j % nt, 0)

    def ctx_map(i):
        j = jnp.clip(i - n_lat, 0, b * ntc - 1)
        return (j // ntc, j % ntc, 0)

    vec = pl.BlockSpec((1, 1, d), lambda i: (jnp.where(real(i) < n_lat, real(i) // nt, b), 0, 0))
    xspecs = [pl.BlockSpec((1, tm, d), lat_map)] + ([pl.BlockSpec((1, tm, d), ctx_map)] if has_ctx else [])
    xargs = [x] + ([xc] if has_ctx else [])
    return pl.pallas_call(
        functools.partial(_ffn_front_kernel, n_lat=n_lat, has_ctx=has_ctx),
        grid=(n_all + 1,),
        in_specs=xspecs + [vec, vec, vec, full(g), full(w_rt), full(rb), full(wsg), full(wsu), full(wsd)],
        out_specs=[pl.BlockSpec((1, tm, d), lambda i: (real(i), 0, 0)),
                   pl.BlockSpec((1, TILE_ROWS, d), lambda i: (i, 0, 0)),
                   pl.BlockSpec((1, 4, N_EXPERTS, tm), lambda i: (real(i), 0, 0, 0)),
                   pl.BlockSpec((1, N_EXPERTS, 1), lambda i: (real(i), 0, 0))],
        out_shape=[jax.ShapeDtypeStruct((n_all, tm, d), F32),
                   jax.ShapeDtypeStruct((n_all + 1, TILE_ROWS, d), BF16),
                   jax.ShapeDtypeStruct((n_all, 4, N_EXPERTS, tm), F32),
                   jax.ShapeDtypeStruct((n_all, N_EXPERTS, 1), F32)],
        compiler_params=_cparams(("arbitrary",)),
        name="ffn_front",
    )(*xargs, sh, sc, gt, g, w_rt, rb, wsg, wsu, wsd)


def _expert_kernel(bexp_ref, nused_ref, rlen_ref, rsrc_ref, rnext_ref, xs_ref, wg_ref, wu_ref, wd_ref, y_ref,
                   state, dst, xbuf, ybuf, wgu_s, wd_s, sem_in, sem_out, *, runs_per_expert, spare_piece):
    i = pl.program_id(0)
    n_used = nused_ref[0]
    end_run = rlen_ref.shape[0] - 1

    def copy_in(pid, slot, p):
        return pltpu.make_async_copy(xs_ref.at[pid], xbuf.at[slot, pl.ds(p * PIECE, PIECE), :], sem_in.at[slot])

    def copy_out(pid, slot, p):
        return pltpu.make_async_copy(ybuf.at[slot, pl.ds(p * PIECE, PIECE), :], y_ref.at[pid], sem_out.at[slot])

    def issue_gather(blk, slot):
        r_lo = bexp_ref[blk] * runs_per_expert
        r_end = r_lo + runs_per_expert
        fresh = state[0] < r_lo
        r = jnp.where(fresh, r_lo, state[0])
        o = jnp.where(fresh, 0, state[1])
        read_pid = jnp.int32(0)
        for p in range(BLOCK_PIECES):
            exhausted = o >= rlen_ref[jnp.minimum(r, end_run)]
            r = jnp.where(exhausted, rnext_ref[jnp.minimum(r + 1, end_run)], r)
            o = jnp.where(exhausted, 0, o)
            valid = r < r_end
            pid = rsrc_ref[jnp.minimum(r, end_run)] + o
            read_pid = jnp.where(valid, pid, read_pid)
            dst[slot, p] = jnp.where(valid, pid, spare_piece + slot * BLOCK_PIECES + p)
            copy_in(read_pid, slot, p).start()
            o = o + valid.astype(jnp.int32)
        state[0] = r
        state[1] = o

    def wait_gather(slot):
        pltpu.make_async_copy(xbuf.at[slot], xbuf.at[slot], sem_in.at[slot]).wait()

    def send_results(slot):
        for p in range(BLOCK_PIECES):
            copy_out(dst[slot, p], slot, p).start()

    def wait_results(slot):
        pltpu.make_async_copy(ybuf.at[slot], ybuf.at[slot], sem_out.at[slot]).wait()

    @pl.when(i < n_used)
    def _():
        slot = i % 2

        @pl.when(i == 0)
        def _():
            state[0] = 0
            state[1] = 0
            issue_gather(0, 0)

        @pl.when(i + 1 < n_used)
        def _():
            issue_gather(i + 1, (i + 1) % 2)

        wait_gather(slot)

        @pl.when(i >= 2)
        def _():
            wait_results(slot)

        f = wg_ref.shape[3]

        @pl.when(jnp.logical_or(i == 0, bexp_ref[i] != bexp_ref[jnp.maximum(i - 1, 0)]))
        def _():
            wgu_s[:, :f] = wg_ref[0, 0].astype(BF16)
            wgu_s[:, f:] = wu_ref[0, 0].astype(BF16)
            wd_s[...] = wd_ref[0, 0].astype(BF16)

        gu = jnp.dot(xbuf[slot], wgu_s[...], preferred_element_type=F32)
        act = (_silu(gu[:, :f]) * gu[:, f:]).astype(BF16)
        ybuf[slot] = jnp.dot(act, wd_s[...], preferred_element_type=F32).astype(BF16)
        send_results(slot)

        @pl.when(i == n_used - 1)
        def _():
            wait_results(slot)

            @pl.when(i >= 1)
            def _():
                wait_results(1 - slot)


def _experts(block_expert, n_used, run_len, run_src, run_next, xs_pieces, w_gate, w_up, w_down, layer,
             runs_per_expert):
    n_blocks = block_expert.shape[0]
    _, _, d = xs_pieces.shape
    f = w_gate.shape[3]
    wspec = lambda shape: pl.BlockSpec((1, 1) + shape, lambda i, be, *_: (layer, be[i], 0, 0))
    grid_spec = pltpu.PrefetchScalarGridSpec(
        num_scalar_prefetch=5,
        grid=(n_blocks,),
        in_specs=[pl.BlockSpec(memory_space=pl.ANY), wspec((d, f)), wspec((d, f)), wspec((f, d))],
        out_specs=pl.BlockSpec(memory_space=pl.ANY),
        scratch_shapes=[pltpu.SMEM((2,), jnp.int32),
                        pltpu.SMEM((2, BLOCK_PIECES), jnp.int32),
                        pltpu.VMEM((2, EXPERT_BLOCK, d), BF16),
                        pltpu.VMEM((2, EXPERT_BLOCK, d), BF16),
                        pltpu.VMEM((d, 2 * f), BF16),
                        pltpu.VMEM((f, d), BF16),
                        pltpu.SemaphoreType.DMA((2,)), pltpu.SemaphoreType.DMA((2,))],
    )
    return pl.pallas_call(
        functools.partial(_expert_kernel, runs_per_expert=runs_per_expert,
                          spare_piece=runs_per_expert * PIECES_PER_TILE),
        grid_spec=grid_spec,
        out_shape=jax.ShapeDtypeStruct(xs_pieces.shape, xs_pieces.dtype),
        input_output_aliases={5: 0},
        compiler_params=_cparams(("arbitrary",)),
        name="experts",
    )(block_expert, n_used, run_len, run_src, run_next, xs_pieces, w_gate, w_up, w_down)


def _combine_kernel(y_ref, route_ref, x_ref, gt_ref, gf_ref, o_ref, *, final):
    tm = x_ref.shape[1]
    match, w_rows = _slot_match(route_ref[0, 0], route_ref[0, 1], route_ref[0, 2][:, :1],
                                route_ref[0, 3][:, :1], tm)
    sel_t = (match * w_rows).astype(BF16)
    routed = lax.dot_general(sel_t, y_ref[0], (((0,), (0,)), ((), ())), preferred_element_type=F32)
    out = x_ref[0] + gt_ref[0] * routed
    if final:
        out = _rms(out, gf_ref[...])
    o_ref[0] = out


def _combine(y_tiles, route, x_tiles, gt, g_final, seq_shape, tile_off, final):
    b, l = seq_shape
    _, tm, d = x_tiles.shape
    nt = l // tm
    tile = lambda bi, i: tile_off + bi * nt + i
    return pl.pallas_call(
        functools.partial(_combine_kernel, final=final),
        grid=(b, nt),
        in_specs=[pl.BlockSpec((1, TILE_ROWS, d), lambda bi, i: (tile(bi, i), 0, 0)),
                  pl.BlockSpec((1, 4, N_EXPERTS, tm), lambda bi, i: (tile(bi, i), 0, 0, 0)),
                  pl.BlockSpec((1, tm, d), lambda bi, i: (tile(bi, i), 0, 0)),
                  pl.BlockSpec((1, 1, d), lambda bi, i: (bi, 0, 0)),
                  pl.BlockSpec(g_final.shape, lambda bi, i: (0, 0))],
        out_specs=pl.BlockSpec((1, tm, d), lambda bi, i: (bi, i, 0)),
        out_shape=jax.ShapeDtypeStruct((b, l, d), F32),
        compiler_params=_cparams(("parallel", "parallel")),
        name="combine",
    )(y_tiles, route, x_tiles, gt, g_final)


def _rope_tables(l):
    rows = l // GRID_W
    row = jnp.repeat(jnp.arange(rows, dtype=F32), GRID_W)
    col = jnp.tile(jnp.arange(GRID_W, dtype=F32), rows)
    inv_freq = ROPE_THETA ** (-jnp.arange(ROPE_FREQS, dtype=F32) * 2.0 / (2 * ROPE_FREQS))
    ar, ac = row[:, None] * inv_freq, col[:, None] * inv_freq
    ones = jnp.ones((l, ROPE_LANE0), F32)
    zpad = jnp.zeros((l, HEAD_PAD - ROPE_LANE0 - QK_ROPE), F32)
    cos_t = jnp.concatenate([ones, jnp.cos(ar), jnp.cos(ar), jnp.cos(ac), jnp.cos(ac), zpad], axis=1)
    sin_t = jnp.concatenate([0.0 * ones, -jnp.sin(ar), jnp.sin(ar), -jnp.sin(ac), jnp.sin(ac), zpad], axis=1)
    return cos_t, sin_t


def _identity_tables(l):
    lane = jnp.arange(HEAD_PAD)
    cos_t = jnp.broadcast_to((lane < ROPE_LANE0 + QK_ROPE).astype(F32), (l, HEAD_PAD))
    return cos_t, jnp.zeros((l, HEAD_PAD), F32)


def _block_diag(w):
    g, a, b = w.shape
    out = jnp.zeros((g * a, g * b), w.dtype)
    for i in range(g):
        out = out.at[i * a:(i + 1) * a, i * b:(i + 1) * b].set(w[i])
    return out


def _prep_layer(w_in, w_pool, w_fno, w_uq, w_ukv):
    d = w_in.shape[0]
    kr = jnp.zeros((d, HEAD_PAD), w_in.dtype).at[:, ROPE_LANE0:ROPE_LANE0 + QK_ROPE].set(w_in[:, COL_KR:])
    w_in_p = jnp.concatenate([w_in[:, :COL_KR], kr], axis=1).astype(BF16)
    qk = QK_NOPE + QK_ROPE
    wq = w_uq.reshape(Q_LORA, N_HEADS, qk)
    wq = jnp.pad(wq, ((0, 0), (0, 0), (0, HEAD_PAD - qk))).reshape(Q_LORA, N_HEADS * HEAD_PAD)
    wkv = w_ukv.reshape(KV_LORA, N_HEADS, QK_NOPE + V_DIM)
    wk = jnp.pad(wkv[..., :QK_NOPE], ((0, 0), (0, 0), (0, HEAD_PAD - QK_NOPE))).reshape(KV_LORA, -1)
    wv = jnp.pad(wkv[..., QK_NOPE:], ((0, 0), (0, 0), (0, V_ROWS - V_DIM))).reshape(KV_LORA, -1)
    return (w_in_p, wq.T.astype(BF16), wk.astype(BF16), wv.T.astype(BF16),
            _block_diag(w_pool).astype(BF16), _block_diag(w_fno).astype(BF16))


def _run_tables(n_pieces):
    n_all = n_pieces.shape[0]
    npc = n_pieces.astype(jnp.int32)
    piece_off = jnp.cumsum(npc, axis=1) - npc
    run_src = (jnp.arange(n_all, dtype=jnp.int32)[:, None] * PIECES_PER_TILE + piece_off).T.reshape(-1)
    run_len = npc.T.reshape(-1)
    n_runs = run_len.shape[0]
    run_id = jnp.arange(n_runs, dtype=jnp.int32)
    run_next = lax.cummin(jnp.where(run_len > 0, run_id, n_runs), axis=0, reverse=True)
    pad1 = lambda a, v: jnp.concatenate([a, jnp.full((1,), v, jnp.int32)])
    run_len, run_src, run_next = pad1(run_len, 0), pad1(run_src, 0), pad1(run_next, n_runs)
    blocks = (jnp.sum(npc, axis=0) + BLOCK_PIECES - 1) // BLOCK_PIECES
    block_end = jnp.cumsum(blocks)
    max_blocks = n_all * PIECES_PER_TILE // BLOCK_PIECES + N_EXPERTS
    blk = jnp.arange(max_blocks, dtype=jnp.int32)
    block_expert = jnp.minimum(jnp.sum((block_end[None, :] <= blk[:, None]).astype(jnp.int32), axis=1),
                               N_EXPERTS - 1)
    return block_expert, block_end[-1:], run_len, run_src, run_next


TM_PROJ = 512
TM_MIX = 512
ATTN_TQ = 2048
ATTN_TK = 512


def kernel(x, c, ctx, c_ctx, w_mod, b_mod, g_mix, g_ffn, w_in, w_pool, pool_scale, w_fno, g_q, w_uq,
           g_kv, w_ukv, w_out, w_router, router_bias, w_gate, w_up, w_down, w_sh_gate, w_sh_up,
           w_sh_down, g_final):
    b, l, d = x.shape
    lc = ctx.shape[1]
    depth = w_mod.shape[0]
    cc = jnp.zeros((SUBLANES, d), F32).at[:b].set(c).at[b].set(c_ctx)
    mods = _modulation(cc, w_mod, b_mod)
    with_transposed = lambda cs: (cs[0], cs[1], cs[0].T, cs[1].T)
    tab_lat = with_transposed(_rope_tables(l))
    tab_ctx = with_transposed(_identity_tables(lc))
    xc = ctx
    row = lambda v: v.reshape(1, -1)
    for li in range(depth):
        last = li == depth - 1
        m = mods[li].reshape(SUBLANES, N_MOD, d)
        lat = [m[:b, j][:, None, :] for j in range(N_MOD)]
        cm = [jnp.broadcast_to(m[b, j][None, None, :], (b, 1, d)) for j in range(N_MOD)]
        w_in_p, wq_t, wk_p, wv_t, wp_bd, wf_bd = _prep_layer(w_in[li], w_pool[li], w_fno[li], w_uq[li], w_ukv[li])
        w_out_b = w_out[li].astype(BF16)
        gq, gkv, gm = row(g_q[li]), row(g_kv[li]), row(g_mix[li])

        up, uf, q, k, v = _in_projection(x, lat[0], lat[1], gm, w_in_p, gq, wq_t, gkv, wk_p, wv_t,
                                         tab_lat, TM_PROJ)
        upc, ufc, qc, kc, vc = _in_projection(xc, cm[0], cm[1], gm, w_in_p, gq, wq_t, gkv, wk_p, wv_t,
                                              tab_ctx, TM_PROJ)
        attn = _attention(q, kc, vc, k, v, tq=ATTN_TQ, tk=ATTN_TK)
        f = _fourier(uf)
        ps = row(pool_scale[li])
        x = _mixer_output(x, lat[2], up, f, attn, wp_bd, ps, wf_bd, w_out_b, TM_MIX)
        if not last:
            attn_c = _attention(qc, kc, vc, tq=ATTN_TQ, tk=ATTN_TK)
            fc = _fourier(ufc)
            xc = _mixer_output(xc, cm[2], upc, fc, attn_c, wp_bd, ps, wf_bd, w_out_b, TM_MIX)

        w_rt = w_router[li].T
        rb = router_bias[li].reshape(N_EXPERTS, 1)
        wsg, wsu, wsd = w_sh_gate[li].astype(BF16), w_sh_up[li].astype(BF16), w_sh_down[li].astype(BF16)
        with_ctx = lambda j: jnp.concatenate([lat[j], cm[j][:1]], axis=0)
        x_tiles, xs, route, n_pieces = _ffn_front(
            x, None if last else xc, with_ctx(3), with_ctx(4), with_ctx(5), row(g_ffn[li]), w_rt, rb, wsg, wsu, wsd)
        n_all = xs.shape[0] - 1
        block_expert, n_used, run_len, run_src, run_next = _run_tables(n_pieces[..., 0])
        y = _experts(block_expert, n_used, run_len, run_src, run_next,
                     xs.reshape((n_all + 1) * PIECES_PER_TILE, PIECE, d), w_gate, w_up, w_down, li, n_all)
        y = y.reshape(n_all + 1, TILE_ROWS, d)
        x = _combine(y, route, x_tiles, lat[5], row(g_final), (b, l), 0, last)
        if not last:
            xc = _combine(y, route, x_tiles, cm[5], row(g_final), (b, lc), b * l // MOE_TILE, False)
    return x
```
